```python
import math
import jax, jax.numpy as jnp
from jax import lax
import numpy as np

D_MODEL = 2048
BATCH = 1
SEQ = 8192
DEPTH = 1

NSA_HEADS = 16
NSA_KV_GROUPS = 4
NSA_HEAD_DIM = 128
CMP_BLOCK = 32
CMP_STRIDE = 16
SEL_BLOCK = 64
SEL_TOPK = 16
WINDOW = 512
Q_BLOCK = 128
ROPE_THETA = 10000.0
FORCED_SCORE = 1.0e4
SSM_EXPAND = 2
SSM_D_INNER = SSM_EXPAND * D_MODEL
SSM_HEAD_DIM = 64
SSM_HEADS = SSM_D_INNER // SSM_HEAD_DIM
SSM_GROUPS = 8
SSM_STATE = 128
CONV_WIDTH = 4
SSD_CHUNK = 128
XBC_W = SSM_D_INNER + 2 * SSM_GROUPS * SSM_STATE
D_FF = 4 * D_MODEL
EPS = 1e-6
NEG_INF = -1e30
TINY = 1e-30

Q_W = NSA_HEADS * NSA_HEAD_DIM
KV_W = NSA_KV_GROUPS * NSA_HEAD_DIM
NSA_GATE_W = 3 * NSA_HEADS
IN_SIZES = (Q_W, KV_W, KV_W, KV_W, KV_W, KV_W, KV_W, NSA_GATE_W,
            SSM_D_INNER, XBC_W, SSM_HEADS, 2 * D_MODEL)
IN_W = sum(IN_SIZES)

kernel_name = "nsa_mamba2_gated_hybrid_block"


def _split_points():
    points, acc = [], 0
    for s in IN_SIZES[:-1]:
        acc += s
        points.append(acc)
    return points


def rms_norm(x, w):
    xf = x.astype(jnp.float32)
    y = xf * lax.rsqrt(jnp.mean(xf * xf, axis=-1, keepdims=True) + EPS)
    return (y * w.astype(jnp.float32)).astype(x.dtype)


def rope_tables(positions, dim):
    inv = ROPE_THETA ** (-jnp.arange(0, dim, 2, dtype=jnp.float32) / dim)
    ang = positions.astype(jnp.float32)[..., None] * inv
    return jnp.cos(ang), jnp.sin(ang)


def apply_rope(x, cos, sin):
    x1, x2 = jnp.split(x.astype(jnp.float32), 2, axis=-1)
    c, s = cos[:, :, None, :], sin[:, :, None, :]
    return jnp.concatenate([x1 * c - x2 * s, x2 * c + x1 * s], axis=-1).astype(x.dtype)


def masked_softmax(s, mask):
    s = jnp.where(mask, s.astype(jnp.float32), NEG_INF)
    m = jnp.max(s, axis=-1, keepdims=True)
    p = jnp.exp(s - m) * mask
    return p / jnp.maximum(jnp.sum(p, axis=-1, keepdims=True), TINY)


def nsa_mixer(q, k_c, v_c, k_s, v_s, k_w, v_w, gate_logits, cos, sin,
              cmp_pos_k, cmp_pos_v, cmp_k_w1, cmp_k_w2, cmp_v_w1, cmp_v_w2):
    Bsz, S = q.shape[:2]
    G, R, Dh = NSA_KV_GROUPS, NSA_HEADS // NSA_KV_GROUPS, NSA_HEAD_DIM
    scale = Dh ** -0.5
    t = jnp.arange(S)
    nb = S // Q_BLOCK

    q = apply_rope(q.reshape(Bsz, S, NSA_HEADS, Dh), cos, sin)
    q = q.reshape(Bsz, S, G, R, Dh).transpose(0, 2, 3, 1, 4)

    n_cmp = (S - CMP_BLOCK) // CMP_STRIDE + 1
    cmp_idx = jnp.arange(n_cmp)[:, None] * CMP_STRIDE + jnp.arange(CMP_BLOCK)[None, :]
    cmp_end = cmp_idx[:, -1]

    def compress(kv, pos_emb, w1, w2):
        blocks = kv.reshape(Bsz, S, G, Dh)[:, cmp_idx] + pos_emb[:, None, :]
        flat = blocks.transpose(0, 1, 3, 2, 4).reshape(Bsz, n_cmp, G, CMP_BLOCK * Dh)
        return jax.nn.silu(flat @ w1) @ w2

    kc = apply_rope(compress(k_c, cmp_pos_k, cmp_k_w1, cmp_k_w2), cos[:, cmp_end], sin[:, cmp_end])
    kc = kc.transpose(0, 2, 1, 3)
    vc = compress(v_c, cmp_pos_v, cmp_v_w1, cmp_v_w2).transpose(0, 2, 1, 3)
    s_c = jnp.einsum('bgrsd,bgcd->bgrsc', q, kc) * scale
    p_c = masked_softmax(s_c, cmp_end[None, :] <= t[:, None])
    o_c = jnp.einsum('bgrsc,bgcd->bgrsd', p_c.astype(vc.dtype), vc)

    n_sel = S // SEL_BLOCK
    k_top = min(SEL_TOPK, n_sel)
    c_start = jnp.arange(n_cmp) * CMP_STRIDE
    s_start = jnp.arange(n_sel) * SEL_BLOCK
    overlap = jnp.clip(jnp.minimum(c_start[:, None] + CMP_BLOCK, s_start[None, :] + SEL_BLOCK)
                       - jnp.maximum(c_start[:, None], s_start[None, :]), 0, None)
    cmp_to_sel = (overlap / CMP_STRIDE).astype(jnp.float32)
    imp = jnp.einsum('bgrsc,cn->bgsn', p_c, cmp_to_sel)
    cur = t // SEL_BLOCK
    j = jnp.arange(n_sel)
    forced = (j[None, :] == 0) | (j[None, :] == cur[:, None]) | (j[None, :] == cur[:, None] - 1)
    future = j[None, :] > cur[:, None]
    imp = jnp.where(future, -1.0, jnp.where(forced, FORCED_SCORE, imp))
    _, sel_idx = lax.top_k(imp, k_top)

    k_s = apply_rope(k_s.reshape(Bsz, S, G, Dh), cos, sin)
    ks_blocks = k_s.transpose(0, 2, 1, 3).reshape(Bsz, G, n_sel, SEL_BLOCK, Dh)
    vs_blocks = v_s.reshape(Bsz, S, G, Dh).transpose(0, 2, 1, 3).reshape(Bsz, G, n_sel, SEL_BLOCK, Dh)
    q_blk = q.reshape(Bsz, G, R, nb, Q_BLOCK, Dh)
    gather = jax.vmap(jax.vmap(lambda blocks, idx: blocks[idx]))
    n_keys = k_top * SEL_BLOCK

    def sel_block(args):
        qb, ib, tb = args
        kg = gather(ks_blocks, ib).reshape(Bsz, G, Q_BLOCK, n_keys, Dh)
        vg = gather(vs_blocks, ib).reshape(Bsz, G, Q_BLOCK, n_keys, Dh)
        s = jnp.einsum('bgrqd,bgqnd->bgrqn', qb, kg) * scale
        kpos = (ib[..., None] * SEL_BLOCK + jnp.arange(SEL_BLOCK)).reshape(Bsz, G, Q_BLOCK, n_keys)
        mask = (kpos <= tb[:, None])[:, :, None]
        p = masked_softmax(s, mask)
        return jnp.einsum('bgrqn,bgqnd->bgrqd', p.astype(vg.dtype), vg)

    o_s = lax.map(sel_block, (jnp.moveaxis(q_blk, 3, 0),
                              jnp.moveaxis(sel_idx.reshape(Bsz, G, nb, Q_BLOCK, k_top), 2, 0),
                              t.reshape(nb, Q_BLOCK)))
    o_s = jnp.moveaxis(o_s, 0, 3).reshape(Bsz, G, R, S, Dh)

    k_w = apply_rope(k_w.reshape(Bsz, S, G, Dh), cos, sin).transpose(0, 2, 1, 3)
    v_w = v_w.reshape(Bsz, S, G, Dh).transpose(0, 2, 1, 3)
    span = WINDOW + Q_BLOCK
    win_idx = jnp.arange(nb)[:, None] * Q_BLOCK + jnp.arange(span)[None, :]
    pad = ((0, 0), (0, 0), (WINDOW, 0), (0, 0))
    kw = jnp.pad(k_w, pad)[:, :, win_idx]
    vw = jnp.pad(v_w, pad)[:, :, win_idx]
    kpos = win_idx - WINDOW
    diff = t.reshape(nb, Q_BLOCK)[:, :, None] - kpos[:, None, :]
    wmask = (diff >= 0) & (diff < WINDOW) & (kpos[:, None, :] >= 0)
    s_w = jnp.einsum('bgrnqd,bgnkd->bgrnqk', q_blk, kw) * scale
    p_w = masked_softmax(s_w, wmask)
    o_w = jnp.einsum('bgrnqk,bgnkd->bgrnqd', p_w.astype(vw.dtype), vw).reshape(Bsz, G, R, S, Dh)

    def to_heads(o):
        return o.transpose(0, 3, 1, 2, 4).reshape(Bsz, S, NSA_HEADS, Dh)

    g = jax.nn.sigmoid(gate_logits.reshape(Bsz, S, NSA_HEADS, 3))
    o = g[..., 0:1] * to_heads(o_c) + g[..., 1:2] * to_heads(o_s) + g[..., 2:3] * to_heads(o_w)
    return o.reshape(Bsz, S, Q_W)


def ssd_chunked(xdt, a, bm, cm):
    Bsz, S, H, P = xdt.shape
    G, N = bm.shape[2], bm.shape[3]
    R = H // G
    L = SSD_CHUNK
    C = S // L
    x = xdt.reshape(Bsz, C, L, G, R, P)
    a = a.astype(jnp.float32).reshape(Bsz, C, L, G, R).transpose(0, 3, 4, 1, 2)
    bm = bm.reshape(Bsz, C, L, G, N)
    cm = cm.reshape(Bsz, C, L, G, N)
    a_cs = jnp.cumsum(a, axis=-1)
    causal = jnp.tril(jnp.ones((L, L), dtype=bool))
    seg = a_cs[..., :, None] - a_cs[..., None, :]
    decay = jnp.exp(jnp.where(causal, seg, -jnp.inf))
    cb = jnp.einsum('bclgn,bcsgn->bgcls', cm, bm)
    y_diag = jnp.einsum('bgcls,bgrcls,bcsgrp->bclgrp', cb, decay, x)
    decay_states = jnp.exp(a_cs[..., -1:] - a_cs)
    states = jnp.einsum('bclgn,bgrcl,bclgrp->bcgrpn', bm, decay_states, x).astype(jnp.float32)
    chunk_decay = jnp.exp(a_cs[..., -1])

    def step(h, inp):
        st, dec = inp
        return h * dec[..., None, None] + st, h

    h0 = jnp.zeros((Bsz, G, R, P, N), jnp.float32)
    _, prev = lax.scan(step, h0, (jnp.moveaxis(states, 1, 0), jnp.moveaxis(chunk_decay, 3, 0)))
    prev = jnp.moveaxis(prev, 0, 1)
    y_off = jnp.einsum('bclgn,bcgrpn,bgrcl->bclgrp', cm, prev, jnp.exp(a_cs))
    return (y_diag + y_off).reshape(Bsz, S, H, P).astype(xdt.dtype)


def mamba2_mixer(z, xbc, dt_raw, conv_w, conv_b, dt_bias, a_log, ssm_d, ssm_norm_w):
    Bsz, S = z.shape[:2]
    xbc = lax.conv_general_dilated(xbc, conv_w[:, None, :], window_strides=(1,),
                                   padding=[(CONV_WIDTH - 1, 0)],
                                   dimension_numbers=('NWC', 'WIO', 'NWC'),
                                   feature_group_count=XBC_W)
    xbc = jax.nn.silu(xbc + conv_b)
    xs, bm, cm = jnp.split(xbc, [SSM_D_INNER, SSM_D_INNER + SSM_GROUPS * SSM_STATE], axis=-1)
    xs = xs.reshape(Bsz, S, SSM_HEADS, SSM_HEAD_DIM)
    bm = bm.reshape(Bsz, S, SSM_GROUPS, SSM_STATE)
    cm = cm.reshape(Bsz, S, SSM_GROUPS, SSM_STATE)
    dt = jax.nn.softplus(dt_raw.astype(jnp.float32) + dt_bias.astype(jnp.float32))
    a = -jnp.exp(a_log.astype(jnp.float32))
    y = ssd_chunked(xs * dt[..., None].astype(xs.dtype), dt * a, bm, cm)
    y = y + ssm_d[:, None] * xs
    y = y.reshape(Bsz, S, SSM_D_INNER) * jax.nn.silu(z)
    yg = y.reshape(Bsz, S, SSM_GROUPS, SSM_D_INNER // SSM_GROUPS).astype(jnp.float32)
    yg = yg * lax.rsqrt(jnp.mean(yg * yg, axis=-1, keepdims=True) + EPS)
    return (yg.reshape(Bsz, S, SSM_D_INNER) * ssm_norm_w.astype(jnp.float32)).astype(z.dtype)


def setup_inputs(seed: int = 0) -> dict:
    key = jax.random.key(seed)
    ks = jax.random.split(key, 24)
    f32 = jnp.float32
    L = DEPTH

    def nrm(k, shape, fan_in):
        return jax.random.normal(k, shape, f32) * fan_in ** -0.5

    def gain(k, shape):
        return 1.0 + 0.02 * jax.random.normal(k, shape, f32)

    x = jax.random.normal(ks[0], (BATCH, SEQ, D_MODEL), f32)
    offset = jax.random.randint(ks[1], (BATCH, 1), 0, 1024, jnp.int32)
    positions = jnp.arange(SEQ, dtype=jnp.int32)[None, :] + offset
    dt0 = jnp.exp(jax.random.uniform(ks[12], (L, SSM_HEADS), f32, math.log(1e-3), math.log(1e-1)))
    return {
        "x": x,
        "positions": positions,
        "norm_mix_w": gain(ks[2], (L, D_MODEL)),
        "w_in": nrm(ks[3], (L, D_MODEL, IN_W), D_MODEL),
        "cmp_pos_k": 0.02 * jax.random.normal(ks[4], (L, CMP_BLOCK, NSA_HEAD_DIM), f32),
        "cmp_pos_v": 0.02 * jax.random.normal(ks[5], (L, CMP_BLOCK, NSA_HEAD_DIM), f32),
        "cmp_k_w1": nrm(ks[6], (L, CMP_BLOCK * NSA_HEAD_DIM, NSA_HEAD_DIM), CMP_BLOCK * NSA_HEAD_DIM),
        "cmp_k_w2": nrm(ks[7], (L, NSA_HEAD_DIM, NSA_HEAD_DIM), NSA_HEAD_DIM),
        "cmp_v_w1": nrm(ks[8], (L, CMP_BLOCK * NSA_HEAD_DIM, NSA_HEAD_DIM), CMP_BLOCK * NSA_HEAD_DIM),
        "cmp_v_w2": nrm(ks[9], (L, NSA_HEAD_DIM, NSA_HEAD_DIM), NSA_HEAD_DIM),
        "conv_w": nrm(ks[10], (L, CONV_WIDTH, XBC_W), CONV_WIDTH),
        "conv_b": 0.01 * jax.random.normal(ks[11], (L, XBC_W), f32),
        "dt_bias": dt0 + jnp.log(-jnp.expm1(-dt0)),
        "a_log": jnp.log(jax.random.uniform(ks[13], (L, SSM_HEADS), f32, 1.0, 16.0)),
        "ssm_d": 1.0 + 0.1 * jax.random.normal(ks[14], (L, SSM_HEADS), f32),
        "ssm_norm_w": gain(ks[15], (L, SSM_D_INNER)),
        "w_proj_nsa": nrm(ks[16], (L, Q_W, D_MODEL), Q_W),
        "w_proj_ssm": nrm(ks[17], (L, SSM_D_INNER, D_MODEL), SSM_D_INNER),
        "w_out": nrm(ks[18], (L, D_MODEL, D_MODEL), D_MODEL),
        "norm_mlp_w": gain(ks[19], (L, D_MODEL)),
        "w_up": nrm(ks[20], (L, D_MODEL, D_FF), D_MODEL),
        "w_down": nrm(ks[21], (L, D_FF, D_MODEL), D_FF),
        "norm_final_w": gain(ks[22], (D_MODEL,)),
    }


def reference(x, positions, norm_mix_w, w_in, cmp_pos_k, cmp_pos_v, cmp_k_w1, cmp_k_w2,
              cmp_v_w1, cmp_v_w2, conv_w, conv_b, dt_bias, a_log, ssm_d, ssm_norm_w,
              w_proj_nsa, w_proj_ssm, w_out, norm_mlp_w, w_up, w_down, norm_final_w):
    cos, sin = rope_tables(positions, NSA_HEAD_DIM)
    split_points = _split_points()
    h = x
    for l in range(DEPTH):
        u = rms_norm(h, norm_mix_w[l])
        proj = u @ w_in[l]
        (q, k_c, v_c, k_s, v_s, k_w, v_w, nsa_gate, z, xbc, dt_raw,
         branch_gate) = jnp.split(proj, split_points, axis=-1)
        o_nsa = nsa_mixer(q, k_c, v_c, k_s, v_s, k_w, v_w, nsa_gate, cos, sin,
                          cmp_pos_k[l], cmp_pos_v[l], cmp_k_w1[l], cmp_k_w2[l],
                          cmp_v_w1[l], cmp_v_w2[l])
        o_ssm = mamba2_mixer(z, xbc, dt_raw, conv_w[l], conv_b[l], dt_bias[l], a_log[l],
                             ssm_d[l], ssm_norm_w[l])
        gate_a, gate_b = jnp.split(branch_gate, 2, axis=-1)
        merged = (jax.nn.sigmoid(gate_a) * (o_nsa @ w_proj_nsa[l])
                  + jax.nn.sigmoid(gate_b) * (o_ssm @ w_proj_ssm[l]))
        h = h + merged @ w_out[l]
        hn = rms_norm(h, norm_mlp_w[l])
        h = h + jnp.square(jax.nn.relu(hn @ w_up[l])) @ w_down[l]
    return rms_norm(h, norm_final_w)
```

```python
import functools
import math

import jax
import jax.numpy as jnp
import numpy as np
from jax import lax
from jax.experimental import pallas as pl
from jax.experimental.pallas import tpu as pltpu

F32 = jnp.float32
BF16 = jnp.bfloat16

D_MODEL = 2048
NSA_HEADS = 16
NSA_GROUPS = 4
NSA_REP = NSA_HEADS // NSA_GROUPS
DH = 128
CMP_BLOCK = 32
CMP_STRIDE = 16
SEL_BLOCK = 64
SEL_TOPK = 16
WINDOW = 512
ROPE_THETA = 10000.0
FORCED_SCORE = 1.0e4
SSM_D_INNER = 4096
SSM_HEAD_DIM = 64
SSM_HEADS = 64
SSM_GROUPS = 8
SSM_STATE = 128
CONV_WIDTH = 4
SSD_CHUNK = 128
D_FF = 4 * D_MODEL
EPS = 1e-6
NEG_INF = -1e30
TINY = 1e-30
MASK_NEG = -1e9
TAKEN = -2.0
LANES = 128
VMEM_LIMIT = 56 * 1024 * 1024

COL_Z = 0
COL_XS = 4096
COL_B = 8192
COL_C = 9216
COL_Q = 10240
COL_KS = 12288
COL_KW = 12800
COL_KC = 13312
COL_VC = 13824
COL_VS = 14336
COL_VW = 14848
COL_GA = 15360
COL_GB = 17408
MAIN_W = 19456
SMALL_W = 640


def _dot(a, b):
    return jnp.dot(a, b, preferred_element_type=F32)


def _dot_nt(a, b):
    return lax.dot_general(a, b, (((1,), (1,)), ((), ())), preferred_element_type=F32)


def _dot_tn(a, b):
    return lax.dot_general(a, b, (((0,), (0,)), ((), ())), preferred_element_type=F32)


def _params(sem):
    return pltpu.CompilerParams(dimension_semantics=sem, vmem_limit_bytes=VMEM_LIMIT)


def _rope(x, cos, sin_signed):
    return x * cos + pltpu.roll(x, DH // 2, 1) * sin_signed


def _split2(x):
    hi = x.astype(BF16)
    lo = (x - hi.astype(F32)).astype(BF16)
    return hi, lo


def _split3(x):
    hi = x.astype(BF16)
    r = x - hi.astype(F32)
    mid = r.astype(BF16)
    lo = (r - mid.astype(F32)).astype(BF16)
    return hi, mid, lo


def _rope_table_kernel(pos_ref, inv_ref, sign_ref, cos_ref, sin_ref):
    ang = pos_ref[...].astype(F32) * inv_ref[...]
    cos_ref[...] = jnp.cos(ang)
    sin_ref[...] = jnp.sin(ang) * sign_ref[...]


def _rope_tables(positions, seq):
    half = DH // 2
    inv = ROPE_THETA ** (-jnp.arange(0, DH, 2, dtype=F32) / DH)
    inv_full = jnp.concatenate([inv, inv]).reshape(1, DH)
    sign = jnp.concatenate([-jnp.ones((half,), F32), jnp.ones((half,), F32)]).reshape(1, DH)
    tm = min(seq, 1024)
    return pl.pallas_call(
        _rope_table_kernel,
        grid=(seq // tm,),
        in_specs=[pl.BlockSpec((tm, 1), lambda i: (i, 0)),
                  pl.BlockSpec((1, DH), lambda i: (0, 0)),
                  pl.BlockSpec((1, DH), lambda i: (0, 0))],
        out_specs=[pl.BlockSpec((tm, DH), lambda i: (i, 0))] * 2,
        out_shape=[jax.ShapeDtypeStruct((seq, DH), F32)] * 2,
        compiler_params=_params(("arbitrary",)),
        name="rope_tables",
    )(positions.reshape(seq, 1), inv_full, sign)


INPROJ_TN = 512
ROPE_TILE_LO = COL_Q // INPROJ_TN
ROPE_TILE_HI = COL_KC // INPROJ_TN
Q_TILE_HI = COL_KS // INPROJ_TN


def _inproj_kernel(x_ref, nw_ref, w_ref, ws_ref, cos_ref, sin_ref, o_ref, small_ref, u_scr):
    j = pl.program_id(1)

    @pl.when(j == 0)
    def _():
        xf = x_ref[...]
        ms = jnp.mean(xf * xf, axis=-1, keepdims=True)
        u = (xf * lax.rsqrt(ms + EPS) * nw_ref[...]).astype(BF16)
        u_scr[...] = u
        small_ref[...] = _dot(u, ws_ref[...])

    acc = _dot(u_scr[...], w_ref[...])
    is_rope = jnp.logical_and(j >= ROPE_TILE_LO, j < ROPE_TILE_HI)

    @pl.when(jnp.logical_not(is_rope))
    def _():
        o_ref[...] = acc.astype(o_ref.dtype)

    @pl.when(is_rope)
    def _():
        cos = cos_ref[...]
        sin = sin_ref[...]
        scale = jnp.where(j < Q_TILE_HI, DH ** -0.5, 1.0).astype(F32)
        for h in range(INPROJ_TN // DH):
            xh = acc[:, h * DH:(h + 1) * DH]
            o_ref[:, h * DH:(h + 1) * DH] = (_rope(xh, cos, sin) * scale).astype(o_ref.dtype)


def _inproj(x2, norm_w, w_main, w_small, cosf, sinf):
    seq = x2.shape[0]
    tm = min(seq, 1024)
    tn = INPROJ_TN
    return pl.pallas_call(
        _inproj_kernel,
        grid=(seq // tm, MAIN_W // tn),
        in_specs=[pl.BlockSpec((tm, D_MODEL), lambda i, j: (i, 0)),
                  pl.BlockSpec((1, D_MODEL), lambda i, j: (0, 0)),
                  pl.BlockSpec((D_MODEL, tn), lambda i, j: (0, j)),
                  pl.BlockSpec((D_MODEL, SMALL_W), lambda i, j: (0, 0)),
                  pl.BlockSpec((tm, DH), lambda i, j: (i, 0)),
                  pl.BlockSpec((tm, DH), lambda i, j: (i, 0))],
        out_specs=[pl.BlockSpec((tm, tn), lambda i, j: (i, j)),
                   pl.BlockSpec((tm, SMALL_W), lambda i, j: (i, 0))],
        out_shape=[jax.ShapeDtypeStruct((seq, MAIN_W), BF16),
                   jax.ShapeDtypeStruct((seq, SMALL_W), F32)],
        scratch_shapes=[pltpu.VMEM((tm, D_MODEL), BF16)],
        compiler_params=_params(("arbitrary", "arbitrary")),
        name="inproj",
    )(x2, norm_w.reshape(1, D_MODEL), w_main, w_small, cosf, sinf)


def _compress_kernel(xk_ref, xv_ref, pk_ref, pv_ref, w1k_ref, w2k_ref, w1v_ref, w2v_ref,
                     cos_ref, sin_ref, kc_ref, vc_ref):
    half = CMP_STRIDE * DH
    ncp = xk_ref.shape[1]

    def comp(x_ref, p_ref, w1_ref, w2_ref):
        x = x_ref[0]
        wt = w1_ref[0:half, :]
        wb = w1_ref[half:2 * half, :]
        a = _dot(x, wt)
        b = _dot(x, wb)
        pb = _dot(p_ref[...], wt)[0:1] + _dot(p_ref[...], wb)[1:2]
        h = a + pltpu.roll(b, ncp - 1, 0) + pb
        h = h * jax.nn.sigmoid(h)
        return _dot(h.astype(BF16), w2_ref[...])

    kc = comp(xk_ref, pk_ref, w1k_ref, w2k_ref)
    kc_ref[0] = _rope(kc, cos_ref[...], sin_ref[...]).astype(BF16)
    vc_ref[0] = comp(xv_ref, pv_ref, w1v_ref, w2v_ref).astype(BF16)


def _compress(xk, xv, pk, pv, w1k, w2k, w1v, w2v, cos_c, sin_c):
    g, ncp, width = xk.shape
    full = lambda shape: pl.BlockSpec(shape, lambda i: (0,) * len(shape))
    per_g = pl.BlockSpec((1, ncp, width), lambda i: (i, 0, 0))
    out_g = pl.BlockSpec((1, ncp, DH), lambda i: (i, 0, 0))
    return pl.pallas_call(
        _compress_kernel,
        grid=(g,),
        in_specs=[per_g, per_g, full(pk.shape), full(pv.shape), full(w1k.shape), full(w2k.shape),
                  full(w1v.shape), full(w2v.shape), full(cos_c.shape), full(sin_c.shape)],
        out_specs=[out_g, out_g],
        out_shape=[jax.ShapeDtypeStruct((g, ncp, DH), BF16)] * 2,
        compiler_params=_params(("arbitrary",)),
        name="compress",
    )(xk, xv, pk, pv, w1k, w2k, w1v, w2v, cos_c, sin_c)


CMP_TQ = 128


def _cmp_attn_kernel(q_ref, kc_ref, vc_ref, mt_ref, oc_ref, mask_ref):
    i = pl.program_id(0)
    qbase = i * CMP_TQ
    rows = NSA_REP * CMP_TQ
    ncp = kc_ref.shape[1]
    nsel = mt_ref.shape[0]
    row = lax.broadcasted_iota(jnp.int32, (rows, ncp), 0)
    col = lax.broadcasted_iota(jnp.int32, (rows, ncp), 1)
    t = qbase + (row & (CMP_TQ - 1))
    valid = (col * CMP_STRIDE + (CMP_BLOCK - 1)) <= t
    blk = lax.broadcasted_iota(jnp.int32, (nsel, CMP_TQ), 0)
    cur = (qbase + lax.broadcasted_iota(jnp.int32, (nsel, CMP_TQ), 1)) >> 6
    future = blk > cur
    forced = (blk == 0) | (blk == cur) | (blk == cur - 1)
    blk_f = blk.astype(F32)
    mt = mt_ref[...]
    for g in range(NSA_GROUPS):
        qg = jnp.concatenate(
            [q_ref[:, (g * NSA_REP + r) * DH:(g * NSA_REP + r + 1) * DH] for r in range(NSA_REP)], axis=0)
        s = jnp.where(valid, _dot_nt(qg, kc_ref[g]), NEG_INF)
        m = jnp.max(s, axis=-1, keepdims=True)
        p = jnp.where(valid, jnp.exp(s - m), 0.0)
        p = p / jnp.maximum(jnp.sum(p, axis=-1, keepdims=True), TINY)
        o = _dot(p.astype(BF16), vc_ref[g])
        for r in range(NSA_REP):
            h = g * NSA_REP + r
            oc_ref[:, h * DH:(h + 1) * DH] = o[r * CMP_TQ:(r + 1) * CMP_TQ].astype(oc_ref.dtype)
        ps = p[0:CMP_TQ]
        for r in range(1, NSA_REP):
            ps = ps + p[r * CMP_TQ:(r + 1) * CMP_TQ]
        ps_hi, ps_lo = _split2(ps)
        imp_t = _dot_nt(mt, ps_hi) + _dot_nt(mt, ps_lo)
        v = jnp.where(future, -1.0, jnp.where(forced, FORCED_SCORE, imp_t))
        for _ in range(SEL_TOPK):
            mx = jnp.max(v, axis=0, keepdims=True)
            first = jnp.min(jnp.where(v == mx, blk_f, float(nsel)), axis=0, keepdims=True)
            v = jnp.where(blk_f == first, TAKEN, v)
        bias_t = jnp.where(v == TAKEN, 0.0, MASK_NEG)
        mask_ref[g] = bias_t.T.astype(mask_ref.dtype)


def _cmp_attn(main, kc, vc, mt, seq):
    nsel = mt.shape[0]
    ncp = kc.shape[1]
    return pl.pallas_call(
        _cmp_attn_kernel,
        grid=(seq // CMP_TQ,),
        in_specs=[pl.BlockSpec((CMP_TQ, NSA_HEADS * DH), lambda i: (i, COL_Q // (NSA_HEADS * DH))),
                  pl.BlockSpec((NSA_GROUPS, ncp, DH), lambda i: (0, 0, 0)),
                  pl.BlockSpec((NSA_GROUPS, ncp, DH), lambda i: (0, 0, 0)),
                  pl.BlockSpec((nsel, ncp), lambda i: (0, 0))],
        out_specs=[pl.BlockSpec((CMP_TQ, NSA_HEADS * DH), lambda i: (i, 0)),
                   pl.BlockSpec((NSA_GROUPS, CMP_TQ, nsel), lambda i: (0, i, 0))],
        out_shape=[jax.ShapeDtypeStruct((seq, NSA_HEADS * DH), BF16),
                   jax.ShapeDtypeStruct((NSA_GROUPS, seq, nsel), BF16)],
        compiler_params=_params(("arbitrary",)),
        name="cmp_attn_topk",
    )(main, kc, vc, mt)


ATT_TQ = 256
ATT_TK = 256
WIN_SPAN = WINDOW + ATT_TQ


def _nsa_attn_kernel(q_ref, mask_ref, ks_ref, vs_ref, kw_ref, vw_ref, oc_ref, gate_ref, o_ref,
                     acc_scr, m_scr):
    i = pl.program_id(1)
    qbase = i * ATT_TQ
    rows = NSA_REP * ATT_TQ
    nsel = mask_ref.shape[2]
    assert nsel == LANES, "the block-selection bias rides in one extra 128-wide contraction slab"
    bias = mask_ref[0]
    heads = [q_ref[:, r * DH:(r + 1) * DH] for r in range(NSA_REP)]
    q_plain = jnp.concatenate(heads, axis=0)
    q_aug = jnp.concatenate([jnp.concatenate([h, bias], axis=1) for h in heads], axis=0)

    m_scr[...] = jnp.full(m_scr.shape, NEG_INF, F32)
    acc_scr[...] = jnp.zeros(acc_scr.shape, F32)
    ones = jnp.ones((ATT_TK, LANES), BF16)
    krow = lax.broadcasted_iota(jnp.int32, (ATT_TK, LANES), 0)
    kblk = lax.broadcasted_iota(jnp.int32, (ATT_TK, LANES), 1)

    def step(j, diagonal):
        k0 = pl.multiple_of(j * ATT_TK, ATT_TK)
        onehot = jnp.where(((krow + k0) >> 6) == kblk, 1.0, 0.0).astype(BF16)
        k_aug = jnp.concatenate([ks_ref[pl.ds(k0, ATT_TK), :], onehot], axis=1)
        s = _dot_nt(q_aug, k_aug)
        if diagonal:
            r_ = lax.broadcasted_iota(jnp.int32, (rows, ATT_TK), 0) & (ATT_TQ - 1)
            c_ = lax.broadcasted_iota(jnp.int32, (rows, ATT_TK), 1)
            s = jnp.where(c_ <= r_, s, MASK_NEG)
        m_prev = m_scr[...]
        m_new = jnp.maximum(m_prev, jnp.max(s, axis=-1, keepdims=True))
        alpha = jnp.exp(m_prev - m_new)
        p = jnp.exp(s - jnp.concatenate([m_new] * (ATT_TK // LANES), axis=1))
        v_aug = jnp.concatenate([vs_ref[pl.ds(k0, ATT_TK), :], ones], axis=1)
        acc_scr[...] = acc_scr[...] * jnp.concatenate([alpha, alpha], axis=1) + _dot(p.astype(BF16), v_aug)
        m_scr[...] = m_new

    def body(j, carry):
        step(j, False)
        return carry

    lax.fori_loop(0, i, body, 0)
    step(i, True)
    acc = acc_scr[...]
    o_sel = acc[:, :DH] / jnp.maximum(acc[:, DH:], TINY)

    start = pl.multiple_of(jnp.maximum(qbase - WINDOW, 0), ATT_TQ)
    kw = kw_ref[pl.ds(start, WIN_SPAN), :]
    vw = vw_ref[pl.ds(start, WIN_SPAN), :]
    r_ = lax.broadcasted_iota(jnp.int32, (rows, WIN_SPAN), 0) & (ATT_TQ - 1)
    c_ = lax.broadcasted_iota(jnp.int32, (rows, WIN_SPAN), 1)
    diff = (qbase + r_) - (start + c_)
    valid = (diff >= 0) & (diff < WINDOW)
    s = jnp.where(valid, _dot_nt(q_plain, kw), NEG_INF)
    m = jnp.max(s, axis=-1, keepdims=True)
    p = jnp.where(valid, jnp.exp(s - m), 0.0)
    p = p / jnp.maximum(jnp.sum(p, axis=-1, keepdims=True), TINY)
    o_win = _dot(p.astype(BF16), vw)

    gates = jax.nn.sigmoid(gate_ref[...])
    for r in range(NSA_REP):
        sl = slice(r * ATT_TQ, (r + 1) * ATT_TQ)
        o = (gates[:, 3 * r:3 * r + 1] * oc_ref[:, r * DH:(r + 1) * DH].astype(F32)
             + gates[:, 3 * r + 1:3 * r + 2] * o_sel[sl]
             + gates[:, 3 * r + 2:3 * r + 3] * o_win[sl])
        o_ref[:, r * DH:(r + 1) * DH] = o.astype(o_ref.dtype)


def _nsa_attn(main, small, mask, o_cmp, seq):
    gw = NSA_REP * DH
    nsel = mask.shape[2]
    kv_spec = lambda col: pl.BlockSpec((seq, DH), lambda g, i: (0, col // DH + g))
    return pl.pallas_call(
        _nsa_attn_kernel,
        grid=(NSA_GROUPS, seq // ATT_TQ),
        in_specs=[pl.BlockSpec((ATT_TQ, gw), lambda g, i: (i, COL_Q // gw + g)),
                  pl.BlockSpec((1, ATT_TQ, nsel), lambda g, i: (g, i, 0)),
                  kv_spec(COL_KS), kv_spec(COL_VS), kv_spec(COL_KW), kv_spec(COL_VW),
                  pl.BlockSpec((ATT_TQ, gw), lambda g, i: (i, g)),
                  pl.BlockSpec((ATT_TQ, LANES), lambda g, i: (i, 1 + g))],
        out_specs=pl.BlockSpec((ATT_TQ, gw), lambda g, i: (i, g)),
        out_shape=jax.ShapeDtypeStruct((seq, NSA_HEADS * DH), BF16),
        scratch_shapes=[pltpu.VMEM((NSA_REP * ATT_TQ, 2 * DH), F32),
                        pltpu.VMEM((NSA_REP * ATT_TQ, LANES), F32)],
        compiler_params=_params(("arbitrary", "arbitrary")),
        name="nsa_sel_win_attn",
    )(main, mask, main, main, main, main, o_cmp, small)


HALO = 8
GROUP_W = SSM_D_INNER // SSM_GROUPS
HEADS_PER_GROUP = SSM_HEADS // SSM_GROUPS


def _mamba_kernel(z_ref, xs_ref, b_ref, c_ref, xsh_ref, bh_ref, ch_ref, dt_ref,
                  cwx_ref, cwb_ref, cwc_ref, cbx_ref, cbb_ref, cbc_ref,
                  dtb_ref, alog_ref, dfull_ref, nw_ref, expand_ref, tril_ref,
                  o_ref, state_scr):
    c = pl.program_id(0)
    L = SSD_CHUNK

    @pl.when(c == 0)
    def _():
        state_scr[...] = jnp.zeros(state_scr.shape, F32)

    def conv_silu(x_ref, h_ref, w_ref, bias_ref):
        x = x_ref[...].astype(F32)
        halo = jnp.where(c == 0, 0.0, h_ref[...].astype(F32))
        ext = jnp.concatenate([halo, x], axis=0)
        acc = bias_ref[...] + w_ref[CONV_WIDTH - 1:CONV_WIDTH, :] * x
        for k in range(CONV_WIDTH - 1):
            off = HALO - (CONV_WIDTH - 1) + k
            acc = acc + w_ref[k:k + 1, :] * ext[off:off + L, :]
        return acc * jax.nn.sigmoid(acc)

    xs = conv_silu(xs_ref, xsh_ref, cwx_ref, cbx_ref)
    bm = conv_silu(b_ref, bh_ref, cwb_ref, cbb_ref).astype(BF16)
    cm = conv_silu(c_ref, ch_ref, cwc_ref, cbc_ref).astype(BF16)

    xdt_raw = dt_ref[...] + dtb_ref[...]
    dt = jnp.maximum(xdt_raw, 0.0) + jnp.log1p(jnp.exp(-jnp.abs(xdt_raw)))
    adt = dt * (-jnp.exp(alog_ref[...]))
    tril = tril_ref[...]
    a_hi, a_mid, a_lo = _split3(adt)
    acs = _dot(tril, a_hi) + _dot(tril, a_mid) + _dot(tril, a_lo)
    acs_t = acs.T
    a_last = acs[L - 1:L, :]
    stacked = jnp.concatenate([dt, jnp.exp(acs), jnp.exp(a_last - acs)], axis=0)
    s_hi, s_lo = _split2(stacked)
    expand = expand_ref[...]
    wide = _dot(s_hi, expand) + _dot(s_lo, expand)
    dt_w = wide[0:L]
    in_decay_w = wide[L:2 * L]
    out_decay_w = wide[2 * L:3 * L]

    xdt = xs * dt_w
    xst = xdt * out_decay_w
    causal = lax.broadcasted_iota(jnp.int32, (L, L), 0) >= lax.broadcasted_iota(jnp.int32, (L, L), 1)
    lane = lax.broadcasted_iota(jnp.int32, (L, LANES), 1)
    low_half = lane < SSM_HEAD_DIM
    chunk_decay = jnp.exp(acs_t[:, L - 1:L])

    for g in range(SSM_GROUPS):
        gs = slice(g * GROUP_W, (g + 1) * GROUP_W)
        bg = bm[:, g * SSM_STATE:(g + 1) * SSM_STATE]
        cg = cm[:, g * SSM_STATE:(g + 1) * SSM_STATE]
        cb = _dot_nt(cg, bg)
        prev = state_scr[gs, :]
        y_off = _dot_nt(cg, prev.astype(BF16)) * in_decay_w[:, gs]
        pieces = []
        for hp in range(HEADS_PER_GROUP // 2):
            lhs = []
            for e in range(2):
                h = g * HEADS_PER_GROUP + 2 * hp + e
                seg = acs[:, h:h + 1] - acs_t[h:h + 1, :]
                decay = jnp.where(causal, jnp.exp(jnp.where(causal, seg, 0.0)), 0.0)
                lhs.append((cb * decay).astype(BF16))
            slab = xdt[:, g * GROUP_W + hp * LANES:g * GROUP_W + (hp + 1) * LANES]
            rhs = jnp.concatenate([jnp.where(low_half, slab, 0.0), jnp.where(low_half, 0.0, slab)], axis=0)
            pieces.append(_dot(jnp.concatenate(lhs, axis=1), rhs.astype(BF16)))
        y_g = jnp.concatenate(pieces, axis=1) + y_off
        y_g = y_g + dfull_ref[:, gs] * xs[:, gs]
        zg = z_ref[:, gs].astype(F32)
        y_g = y_g * (zg * jax.nn.sigmoid(zg))
        ms = jnp.mean(y_g * y_g, axis=-1, keepdims=True)
        o_ref[:, gs] = (y_g * lax.rsqrt(ms + EPS) * nw_ref[:, gs]).astype(o_ref.dtype)
        new = _dot(xst[:, gs].T.astype(BF16), bg)
        cd = jnp.broadcast_to(chunk_decay[g * HEADS_PER_GROUP:(g + 1) * HEADS_PER_GROUP], (HEADS_PER_GROUP, LANES))
        cd = jnp.broadcast_to(cd[:, None, :], (HEADS_PER_GROUP, SSM_HEAD_DIM, LANES)).reshape(GROUP_W, LANES)
        state_scr[gs, :] = prev * cd + new


def _mamba(main, small, cwx, cwb, cwc, cbx, cbb, cbc, dtb, alog, dfull, nw, expand, tril, seq):
    L = SSD_CHUNK
    bw = SSM_GROUPS * SSM_STATE
    halo_idx = lambda col_blk: (lambda c: (jnp.maximum(c * (L // HALO) - 1, 0), col_blk))
    full = lambda a: pl.BlockSpec(a.shape, lambda c: (0,) * a.ndim)
    consts = [cwx, cwb, cwc, cbx, cbb, cbc, dtb, alog, dfull, nw, expand, tril]
    return pl.pallas_call(
        _mamba_kernel,
        grid=(seq // L,),
        in_specs=[pl.BlockSpec((L, SSM_D_INNER), lambda c: (c, COL_Z // SSM_D_INNER)),
                  pl.BlockSpec((L, SSM_D_INNER), lambda c: (c, COL_XS // SSM_D_INNER)),
                  pl.BlockSpec((L, bw), lambda c: (c, COL_B // bw)),
                  pl.BlockSpec((L, bw), lambda c: (c, COL_C // bw)),
                  pl.BlockSpec((HALO, SSM_D_INNER), halo_idx(COL_XS // SSM_D_INNER)),
                  pl.BlockSpec((HALO, bw), halo_idx(COL_B // bw)),
                  pl.BlockSpec((HALO, bw), halo_idx(COL_C // bw)),
                  pl.BlockSpec((L, LANES), lambda c: (c, 0))] + [full(a) for a in consts],
        out_specs=pl.BlockSpec((L, SSM_D_INNER), lambda c: (c, 0)),
        out_shape=jax.ShapeDtypeStruct((seq, SSM_D_INNER), BF16),
        scratch_shapes=[pltpu.VMEM((SSM_D_INNER, SSM_STATE), F32)],
        compiler_params=_params(("arbitrary",)),
        name="mamba2_ssd",
    )(main, main, main, main, main, main, main, small, *consts)


def _merge_kernel(a_ref, b_ref, pa_ref, pb_ref, ga_ref, gb_ref, o_ref):
    ya = _dot(a_ref[...], pa_ref[...])
    yb = _dot(b_ref[...], pb_ref[...])
    o = jax.nn.sigmoid(ga_ref[...].astype(F32)) * ya + jax.nn.sigmoid(gb_ref[...].astype(F32)) * yb
    o_ref[...] = o.astype(o_ref.dtype)


def _merge(o_nsa, o_ssm, pa, pb, main, seq):
    tm = min(seq, 512)
    tn = 512
    return pl.pallas_call(
        _merge_kernel,
        grid=(seq // tm, D_MODEL // tn),
        in_specs=[pl.BlockSpec((tm, NSA_HEADS * DH), lambda i, j: (i, 0)),
                  pl.BlockSpec((tm, SSM_D_INNER), lambda i, j: (i, 0)),
                  pl.BlockSpec((NSA_HEADS * DH, tn), lambda i, j: (0, j)),
                  pl.BlockSpec((SSM_D_INNER, tn), lambda i, j: (0, j)),
                  pl.BlockSpec((tm, tn), lambda i, j: (i, COL_GA // tn + j)),
                  pl.BlockSpec((tm, tn), lambda i, j: (i, COL_GB // tn + j))],
        out_specs=pl.BlockSpec((tm, tn), lambda i, j: (i, j)),
        out_shape=jax.ShapeDtypeStruct((seq, D_MODEL), BF16),
        compiler_params=_params(("arbitrary", "arbitrary")),
        name="gated_merge",
    )(o_nsa, o_ssm, pa, pb, main, main)


def _outproj_kernel(m_ref, w_ref, x_ref, nw_ref, h_ref, hn_ref):
    h = x_ref[...] + _dot(m_ref[...], w_ref[...])
    h_ref[...] = h
    ms = jnp.mean(h * h, axis=-1, keepdims=True)
    hn_ref[...] = (h * lax.rsqrt(ms + EPS) * nw_ref[...]).astype(hn_ref.dtype)


def _outproj(merged, w_out, x2, norm_w, seq):
    tm = min(seq, 256)
    row = pl.BlockSpec((tm, D_MODEL), lambda i: (i, 0))
    return pl.pallas_call(
        _outproj_kernel,
        grid=(seq // tm,),
        in_specs=[row, pl.BlockSpec((D_MODEL, D_MODEL), lambda i: (0, 0)), row,
                  pl.BlockSpec((1, D_MODEL), lambda i: (0, 0))],
        out_specs=[row, row],
        out_shape=[jax.ShapeDtypeStruct((seq, D_MODEL), F32), jax.ShapeDtypeStruct((seq, D_MODEL), BF16)],
        compiler_params=_params(("arbitrary",)),
        name="outproj_residual_norm",
    )(merged, w_out, x2, norm_w.reshape(1, D_MODEL))


def _mlp_kernel(hn_ref, wu_ref, wd_ref, h_ref, nw_ref, o_ref, acc_scr):
    f = pl.program_id(1)

    @pl.when(f == 0)
    def _():
        acc_scr[...] = jnp.zeros(acc_scr.shape, F32)

    up = jnp.maximum(_dot(hn_ref[...], wu_ref[...]), 0.0)
    acc_scr[...] += _dot((up * up).astype(BF16), wd_ref[...])

    @pl.when(f == pl.num_programs(1) - 1)
    def _():
        h = h_ref[...] + acc_scr[...]
        ms = jnp.mean(h * h, axis=-1, keepdims=True)
        o_ref[...] = h * lax.rsqrt(ms + EPS) * nw_ref[...]


def _mlp(hn, w_up, w_down, h1, norm_w, seq):
    tm = min(seq, 512)
    tf = 512
    row = lambda i, f: (i, 0)
    return pl.pallas_call(
        _mlp_kernel,
        grid=(seq // tm, D_FF // tf),
        in_specs=[pl.BlockSpec((tm, D_MODEL), row),
                  pl.BlockSpec((D_MODEL, tf), lambda i, f: (0, f)),
                  pl.BlockSpec((tf, D_MODEL), lambda i, f: (f, 0)),
                  pl.BlockSpec((tm, D_MODEL), row),
                  pl.BlockSpec((1, D_MODEL), lambda i, f: (0, 0))],
        out_specs=pl.BlockSpec((tm, D_MODEL), row),
        out_shape=jax.ShapeDtypeStruct((seq, D_MODEL), F32),
        scratch_shapes=[pltpu.VMEM((tm, D_MODEL), F32)],
        compiler_params=_params(("arbitrary", "arbitrary")),
        name="mlp_final_norm",
    )(hn, w_up, w_down, h1, norm_w.reshape(1, D_MODEL))


def _cmp_to_sel(seq):
    n_cmp = (seq - CMP_BLOCK) // CMP_STRIDE + 1
    n_sel = seq // SEL_BLOCK
    c_start = np.arange(n_cmp) * CMP_STRIDE
    s_start = np.arange(n_sel) * SEL_BLOCK
    overlap = np.clip(np.minimum(c_start[:, None] + CMP_BLOCK, s_start[None, :] + SEL_BLOCK)
                      - np.maximum(c_start[:, None], s_start[None, :]), 0, None)
    m = np.zeros((seq // CMP_STRIDE, LANES), np.float32)
    m[:n_cmp, :n_sel] = overlap / CMP_STRIDE
    return jnp.asarray(m.T, BF16)


def kernel(x, positions, norm_mix_w, w_in, cmp_pos_k, cmp_pos_v, cmp_k_w1, cmp_k_w2, cmp_v_w1, cmp_v_w2, conv_w, conv_b, dt_bias, a_log, ssm_d, ssm_norm_w, w_proj_nsa, w_proj_ssm, w_out, norm_mlp_w, w_up, w_down, norm_final_w):
    bsz, seq, _ = x.shape
    assert bsz == 1 and seq % 1024 == 0 and SEL_TOPK <= seq // SEL_BLOCK <= LANES
    assert w_in.shape[0] == 1, "one layer"
    x2 = x.reshape(seq, D_MODEL)

    wi = w_in[0]
    o_kc, o_vc, o_ks, o_vs, o_kw, o_vw, o_gate = 2048, 2560, 3072, 3584, 4096, 4608, 5120
    o_z = o_gate + 3 * NSA_HEADS
    o_xbc = o_z + SSM_D_INNER
    o_dt = o_xbc + SSM_D_INNER + 2 * SSM_GROUPS * SSM_STATE
    o_bg = o_dt + SSM_HEADS
    cols = lambda a, n: wi[:, a:a + n]
    w_main = jnp.concatenate([
        cols(o_z, 4096), cols(o_xbc, 4096), cols(o_xbc + 4096, 1024), cols(o_xbc + 5120, 1024),
        cols(0, 2048), cols(o_ks, 512), cols(o_kw, 512), cols(o_kc, 512), cols(o_vc, 512),
        cols(o_vs, 512), cols(o_vw, 512), cols(o_bg, 2048), cols(o_bg + 2048, 2048)], axis=1).astype(BF16)
    zpad = lambda n: jnp.zeros((D_MODEL, n), F32)
    per_group = 3 * NSA_REP
    small_parts = [cols(o_dt, SSM_HEADS), zpad(LANES - SSM_HEADS)]
    for g in range(NSA_GROUPS):
        small_parts += [cols(o_gate + g * per_group, per_group), zpad(LANES - per_group)]
    w_small = jnp.concatenate(small_parts, axis=1).astype(BF16)

    cosf, sinf = _rope_tables(positions, seq)
    main, small = _inproj(x2, norm_mix_w[0], w_main, w_small, cosf, sinf)

    ncp = seq // CMP_STRIDE
    def chunks(col):
        a = main[:, col:col + NSA_GROUPS * DH].reshape(ncp, CMP_STRIDE, NSA_GROUPS, DH)
        return a.transpose(2, 0, 1, 3).reshape(NSA_GROUPS, ncp, CMP_STRIDE * DH)
    def pos2(p):
        p = p.reshape(2, CMP_STRIDE * DH)
        return jnp.concatenate([p, jnp.zeros((14, CMP_STRIDE * DH), F32)], axis=0).astype(BF16)
    end_rows = slice(CMP_BLOCK - 1, None, CMP_STRIDE)
    pad_row = lambda a: jnp.concatenate([a[end_rows], jnp.zeros((ncp - a[end_rows].shape[0], DH), F32)], axis=0)
    kc, vc = _compress(chunks(COL_KC), chunks(COL_VC), pos2(cmp_pos_k[0]), pos2(cmp_pos_v[0]),
                       cmp_k_w1[0].astype(BF16), cmp_k_w2[0].astype(BF16),
                       cmp_v_w1[0].astype(BF16), cmp_v_w2[0].astype(BF16), pad_row(cosf), pad_row(sinf))

    o_cmp, sel_bias = _cmp_attn(main, kc, vc, _cmp_to_sel(seq), seq)
    o_nsa = _nsa_attn(main, small, sel_bias, o_cmp, seq)

    cw = conv_w[0]
    cb_ = conv_b[0].reshape(1, -1)
    lane_pad = lambda a: jnp.concatenate([a.reshape(1, -1), jnp.zeros((1, LANES - a.shape[-1]), F32)], axis=1)
    head_of = np.arange(SSM_D_INNER) // SSM_HEAD_DIM
    expand = jnp.asarray(np.arange(LANES)[:, None] == head_of[None, :], BF16)
    tril = jnp.asarray(np.tril(np.ones((SSD_CHUNK, SSD_CHUNK), np.float32)), BF16)
    o_ssm = _mamba(main, small, cw[:, :4096], cw[:, 4096:5120], cw[:, 5120:],
                   cb_[:, :4096], cb_[:, 4096:5120], cb_[:, 5120:],
                   lane_pad(dt_bias[0]), lane_pad(a_log[0]),
                   jnp.repeat(ssm_d[0], SSM_HEAD_DIM).reshape(1, -1), ssm_norm_w[0].reshape(1, -1),
                   expand, tril, seq)

    merged = _merge(o_nsa, o_ssm, w_proj_nsa[0].astype(BF16), w_proj_ssm[0].astype(BF16), main, seq)
    h1, hn = _outproj(merged, w_out[0].astype(BF16), x2, norm_mlp_w[0], seq)
    out = _mlp(hn, w_up[0].astype(BF16), w_down[0].astype(BF16), h1, norm_final_w, seq)
    return out.reshape(bsz, seq, D_MODEL)
```

```python
import functools
import math

import jax
import jax.numpy as jnp
import numpy as np
from jax import lax
from jax.experimental import pallas as pl
from jax.experimental.pallas import tpu as pltpu

F32 = jnp.float32
BF16 = jnp.bfloat16

D_MODEL = 2048
NSA_HEADS = 16
NSA_GROUPS = 4
NSA_REP = NSA_HEADS // NSA_GROUPS
DH = 128
CMP_BLOCK = 32
CMP_STRIDE = 16
SEL_BLOCK = 64
SEL_TOPK = 16
WINDOW = 512
ROPE_THETA = 10000.0
FORCED_SCORE = 1.0e4
SSM_D_INNER = 4096
SSM_HEAD_DIM = 64
SSM_HEADS = 64
SSM_GROUPS = 8
SSM_STATE = 128
CONV_WIDTH = 4
SSD_CHUNK = 128
D_FF = 4 * D_MODEL
EPS = 1e-6
NEG_INF = -1e30
TINY = 1e-30
MASK_NEG = -1e9
TAKEN = -2.0
LANES = 128
VMEM_LIMIT = 56 * 1024 * 1024

COL_Z = 0
COL_XS = 4096
COL_B = 8192
COL_C = 9216
COL_Q = 10240
COL_KS = 12288
COL_KW = 12800
COL_KC = 13312
COL_VC = 13824
COL_VS = 14336
COL_VW = 14848
COL_GA = 15360
COL_GB = 17408
MAIN_W = 19456
SMALL_W = 640


def _dot(a, b):
    return jnp.dot(a, b, preferred_element_type=F32)


def _dot_nt(a, b):
    return lax.dot_general(a, b, (((1,), (1,)), ((), ())), preferred_element_type=F32)


def _dot_tn(a, b):
    return lax.dot_general(a, b, (((0,), (0,)), ((), ())), preferred_element_type=F32)


def _params(sem):
    return pltpu.CompilerParams(dimension_semantics=sem, vmem_limit_bytes=VMEM_LIMIT)


def _rope(x, cos, sin_signed):
    return x * cos + pltpu.roll(x, DH // 2, 1) * sin_signed


def _split2(x):
    hi = x.astype(BF16)
    lo = (x - hi.astype(F32)).astype(BF16)
    return hi, lo


def _split3(x):
    hi = x.astype(BF16)
    r = x - hi.astype(F32)
    mid = r.astype(BF16)
    lo = (r - mid.astype(F32)).astype(BF16)
    return hi, mid, lo


def _rope_table_kernel(pos_ref, inv_ref, sign_ref, cos_ref, sin_ref):
    ang = pos_ref[...].astype(F32) * inv_ref[...]
    cos_ref[...] = jnp.cos(ang)
    sin_ref[...] = jnp.sin(ang) * sign_ref[...]


def _rope_tables(positions, seq):
    half = DH // 2
    inv = ROPE_THETA ** (-jnp.arange(0, DH, 2, dtype=F32) / DH)
    inv_full = jnp.concatenate([inv, inv]).reshape(1, DH)
    sign = jnp.concatenate([-jnp.ones((half,), F32), jnp.ones((half,), F32)]).reshape(1, DH)
    tm = min(seq, 1024)
    return pl.pallas_call(
        _rope_table_kernel,
        grid=(seq // tm,),
        in_specs=[pl.BlockSpec((tm, 1), lambda i: (i, 0)),
                  pl.BlockSpec((1, DH), lambda i: (0, 0)),
                  pl.BlockSpec((1, DH), lambda i: (0, 0))],
        out_specs=[pl.BlockSpec((tm, DH), lambda i: (i, 0))] * 2,
        out_shape=[jax.ShapeDtypeStruct((seq, DH), F32)] * 2,
        compiler_params=_params(("arbitrary",)),
        name="rope_tables",
    )(positions.reshape(seq, 1), inv_full, sign)


INPROJ_TN = 512
ROPE_TILE_LO = COL_Q // INPROJ_TN
ROPE_TILE_HI = COL_KC // INPROJ_TN
Q_TILE_HI = COL_KS // INPROJ_TN


def _inproj_kernel(x_ref, nw_ref, w_ref, ws_ref, cos_ref, sin_ref, o_ref, small_ref, u_scr):
    j = pl.program_id(1)

    @pl.when(j == 0)
    def _():
        xf = x_ref[...]
        ms = jnp.mean(xf * xf, axis=-1, keepdims=True)
        u = (xf * lax.rsqrt(ms + EPS) * nw_ref[...]).astype(BF16)
        u_scr[...] = u
        small_ref[...] = _dot(u, ws_ref[...])

    acc = _dot(u_scr[...], w_ref[...])
    is_rope = jnp.logical_and(j >= ROPE_TILE_LO, j < ROPE_TILE_HI)

    @pl.when(jnp.logical_not(is_rope))
    def _():
        o_ref[...] = acc.astype(o_ref.dtype)

    @pl.when(is_rope)
    def _():
        cos = cos_ref[...]
        sin = sin_ref[...]
        scale = jnp.where(j < Q_TILE_HI, DH ** -0.5 * math.log2(math.e), 1.0).astype(F32)
        for h in range(INPROJ_TN // DH):
            xh = acc[:, h * DH:(h + 1) * DH]
            o_ref[:, h * DH:(h + 1) * DH] = (_rope(xh, cos, sin) * scale).astype(o_ref.dtype)


def _inproj(x2, norm_w, w_main, w_small, cosf, sinf):
    seq = x2.shape[0]
    tm = min(seq, 1024)
    tn = INPROJ_TN
    return pl.pallas_call(
        _inproj_kernel,
        grid=(seq // tm, MAIN_W // tn),
        in_specs=[pl.BlockSpec((tm, D_MODEL), lambda i, j: (i, 0)),
                  pl.BlockSpec((1, D_MODEL), lambda i, j: (0, 0)),
                  pl.BlockSpec((D_MODEL, tn), lambda i, j: (0, j)),
                  pl.BlockSpec((D_MODEL, SMALL_W), lambda i, j: (0, 0)),
                  pl.BlockSpec((tm, DH), lambda i, j: (i, 0)),
                  pl.BlockSpec((tm, DH), lambda i, j: (i, 0))],
        out_specs=[pl.BlockSpec((tm, tn), lambda i, j: (i, j)),
                   pl.BlockSpec((tm, SMALL_W), lambda i, j: (i, 0))],
        out_shape=[jax.ShapeDtypeStruct((seq, MAIN_W), BF16),
                   jax.ShapeDtypeStruct((seq, SMALL_W), F32)],
        scratch_shapes=[pltpu.VMEM((tm, D_MODEL), BF16)],
        compiler_params=_params(("arbitrary", "arbitrary")),
        name="inproj",
    )(x2, norm_w.reshape(1, D_MODEL), w_main, w_small, cosf, sinf)


def _compress_kernel(xk_ref, xv_ref, pk_ref, pv_ref, w1k_ref, w2k_ref, w1v_ref, w2v_ref,
                     cos_ref, sin_ref, kc_ref, vc_ref):
    half = CMP_STRIDE * DH
    ncp = xk_ref.shape[1]

    def comp(x_ref, p_ref, w1_ref, w2_ref):
        x = x_ref[0]
        wt = w1_ref[0:half, :]
        wb = w1_ref[half:2 * half, :]
        a = _dot(x, wt)
        b = _dot(x, wb)
        pb = _dot(p_ref[...], wt)[0:1] + _dot(p_ref[...], wb)[1:2]
        h = a + pltpu.roll(b, ncp - 1, 0) + pb
        h = h * jax.nn.sigmoid(h)
        return _dot(h.astype(BF16), w2_ref[...])

    kc = comp(xk_ref, pk_ref, w1k_ref, w2k_ref)
    kc_ref[0] = _rope(kc, cos_ref[...], sin_ref[...]).astype(BF16)
    vc_ref[0] = comp(xv_ref, pv_ref, w1v_ref, w2v_ref).astype(BF16)


def _compress(xk, xv, pk, pv, w1k, w2k, w1v, w2v, cos_c, sin_c):
    g, ncp, width = xk.shape
    full = lambda shape: pl.BlockSpec(shape, lambda i: (0,) * len(shape))
    per_g = pl.BlockSpec((1, ncp, width), lambda i: (i, 0, 0))
    out_g = pl.BlockSpec((1, ncp, DH), lambda i: (i, 0, 0))
    return pl.pallas_call(
        _compress_kernel,
        grid=(g,),
        in_specs=[per_g, per_g, full(pk.shape), full(pv.shape), full(w1k.shape), full(w2k.shape),
                  full(w1v.shape), full(w2v.shape), full(cos_c.shape), full(sin_c.shape)],
        out_specs=[out_g, out_g],
        out_shape=[jax.ShapeDtypeStruct((g, ncp, DH), BF16)] * 2,
        compiler_params=_params(("arbitrary",)),
        name="compress",
    )(xk, xv, pk, pv, w1k, w2k, w1v, w2v, cos_c, sin_c)


CMP_TQ = 128


def _cmp_attn_kernel(q_ref, kc_ref, vc_ref, mt_ref, oc_ref, mask_ref):
    i = pl.program_id(0)
    qbase = i * CMP_TQ
    rows = NSA_REP * CMP_TQ
    ncp = kc_ref.shape[1]
    nsel = mt_ref.shape[0]
    row = lax.broadcasted_iota(jnp.int32, (rows, ncp), 0)
    col = lax.broadcasted_iota(jnp.int32, (rows, ncp), 1)
    t = qbase + (row & (CMP_TQ - 1))
    valid = (col * CMP_STRIDE + (CMP_BLOCK - 1)) <= t
    blk = lax.broadcasted_iota(jnp.int32, (nsel, CMP_TQ), 0)
    cur = (qbase + lax.broadcasted_iota(jnp.int32, (nsel, CMP_TQ), 1)) >> 6
    future = blk > cur
    forced = (blk == 0) | (blk == cur) | (blk == cur - 1)
    blk_f = blk.astype(F32)
    mt = mt_ref[...]
    for g in range(NSA_GROUPS):
        qg = jnp.concatenate(
            [q_ref[:, (g * NSA_REP + r) * DH:(g * NSA_REP + r + 1) * DH] for r in range(NSA_REP)], axis=0)
        s = jnp.where(valid, _dot_nt(qg, kc_ref[g]), NEG_INF)
        m = jnp.max(s, axis=-1, keepdims=True)
        p = jnp.where(valid, jnp.exp2(s - m), 0.0)
        p = p / jnp.maximum(jnp.sum(p, axis=-1, keepdims=True), TINY)
        o = _dot(p.astype(BF16), vc_ref[g])
        for r in range(NSA_REP):
            h = g * NSA_REP + r
            oc_ref[:, h * DH:(h + 1) * DH] = o[r * CMP_TQ:(r + 1) * CMP_TQ].astype(oc_ref.dtype)
        ps = p[0:CMP_TQ]
        for r in range(1, NSA_REP):
            ps = ps + p[r * CMP_TQ:(r + 1) * CMP_TQ]
        ps_hi, ps_lo = _split2(ps)
        imp_t = _dot_nt(mt, ps_hi) + _dot_nt(mt, ps_lo)
        v = jnp.where(future, -1.0, jnp.where(forced, FORCED_SCORE, imp_t))
        for _ in range(SEL_TOPK):
            mx = jnp.max(v, axis=0, keepdims=True)
            first = jnp.min(jnp.where(v == mx, blk_f, float(nsel)), axis=0, keepdims=True)
            v = jnp.where(blk_f == first, TAKEN, v)
        bias_t = jnp.where(v == TAKEN, 0.0, MASK_NEG)
        mask_ref[g] = bias_t.T.astype(mask_ref.dtype)


def _cmp_attn(main, kc, vc, mt, seq):
    nsel = mt.shape[0]
    ncp = kc.shape[1]
    return pl.pallas_call(
        _cmp_attn_kernel,
        grid=(seq // CMP_TQ,),
        in_specs=[pl.BlockSpec((CMP_TQ, NSA_HEADS * DH), lambda i: (i, COL_Q // (NSA_HEADS * DH))),
                  pl.BlockSpec((NSA_GROUPS, ncp, DH), lambda i: (0, 0, 0)),
                  pl.BlockSpec((NSA_GROUPS, ncp, DH), lambda i: (0, 0, 0)),
                  pl.BlockSpec((nsel, ncp), lambda i: (0, 0))],
        out_specs=[pl.BlockSpec((CMP_TQ, NSA_HEADS * DH), lambda i: (i, 0)),
                   pl.BlockSpec((NSA_GROUPS, CMP_TQ, nsel), lambda i: (0, i, 0))],
        out_shape=[jax.ShapeDtypeStruct((seq, NSA_HEADS * DH), BF16),
                   jax.ShapeDtypeStruct((NSA_GROUPS, seq, nsel), BF16)],
        compiler_params=_params(("arbitrary",)),
        name="cmp_attn_topk",
    )(main, kc, vc, mt)


ATT_TQ = 512
ATT_TK = 512
WIN_SPAN = WINDOW + ATT_TQ


def _nsa_attn_kernel(q_ref, mask_ref, ks_ref, vs_ref, kw_ref, vw_ref, oc_ref, gate_ref, o_ref,
                     qa_scr, acc_scr, m_scr):
    i = pl.program_id(1)
    qbase = i * ATT_TQ
    assert mask_ref.shape[2] == LANES, "the block-selection bias rides in one extra 128-wide contraction slab"
    bias = mask_ref[0]
    for r in range(NSA_REP):
        qa_scr[r] = jnp.concatenate([q_ref[:, r * DH:(r + 1) * DH], bias], axis=1)
    m_scr[...] = jnp.full(m_scr.shape, NEG_INF, F32)
    acc_scr[...] = jnp.zeros(acc_scr.shape, F32)
    krow = lax.broadcasted_iota(jnp.int32, (ATT_TK, LANES), 0)
    kblk = lax.broadcasted_iota(jnp.int32, (ATT_TK, LANES), 1)

    def step(j, diagonal):
        k0 = pl.multiple_of(j * ATT_TK, ATT_TK)
        onehot = jnp.where(((krow + k0) >> 6) == kblk, 1.0, 0.0).astype(BF16)
        k_aug = jnp.concatenate([ks_ref[pl.ds(k0, ATT_TK), :], onehot], axis=1)
        v_aug = jnp.concatenate([vs_ref[pl.ds(k0, ATT_TK), :], jnp.ones((ATT_TK, LANES), BF16)], axis=1)
        if diagonal:
            r_ = lax.broadcasted_iota(jnp.int32, (ATT_TQ, ATT_TK), 0)
            c_ = lax.broadcasted_iota(jnp.int32, (ATT_TQ, ATT_TK), 1)
            causal_bias = jnp.where(c_ <= r_, 0.0, MASK_NEG)
        for r in range(NSA_REP):
            s = _dot_nt(qa_scr[r], k_aug)
            if diagonal:
                s = s + causal_bias
            m_prev = m_scr[r]
            m_new = jnp.maximum(m_prev, jnp.max(s, axis=-1, keepdims=True))
            alpha = jnp.exp2(m_prev - m_new)
            p = jnp.exp2(s - jnp.concatenate([m_new] * (ATT_TK // LANES), axis=1))
            acc_scr[r] = acc_scr[r] * jnp.concatenate([alpha, alpha], axis=1) + _dot(p.astype(BF16), v_aug)
            m_scr[r] = m_new

    def body(j, carry):
        step(j, False)
        return carry

    lax.fori_loop(0, i, body, 0)
    step(i, True)

    start = pl.multiple_of(jnp.maximum(qbase - WINDOW, 0), ATT_TQ)
    kw = kw_ref[pl.ds(start, WIN_SPAN), :]
    vw_aug = jnp.concatenate([vw_ref[pl.ds(start, WIN_SPAN), :], jnp.ones((WIN_SPAN, LANES), BF16)], axis=1)
    r_ = lax.broadcasted_iota(jnp.int32, (ATT_TQ, WIN_SPAN), 0)
    c_ = lax.broadcasted_iota(jnp.int32, (ATT_TQ, WIN_SPAN), 1)
    diff = (qbase + r_) - (start + c_)
    win_bias = jnp.where((diff >= 0) & (diff < WINDOW), 0.0, MASK_NEG)
    gates = jax.nn.sigmoid(gate_ref[...])
    for r in range(NSA_REP):
        s = _dot_nt(q_ref[:, r * DH:(r + 1) * DH], kw) + win_bias
        p = jnp.exp2(s - jnp.max(s, axis=-1, keepdims=True))
        ow = _dot(p.astype(BF16), vw_aug)
        o_win = ow[:, :DH] / jnp.maximum(ow[:, DH:], TINY)
        acc = acc_scr[r]
        o_sel = acc[:, :DH] / jnp.maximum(acc[:, DH:], TINY)
        o = (gates[:, 3 * r:3 * r + 1] * oc_ref[:, r * DH:(r + 1) * DH].astype(F32)
             + gates[:, 3 * r + 1:3 * r + 2] * o_sel
             + gates[:, 3 * r + 2:3 * r + 3] * o_win)
        o_ref[:, r * DH:(r + 1) * DH] = o.astype(o_ref.dtype)


def _nsa_attn(main, small, mask, o_cmp, seq):
    gw = NSA_REP * DH
    nsel = mask.shape[2]
    kv_spec = lambda col: pl.BlockSpec((seq, DH), lambda g, i: (0, col // DH + g))
    return pl.pallas_call(
        _nsa_attn_kernel,
        grid=(NSA_GROUPS, seq // ATT_TQ),
        in_specs=[pl.BlockSpec((ATT_TQ, gw), lambda g, i: (i, COL_Q // gw + g)),
                  pl.BlockSpec((1, ATT_TQ, nsel), lambda g, i: (g, i, 0)),
                  kv_spec(COL_KS), kv_spec(COL_VS), kv_spec(COL_KW), kv_spec(COL_VW),
                  pl.BlockSpec((ATT_TQ, gw), lambda g, i: (i, g)),
                  pl.BlockSpec((ATT_TQ, LANES), lambda g, i: (i, 1 + g))],
        out_specs=pl.BlockSpec((ATT_TQ, gw), lambda g, i: (i, g)),
        out_shape=jax.ShapeDtypeStruct((seq, NSA_HEADS * DH), BF16),
        scratch_shapes=[pltpu.VMEM((NSA_REP, ATT_TQ, 2 * DH), BF16),
                        pltpu.VMEM((NSA_REP, ATT_TQ, 2 * DH), F32),
                        pltpu.VMEM((NSA_REP, ATT_TQ, LANES), F32)],
        compiler_params=_params(("arbitrary", "arbitrary")),
        name="nsa_sel_win_attn",
    )(main, mask, main, main, main, main, o_cmp, small)


HALO = 8
GROUP_W = SSM_D_INNER // SSM_GROUPS
HEADS_PER_GROUP = SSM_HEADS // SSM_GROUPS


def _mamba_kernel(z_ref, xs_ref, b_ref, c_ref, xsh_ref, bh_ref, ch_ref, dt_ref,
                  cwx_ref, cwb_ref, cwc_ref, cbx_ref, cbb_ref, cbc_ref,
                  dtb_ref, alog_ref, dfull_ref, nw_ref, expand_ref, tril_ref,
                  o_ref, state_scr):
    c = pl.program_id(0)
    L = SSD_CHUNK

    @pl.when(c == 0)
    def _():
        state_scr[...] = jnp.zeros(state_scr.shape, F32)

    def conv_silu(x_ref, h_ref, w_ref, bias_ref):
        x = x_ref[...].astype(F32)
        halo = jnp.where(c == 0, 0.0, h_ref[...].astype(F32))
        ext = jnp.concatenate([halo, x], axis=0)
        acc = bias_ref[...] + w_ref[CONV_WIDTH - 1:CONV_WIDTH, :] * x
        for k in range(CONV_WIDTH - 1):
            off = HALO - (CONV_WIDTH - 1) + k
            acc = acc + w_ref[k:k + 1, :] * ext[off:off + L, :]
        return acc * jax.nn.sigmoid(acc)

    xs = conv_silu(xs_ref, xsh_ref, cwx_ref, cbx_ref)
    bm = conv_silu(b_ref, bh_ref, cwb_ref, cbb_ref).astype(BF16)
    cm = conv_silu(c_ref, ch_ref, cwc_ref, cbc_ref).astype(BF16)

    xdt_raw = dt_ref[...] + dtb_ref[...]
    dt = jnp.maximum(xdt_raw, 0.0) + jnp.log1p(jnp.exp(-jnp.abs(xdt_raw)))
    adt = dt * (-jnp.exp(alog_ref[...]))
    tril = tril_ref[...]
    a_hi, a_mid, a_lo = _split3(adt)
    acs = _dot(tril, a_hi) + _dot(tril, a_mid) + _dot(tril, a_lo)
    acs_t = acs.T
    a_last = acs[L - 1:L, :]
    stacked = jnp.concatenate([dt, jnp.exp(acs), jnp.exp(a_last - acs)], axis=0)
    s_hi, s_lo = _split2(stacked)
    expand = expand_ref[...]
    wide = _dot(s_hi, expand) + _dot(s_lo, expand)
    dt_w = wide[0:L]
    in_decay_w = wide[L:2 * L]
    out_decay_w = wide[2 * L:3 * L]

    xdt = xs * dt_w
    xst = xdt * out_decay_w
    causal = lax.broadcasted_iota(jnp.int32, (L, L), 0) >= lax.broadcasted_iota(jnp.int32, (L, L), 1)
    lane = lax.broadcasted_iota(jnp.int32, (L, LANES), 1)
    low_half = lane < SSM_HEAD_DIM
    chunk_decay = jnp.exp(acs_t[:, L - 1:L])

    for g in range(SSM_GROUPS):
        gs = slice(g * GROUP_W, (g + 1) * GROUP_W)
        bg = bm[:, g * SSM_STATE:(g + 1) * SSM_STATE]
        cg = cm[:, g * SSM_STATE:(g + 1) * SSM_STATE]
        cb = _dot_nt(cg, bg)
        prev = state_scr[gs, :]
        y_off = _dot_nt(cg, prev.astype(BF16)) * in_decay_w[:, gs]
        pieces = []
        for hp in range(HEADS_PER_GROUP // 2):
            lhs = []
            for e in range(2):
                h = g * HEADS_PER_GROUP + 2 * hp + e
                seg = acs[:, h:h + 1] - acs_t[h:h + 1, :]
                decay = jnp.where(causal, jnp.exp(jnp.where(causal, seg, 0.0)), 0.0)
                lhs.append((cb * decay).astype(BF16))
            slab = xdt[:, g * GROUP_W + hp * LANES:g * GROUP_W + (hp + 1) * LANES]
            rhs = jnp.concatenate([jnp.where(low_half, slab, 0.0), jnp.where(low_half, 0.0, slab)], axis=0)
            pieces.append(_dot(jnp.concatenate(lhs, axis=1), rhs.astype(BF16)))
        y_g = jnp.concatenate(pieces, axis=1) + y_off
        y_g = y_g + dfull_ref[:, gs] * xs[:, gs]
        zg = z_ref[:, gs].astype(F32)
        y_g = y_g * (zg * jax.nn.sigmoid(zg))
        ms = jnp.mean(y_g * y_g, axis=-1, keepdims=True)
        o_ref[:, gs] = (y_g * lax.rsqrt(ms + EPS) * nw_ref[:, gs]).astype(o_ref.dtype)
        new = _dot(xst[:, gs].T.astype(BF16), bg)
        cd = jnp.broadcast_to(chunk_decay[g * HEADS_PER_GROUP:(g + 1) * HEADS_PER_GROUP], (HEADS_PER_GROUP, LANES))
        cd = jnp.broadcast_to(cd[:, None, :], (HEADS_PER_GROUP, SSM_HEAD_DIM, LANES)).reshape(GROUP_W, LANES)
        state_scr[gs, :] = prev * cd + new


def _mamba(main, small, cwx, cwb, cwc, cbx, cbb, cbc, dtb, alog, dfull, nw, expand, tril, seq):
    L = SSD_CHUNK
    bw = SSM_GROUPS * SSM_STATE
    halo_idx = lambda col_blk: (lambda c: (jnp.maximum(c * (L // HALO) - 1, 0), col_blk))
    full = lambda a: pl.BlockSpec(a.shape, lambda c: (0,) * a.ndim)
    consts = [cwx, cwb, cwc, cbx, cbb, cbc, dtb, alog, dfull, nw, expand, tril]
    return pl.pallas_call(
        _mamba_kernel,
        grid=(seq // L,),
        in_specs=[pl.BlockSpec((L, SSM_D_INNER), lambda c: (c, COL_Z // SSM_D_INNER)),
                  pl.BlockSpec((L, SSM_D_INNER), lambda c: (c, COL_XS // SSM_D_INNER)),
                  pl.BlockSpec((L, bw), lambda c: (c, COL_B // bw)),
                  pl.BlockSpec((L, bw), lambda c: (c, COL_C // bw)),
                  pl.BlockSpec((HALO, SSM_D_INNER), halo_idx(COL_XS // SSM_D_INNER)),
                  pl.BlockSpec((HALO, bw), halo_idx(COL_B // bw)),
                  pl.BlockSpec((HALO, bw), halo_idx(COL_C // bw)),
                  pl.BlockSpec((L, LANES), lambda c: (c, 0))] + [full(a) for a in consts],
        out_specs=pl.BlockSpec((L, SSM_D_INNER), lambda c: (c, 0)),
        out_shape=jax.ShapeDtypeStruct((seq, SSM_D_INNER), BF16),
        scratch_shapes=[pltpu.VMEM((SSM_D_INNER, SSM_STATE), F32)],
        compiler_params=_params(("arbitrary",)),
        name="mamba2_ssd",
    )(main, main, main, main, main, main, main, small, *consts)


def _merge_kernel(a_ref, b_ref, pa_ref, pb_ref, ga_ref, gb_ref, o_ref):
    ya = _dot(a_ref[...], pa_ref[...])
    yb = _dot(b_ref[...], pb_ref[...])
    o = jax.nn.sigmoid(ga_ref[...].astype(F32)) * ya + jax.nn.sigmoid(gb_ref[...].astype(F32)) * yb
    o_ref[...] = o.astype(o_ref.dtype)


def _merge(o_nsa, o_ssm, pa, pb, main, seq):
    tm = min(seq, 512)
    tn = 512
    return pl.pallas_call(
        _merge_kernel,
        grid=(seq // tm, D_MODEL // tn),
        in_specs=[pl.BlockSpec((tm, NSA_HEADS * DH), lambda i, j: (i, 0)),
                  pl.BlockSpec((tm, SSM_D_INNER), lambda i, j: (i, 0)),
                  pl.BlockSpec((NSA_HEADS * DH, tn), lambda i, j: (0, j)),
                  pl.BlockSpec((SSM_D_INNER, tn), lambda i, j: (0, j)),
                  pl.BlockSpec((tm, tn), lambda i, j: (i, COL_GA // tn + j)),
                  pl.BlockSpec((tm, tn), lambda i, j: (i, COL_GB // tn + j))],
        out_specs=pl.BlockSpec((tm, tn), lambda i, j: (i, j)),
        out_shape=jax.ShapeDtypeStruct((seq, D_MODEL), BF16),
        compiler_params=_params(("arbitrary", "arbitrary")),
        name="gated_merge",
    )(o_nsa, o_ssm, pa, pb, main, main)


def _outproj_kernel(m_ref, w_ref, x_ref, nw_ref, h_ref, hn_ref):
    h = x_ref[...] + _dot(m_ref[...], w_ref[...])
    h_ref[...] = h
    ms = jnp.mean(h * h, axis=-1, keepdims=True)
    hn_ref[...] = (h * lax.rsqrt(ms + EPS) * nw_ref[...]).astype(hn_ref.dtype)


def _outproj(merged, w_out, x2, norm_w, seq):
    tm = min(seq, 256)
    row = pl.BlockSpec((tm, D_MODEL), lambda i: (i, 0))
    return pl.pallas_call(
        _outproj_kernel,
        grid=(seq // tm,),
        in_specs=[row, pl.BlockSpec((D_MODEL, D_MODEL), lambda i: (0, 0)), row,
                  pl.BlockSpec((1, D_MODEL), lambda i: (0, 0))],
        out_specs=[row, row],
        out_shape=[jax.ShapeDtypeStruct((seq, D_MODEL), F32), jax.ShapeDtypeStruct((seq, D_MODEL), BF16)],
        compiler_params=_params(("arbitrary",)),
        name="outproj_residual_norm",
    )(merged, w_out, x2, norm_w.reshape(1, D_MODEL))


def _mlp_kernel(hn_ref, wu_ref, wd_ref, h_ref, nw_ref, o_ref, acc_scr):
    f = pl.program_id(1)

    @pl.when(f == 0)
    def _():
        acc_scr[...] = jnp.zeros(acc_scr.shape, F32)

    up = jnp.maximum(_dot(hn_ref[...], wu_ref[...]), 0.0)
    acc_scr[...] += _dot((up * up).astype(BF16), wd_ref[...])

    @pl.when(f == pl.num_programs(1) - 1)
    def _():
        h = h_ref[...] + acc_scr[...]
        ms = jnp.mean(h * h, axis=-1, keepdims=True)
        o_ref[...] = h * lax.rsqrt(ms + EPS) * nw_ref[...]


def _mlp(hn, w_up, w_down, h1, norm_w, seq):
    tm = min(seq, 512)
    tf = 512
    row = lambda i, f: (i, 0)
    return pl.pallas_call(
        _mlp_kernel,
        grid=(seq // tm, D_FF // tf),
        in_specs=[pl.BlockSpec((tm, D_MODEL), row),
                  pl.BlockSpec((D_MODEL, tf), lambda i, f: (0, f)),
                  pl.BlockSpec((tf, D_MODEL), lambda i, f: (f, 0)),
                  pl.BlockSpec((tm, D_MODEL), row),
                  pl.BlockSpec((1, D_MODEL), lambda i, f: (0, 0))],
        out_specs=pl.BlockSpec((tm, D_MODEL), row),
        out_shape=jax.ShapeDtypeStruct((seq, D_MODEL), F32),
        scratch_shapes=[pltpu.VMEM((tm, D_MODEL), F32)],
        compiler_params=_params(("arbitrary", "arbitrary")),
        name="mlp_final_norm",
    )(hn, w_up, w_down, h1, norm_w.reshape(1, D_MODEL))


def _cmp_to_sel(seq):
    n_cmp = (seq - CMP_BLOCK) // CMP_STRIDE + 1
    n_sel = seq // SEL_BLOCK
    c_start = np.arange(n_cmp) * CMP_STRIDE
    s_start = np.arange(n_sel) * SEL_BLOCK
    overlap = np.clip(np.minimum(c_start[:, None] + CMP_BLOCK, s_start[None, :] + SEL_BLOCK)
                      - np.maximum(c_start[:, None], s_start[None, :]), 0, None)
    m = np.zeros((seq // CMP_STRIDE, LANES), np.float32)
    m[:n_cmp, :n_sel] = overlap / CMP_STRIDE
    return jnp.asarray(m.T, BF16)


def kernel(x, positions, norm_mix_w, w_in, cmp_pos_k, cmp_pos_v, cmp_k_w1, cmp_k_w2, cmp_v_w1, cmp_v_w2, conv_w, conv_b, dt_bias, a_log, ssm_d, ssm_norm_w, w_proj_nsa, w_proj_ssm, w_out, norm_mlp_w, w_up, w_down, norm_final_w):
    bsz, seq, _ = x.shape
    assert bsz == 1 and seq % 1024 == 0 and SEL_TOPK <= seq // SEL_BLOCK <= LANES
    assert w_in.shape[0] == 1, "one layer"
    x2 = x.reshape(seq, D_MODEL)

    wi = w_in[0]
    o_kc, o_vc, o_ks, o_vs, o_kw, o_vw, o_gate = 2048, 2560, 3072, 3584, 4096, 4608, 5120
    o_z = o_gate + 3 * NSA_HEADS
    o_xbc = o_z + SSM_D_INNER
    o_dt = o_xbc + SSM_D_INNER + 2 * SSM_GROUPS * SSM_STATE
    o_bg = o_dt + SSM_HEADS
    cols = lambda a, n: wi[:, a:a + n]
    w_main = jnp.concatenate([
        cols(o_z, 4096), cols(o_xbc, 4096), cols(o_xbc + 4096, 1024), cols(o_xbc + 5120, 1024),
        cols(0, 2048), cols(o_ks, 512), cols(o_kw, 512), cols(o_kc, 512), cols(o_vc, 512),
        cols(o_vs, 512), cols(o_vw, 512), cols(o_bg, 2048), cols(o_bg + 2048, 2048)], axis=1).astype(BF16)
    zpad = lambda n: jnp.zeros((D_MODEL, n), F32)
    per_group = 3 * NSA_REP
    small_parts = [cols(o_dt, SSM_HEADS), zpad(LANES - SSM_HEADS)]
    for g in range(NSA_GROUPS):
        small_parts += [cols(o_gate + g * per_group, per_group), zpad(LANES - per_group)]
    w_small = jnp.concatenate(small_parts, axis=1).astype(BF16)

    cosf, sinf = _rope_tables(positions, seq)
    main, small = _inproj(x2, norm_mix_w[0], w_main, w_small, cosf, sinf)

    ncp = seq // CMP_STRIDE
    def chunks(col):
        a = main[:, col:col + NSA_GROUPS * DH].reshape(ncp, CMP_STRIDE, NSA_GROUPS, DH)
        return a.transpose(2, 0, 1, 3).reshape(NSA_GROUPS, ncp, CMP_STRIDE * DH)
    def pos2(p):
        p = p.reshape(2, CMP_STRIDE * DH)
        return jnp.concatenate([p, jnp.zeros((14, CMP_STRIDE * DH), F32)], axis=0).astype(BF16)
    end_rows = slice(CMP_BLOCK - 1, None, CMP_STRIDE)
    pad_row = lambda a: jnp.concatenate([a[end_rows], jnp.zeros((ncp - a[end_rows].shape[0], DH), F32)], axis=0)
    kc, vc = _compress(chunks(COL_KC), chunks(COL_VC), pos2(cmp_pos_k[0]), pos2(cmp_pos_v[0]),
                       cmp_k_w1[0].astype(BF16), cmp_k_w2[0].astype(BF16),
                       cmp_v_w1[0].astype(BF16), cmp_v_w2[0].astype(BF16), pad_row(cosf), pad_row(sinf))

    o_cmp, sel_bias = _cmp_attn(main, kc, vc, _cmp_to_sel(seq), seq)
    o_nsa = _nsa_attn(main, small, sel_bias, o_cmp, seq)

    cw = conv_w[0]
    cb_ = conv_b[0].reshape(1, -1)
    lane_pad = lambda a: jnp.concatenate([a.reshape(1, -1), jnp.zeros((1, LANES - a.shape[-1]), F32)], axis=1)
    head_of = np.arange(SSM_D_INNER) // SSM_HEAD_DIM
    expand = jnp.asarray(np.arange(LANES)[:, None] == head_of[None, :], BF16)
    tril = jnp.asarray(np.tril(np.ones((SSD_CHUNK, SSD_CHUNK), np.float32)), BF16)
    o_ssm = _mamba(main, small, cw[:, :4096], cw[:, 4096:5120], cw[:, 5120:],
                   cb_[:, :4096], cb_[:, 4096:5120], cb_[:, 5120:],
                   lane_pad(dt_bias[0]), lane_pad(a_log[0]),
                   jnp.repeat(ssm_d[0], SSM_HEAD_DIM).reshape(1, -1), ssm_norm_w[0].reshape(1, -1),
                   expand, tril, seq)

    merged = _merge(o_nsa, o_ssm, w_proj_nsa[0].astype(BF16), w_proj_ssm[0].astype(BF16), main, seq)
    h1, hn = _outproj(merged, w_out[0].astype(BF16), x2, norm_mlp_w[0], seq)
    out = _mlp(hn, w_up[0].astype(BF16), w_down[0].astype(BF16), h1, norm_final_w, seq)
    return out.reshape(bsz, seq, D_MODEL)
```

```python
import functools
import math

import jax
import jax.numpy as jnp
import numpy as np
from jax import lax
from jax.experimental import pallas as pl
from jax.experimental.pallas import tpu as pltpu

F32 = jnp.float32
BF16 = jnp.bfloat16

D_MODEL = 2048
NSA_HEADS = 16
NSA_GROUPS = 4
NSA_REP = NSA_HEADS // NSA_GROUPS
DH = 128
CMP_BLOCK = 32
CMP_STRIDE = 16
SEL_BLOCK = 64
SEL_TOPK = 16
WINDOW = 512
ROPE_THETA = 10000.0
N_FORCED = 3
SSM_D_INNER = 4096
SSM_HEAD_DIM = 64
SSM_HEADS = 64
SSM_GROUPS = 8
SSM_STATE = 128
CONV_WIDTH = 4
SSD_CHUNK = 128
D_FF = 4 * D_MODEL
EPS = 1e-6
NEG_INF = -1e30
TINY = 1e-30
MASK_NEG = -1e9
TAKEN = -2.0
LOG2E = math.log2(math.e)
LANES = 128
VMEM_LIMIT = 56 * 1024 * 1024

COL_Q = 0
COL_KC = 2048
COL_VC = 2560
COL_KS = 3072
COL_VS = 3584
COL_KW = 4096
COL_VW = 4608
ATT_W = 5120
COL_Z = 0
COL_XS = 4096
COL_B = 8192
COL_C = 9216
COL_GA = 10240
COL_GB = 12288
SSM_W = 14336
SMALL_W = 640


def _dot(a, b):
    return jnp.dot(a, b, preferred_element_type=F32)


def _dot_nt(a, b):
    return lax.dot_general(a, b, (((1,), (1,)), ((), ())), preferred_element_type=F32)


def _dot_tn(a, b):
    return lax.dot_general(a, b, (((0,), (0,)), ((), ())), preferred_element_type=F32)


def _params(sem):
    return pltpu.CompilerParams(dimension_semantics=sem, vmem_limit_bytes=VMEM_LIMIT)


def _rope(x, cos, sin_signed):
    return x * cos + pltpu.roll(x, DH // 2, 1) * sin_signed


def _split2(x):
    hi = x.astype(BF16)
    lo = (x - hi.astype(F32)).astype(BF16)
    return hi, lo


def _split3(x):
    hi = x.astype(BF16)
    r = x - hi.astype(F32)
    mid = r.astype(BF16)
    lo = (r - mid.astype(F32)).astype(BF16)
    return hi, mid, lo


def _rope_table_kernel(pos_ref, inv_ref, sign_ref, cos_ref, sin_ref):
    ang = pos_ref[...].astype(F32) * inv_ref[...]
    cos_ref[...] = jnp.cos(ang)
    sin_ref[...] = jnp.sin(ang) * sign_ref[...]


def _rope_tables(positions, seq):
    half = DH // 2
    inv = ROPE_THETA ** (-jnp.arange(0, DH, 2, dtype=F32) / DH)
    inv_full = jnp.concatenate([inv, inv]).reshape(1, DH)
    sign = jnp.concatenate([-jnp.ones((half,), F32), jnp.ones((half,), F32)]).reshape(1, DH)
    tm = min(seq, 1024)
    return pl.pallas_call(
        _rope_table_kernel,
        grid=(seq // tm,),
        in_specs=[pl.BlockSpec((tm, 1), lambda i: (i, 0)),
                  pl.BlockSpec((1, DH), lambda i: (0, 0)),
                  pl.BlockSpec((1, DH), lambda i: (0, 0))],
        out_specs=[pl.BlockSpec((tm, DH), lambda i: (i, 0))] * 2,
        out_shape=[jax.ShapeDtypeStruct((seq, DH), F32)] * 2,
        compiler_params=_params(("arbitrary",)),
        name="rope_tables",
    )(positions.reshape(seq, 1), inv_full, sign)


INPROJ_TN = 512
Q_TILE_LO = COL_Q // INPROJ_TN
Q_TILE_HI = COL_KC // INPROJ_TN
KC_TILE = COL_KC // INPROJ_TN
VC_TILE = COL_VC // INPROJ_TN
KS_TILE = COL_KS // INPROJ_TN
KW_TILE = COL_KW // INPROJ_TN


def _inproj_kernel(x_ref, nw_ref, w_ref, ws_ref, cos_ref, sin_ref, o_ref, small_ref, kcx_ref, vcx_ref,
                   u_ref, acc_scr):
    j = pl.program_id(1)

    @pl.when(j == 0)
    def _():
        xf = x_ref[...]
        ms = jnp.mean(xf * xf, axis=-1, keepdims=True)
        u = (xf * lax.rsqrt(ms + EPS) * nw_ref[...]).astype(BF16)
        u_ref[...] = u
        small_ref[...] = _dot(u, ws_ref[...])

    acc = _dot(u_ref[...], w_ref[...])
    is_q = jnp.logical_and(j >= Q_TILE_LO, j < Q_TILE_HI)
    is_rope = is_q | (j == KS_TILE) | (j == KW_TILE)

    @pl.when(jnp.logical_not(is_rope))
    def _():
        o_ref[...] = acc.astype(o_ref.dtype)

    @pl.when(is_rope)
    def _():
        cos = cos_ref[...]
        sin = sin_ref[...]
        scale = jnp.where(is_q, DH ** -0.5 * LOG2E, 1.0).astype(F32)
        for h in range(INPROJ_TN // DH):
            xh = acc[:, h * DH:(h + 1) * DH]
            o_ref[:, h * DH:(h + 1) * DH] = (_rope(xh, cos, sin) * scale).astype(o_ref.dtype)

    def chunked(dst_ref):
        n_chunks = acc_scr.shape[1] // CMP_STRIDE
        for g in range(NSA_GROUPS):
            acc_scr[g] = acc[:, g * DH:(g + 1) * DH]
            for l in range(CMP_STRIDE):
                rows_l = acc_scr[g, pl.ds(l, n_chunks, stride=CMP_STRIDE), :]
                dst_ref[g, :, l * DH:(l + 1) * DH] = rows_l.astype(dst_ref.dtype)

    @pl.when(j == KC_TILE)
    def _():
        chunked(kcx_ref)

    @pl.when(j == VC_TILE)
    def _():
        chunked(vcx_ref)


def _inproj(x2, norm_w, w_att, w_small, cosf, sinf):
    seq = x2.shape[0]
    tm = min(seq, 1024)
    tn = INPROJ_TN
    chunk_w = CMP_STRIDE * DH
    chunk_spec = pl.BlockSpec((NSA_GROUPS, tm // CMP_STRIDE, chunk_w), lambda i, j: (0, i, 0))
    chunk_shape = jax.ShapeDtypeStruct((NSA_GROUPS, seq // CMP_STRIDE, chunk_w), BF16)
    return pl.pallas_call(
        _inproj_kernel,
        grid=(seq // tm, ATT_W // tn),
        in_specs=[pl.BlockSpec((tm, D_MODEL), lambda i, j: (i, 0)),
                  pl.BlockSpec((1, D_MODEL), lambda i, j: (0, 0)),
                  pl.BlockSpec((D_MODEL, tn), lambda i, j: (0, j)),
                  pl.BlockSpec((D_MODEL, SMALL_W), lambda i, j: (0, 0)),
                  pl.BlockSpec((tm, DH), lambda i, j: (i, 0)),
                  pl.BlockSpec((tm, DH), lambda i, j: (i, 0))],
        out_specs=[pl.BlockSpec((tm, tn), lambda i, j: (i, j)),
                   pl.BlockSpec((tm, SMALL_W), lambda i, j: (i, 0)),
                   chunk_spec, chunk_spec,
                   pl.BlockSpec((tm, D_MODEL), lambda i, j: (i, 0))],
        out_shape=[jax.ShapeDtypeStruct((seq, ATT_W), BF16),
                   jax.ShapeDtypeStruct((seq, SMALL_W), F32),
                   chunk_shape, chunk_shape,
                   jax.ShapeDtypeStruct((seq, D_MODEL), BF16)],
        scratch_shapes=[pltpu.VMEM((NSA_GROUPS, tm, DH), F32)],
        compiler_params=_params(("arbitrary", "arbitrary")),
        name="inproj_attn",
    )(x2, norm_w.reshape(1, D_MODEL), w_att, w_small, cosf, sinf)


def _plain_matmul_kernel(a_ref, w_ref, o_ref):
    o_ref[...] = _dot(a_ref[...], w_ref[...]).astype(o_ref.dtype)


def _proj_plain(u, w):
    seq, k = u.shape
    n = w.shape[1]
    tm = min(seq, 2048)
    tn = 1024
    return pl.pallas_call(
        _plain_matmul_kernel,
        grid=(seq // tm, n // tn),
        in_specs=[pl.BlockSpec((tm, k), lambda i, j: (i, 0)),
                  pl.BlockSpec((k, tn), lambda i, j: (0, j))],
        out_specs=pl.BlockSpec((tm, tn), lambda i, j: (i, j)),
        out_shape=jax.ShapeDtypeStruct((seq, n), BF16),
        compiler_params=_params(("arbitrary", "arbitrary")),
        name="inproj_ssm",
    )(u, w)


def _compress_kernel(xk_ref, xv_ref, pk_ref, pv_ref, w1k_ref, w2k_ref, w1v_ref, w2v_ref,
                     cos_ref, sin_ref, kc_ref, vc_ref):
    half = CMP_STRIDE * DH
    ncp = xk_ref.shape[1]

    def comp(x_ref, p_ref, w1_ref, w2_ref):
        x = x_ref[0]
        wt = w1_ref[0:half, :]
        wb = w1_ref[half:2 * half, :]
        a = _dot(x, wt)
        b = _dot(x, wb)
        pb = _dot(p_ref[...], wt)[0:1] + _dot(p_ref[...], wb)[1:2]
        h = a + pltpu.roll(b, ncp - 1, 0) + pb
        h = h * jax.nn.sigmoid(h)
        return _dot(h.astype(BF16), w2_ref[...])

    kc = comp(xk_ref, pk_ref, w1k_ref, w2k_ref)
    kc_ref[0] = _rope(kc, cos_ref[...], sin_ref[...]).astype(BF16)
    vc_ref[0] = comp(xv_ref, pv_ref, w1v_ref, w2v_ref).astype(BF16)


def _compress(xk, xv, pk, pv, w1k, w2k, w1v, w2v, cos_c, sin_c):
    g, ncp, width = xk.shape
    full = lambda shape: pl.BlockSpec(shape, lambda i: (0,) * len(shape))
    per_g = pl.BlockSpec((1, ncp, width), lambda i: (i, 0, 0))
    out_g = pl.BlockSpec((1, ncp, DH), lambda i: (i, 0, 0))
    return pl.pallas_call(
        _compress_kernel,
        grid=(g,),
        in_specs=[per_g, per_g, full(pk.shape), full(pv.shape), full(w1k.shape), full(w2k.shape),
                  full(w1v.shape), full(w2v.shape), full(cos_c.shape), full(sin_c.shape)],
        out_specs=[out_g, out_g],
        out_shape=[jax.ShapeDtypeStruct((g, ncp, DH), BF16)] * 2,
        compiler_params=_params(("arbitrary",)),
        name="compress",
    )(xk, xv, pk, pv, w1k, w2k, w1v, w2v, cos_c, sin_c)


CMP_TQ = 128


def _cmp_attn_kernel(q_ref, kc_ref, vc_ref, mt_ref, oc_ref, mask_ref):
    i = pl.program_id(0)
    qbase = i * CMP_TQ
    ncp = kc_ref.shape[1]
    nsel = mt_ref.shape[0]
    row = lax.broadcasted_iota(jnp.int32, (CMP_TQ, ncp), 0)
    col = lax.broadcasted_iota(jnp.int32, (CMP_TQ, ncp), 1)
    bias1 = jnp.where((col * CMP_STRIDE + (CMP_BLOCK - 1)) <= qbase + row, 0.0, MASK_NEG)
    bias = jnp.concatenate([bias1] * NSA_REP, axis=0)
    has_key = jnp.concatenate([bias1[:, 0:1] == 0.0] * NSA_REP, axis=0)
    blk = lax.broadcasted_iota(jnp.int32, (nsel, CMP_TQ), 0)
    cur = (qbase + lax.broadcasted_iota(jnp.int32, (nsel, CMP_TQ), 1)) >> 6
    future = blk > cur
    forced = (blk == 0) | (blk == cur) | (blk == cur - 1)
    blk_f = blk.astype(F32)
    mt = mt_ref[...]
    importance = []
    for g in range(NSA_GROUPS):
        qg = jnp.concatenate(
            [q_ref[:, (g * NSA_REP + r) * DH:(g * NSA_REP + r + 1) * DH] for r in range(NSA_REP)], axis=0)
        s = _dot_nt(qg, kc_ref[g]) + bias
        p = jnp.exp2(s - jnp.max(s, axis=-1, keepdims=True))
        p = p * jnp.where(has_key, 1.0 / jnp.sum(p, axis=-1, keepdims=True), 0.0)
        o = _dot(p.astype(BF16), vc_ref[g])
        for r in range(NSA_REP):
            h = g * NSA_REP + r
            oc_ref[:, h * DH:(h + 1) * DH] = o[r * CMP_TQ:(r + 1) * CMP_TQ].astype(oc_ref.dtype)
        ps = p[0:CMP_TQ]
        for r in range(1, NSA_REP):
            ps = ps + p[r * CMP_TQ:(r + 1) * CMP_TQ]
        ps_hi, ps_lo = _split2(ps)
        importance.append(_dot_nt(mt, ps_hi) + _dot_nt(mt, ps_lo))
    v = jnp.where(future | forced, -1.0, jnp.stack(importance))
    for _ in range(SEL_TOPK - N_FORCED):
        mx = jnp.max(v, axis=1, keepdims=True)
        first = jnp.min(jnp.where(v == mx, blk_f, float(nsel)), axis=1, keepdims=True)
        v = jnp.where(blk_f == first, TAKEN, v)
    bias_t = jnp.where(forced | (v == TAKEN), 0.0, MASK_NEG)
    for g in range(NSA_GROUPS):
        mask_ref[g] = bias_t[g].T.astype(mask_ref.dtype)


def _cmp_attn(main, kc, vc, mt, seq):
    nsel = mt.shape[0]
    ncp = kc.shape[1]
    return pl.pallas_call(
        _cmp_attn_kernel,
        grid=(seq // CMP_TQ,),
        in_specs=[pl.BlockSpec((CMP_TQ, NSA_HEADS * DH), lambda i: (i, COL_Q // (NSA_HEADS * DH))),
                  pl.BlockSpec((NSA_GROUPS, ncp, DH), lambda i: (0, 0, 0)),
                  pl.BlockSpec((NSA_GROUPS, ncp, DH), lambda i: (0, 0, 0)),
                  pl.BlockSpec((nsel, ncp), lambda i: (0, 0))],
        out_specs=[pl.BlockSpec((CMP_TQ, NSA_HEADS * DH), lambda i: (i, 0)),
                   pl.BlockSpec((NSA_GROUPS, CMP_TQ, nsel), lambda i: (0, i, 0))],
        out_shape=[jax.ShapeDtypeStruct((seq, NSA_HEADS * DH), BF16),
                   jax.ShapeDtypeStruct((NSA_GROUPS, seq, nsel), BF16)],
        compiler_params=_params(("arbitrary",)),
        name="cmp_attn_topk",
    )(main, kc, vc, mt)


ATT_TQ = 512
ATT_TK = 512
WIN_SPAN = WINDOW + ATT_TQ


def _nsa_attn_kernel(q_ref, mask_ref, ks_ref, vs_ref, kw_ref, vw_ref, oc_ref, gate_ref, o_ref,
                     qa_scr, acc_scr, m_scr):
    i = pl.program_id(1)
    qbase = i * ATT_TQ
    assert mask_ref.shape[2] == LANES, "the block-selection bias rides in one extra 128-wide contraction slab"
    bias = mask_ref[0]
    for r in range(NSA_REP):
        qa_scr[r] = jnp.concatenate([q_ref[:, r * DH:(r + 1) * DH], bias], axis=1)
    m_scr[...] = jnp.full(m_scr.shape, NEG_INF, F32)
    acc_scr[...] = jnp.zeros(acc_scr.shape, F32)
    krow = lax.broadcasted_iota(jnp.int32, (ATT_TK, LANES), 0)
    kblk = lax.broadcasted_iota(jnp.int32, (ATT_TK, LANES), 1)

    def step(j, diagonal):
        k0 = pl.multiple_of(j * ATT_TK, ATT_TK)
        onehot = jnp.where(((krow + k0) >> 6) == kblk, 1.0, 0.0).astype(BF16)
        k_aug = jnp.concatenate([ks_ref[pl.ds(k0, ATT_TK), :], onehot], axis=1)
        v_aug = jnp.concatenate([vs_ref[pl.ds(k0, ATT_TK), :], jnp.ones((ATT_TK, LANES), BF16)], axis=1)
        if diagonal:
            r_ = lax.broadcasted_iota(jnp.int32, (ATT_TQ, ATT_TK), 0)
            c_ = lax.broadcasted_iota(jnp.int32, (ATT_TQ, ATT_TK), 1)
            causal_bias = jnp.where(c_ <= r_, 0.0, MASK_NEG)
        for r in range(NSA_REP):
            s = _dot_nt(qa_scr[r], k_aug)
            if diagonal:
                s = s + causal_bias
            m_prev = m_scr[r]
            m_new = jnp.maximum(m_prev, jnp.max(s, axis=-1, keepdims=True))
            alpha = jnp.exp2(m_prev - m_new)
            p = jnp.exp2(s - jnp.concatenate([m_new] * (ATT_TK // LANES), axis=1))
            acc_scr[r] = acc_scr[r] * jnp.concatenate([alpha, alpha], axis=1) + _dot(p.astype(BF16), v_aug)
            m_scr[r] = m_new

    def body(jj, carry):
        step(2 * jj, False)
        step(2 * jj + 1, False)
        return carry

    lax.fori_loop(0, i // 2, body, 0)

    @pl.when(i % 2 == 1)
    def _():
        step(i - 1, False)

    step(i, True)

    start = pl.multiple_of(jnp.maximum(qbase - WINDOW, 0), ATT_TQ)
    kw = kw_ref[pl.ds(start, WIN_SPAN), :]
    vw_aug = jnp.concatenate([vw_ref[pl.ds(start, WIN_SPAN), :], jnp.ones((WIN_SPAN, LANES), BF16)], axis=1)
    r_ = lax.broadcasted_iota(jnp.int32, (ATT_TQ, WIN_SPAN), 0)
    c_ = lax.broadcasted_iota(jnp.int32, (ATT_TQ, WIN_SPAN), 1)
    diff = (qbase + r_) - (start + c_)
    win_bias = jnp.where((diff >= 0) & (diff < WINDOW), 0.0, MASK_NEG)
    gates = jax.nn.sigmoid(gate_ref[...])
    for r in range(NSA_REP):
        s = _dot_nt(q_ref[:, r * DH:(r + 1) * DH], kw) + win_bias
        p = jnp.exp2(s - jnp.max(s, axis=-1, keepdims=True))
        ow = _dot(p.astype(BF16), vw_aug)
        o_win = ow[:, :DH] / jnp.maximum(ow[:, DH:], TINY)
        acc = acc_scr[r]
        o_sel = acc[:, :DH] / jnp.maximum(acc[:, DH:], TINY)
        o = (gates[:, 3 * r:3 * r + 1] * oc_ref[:, r * DH:(r + 1) * DH].astype(F32)
             + gates[:, 3 * r + 1:3 * r + 2] * o_sel
             + gates[:, 3 * r + 2:3 * r + 3] * o_win)
        o_ref[:, r * DH:(r + 1) * DH] = o.astype(o_ref.dtype)


def _nsa_attn(main, small, mask, o_cmp, seq):
    gw = NSA_REP * DH
    nsel = mask.shape[2]
    kv_spec = lambda col: pl.BlockSpec((seq, DH), lambda g, i: (0, col // DH + g))
    return pl.pallas_call(
        _nsa_attn_kernel,
        grid=(NSA_GROUPS, seq // ATT_TQ),
        in_specs=[pl.BlockSpec((ATT_TQ, gw), lambda g, i: (i, COL_Q // gw + g)),
                  pl.BlockSpec((1, ATT_TQ, nsel), lambda g, i: (g, i, 0)),
                  kv_spec(COL_KS), kv_spec(COL_VS), kv_spec(COL_KW), kv_spec(COL_VW),
                  pl.BlockSpec((ATT_TQ, gw), lambda g, i: (i, g)),
                  pl.BlockSpec((ATT_TQ, LANES), lambda g, i: (i, 1 + g))],
        out_specs=pl.BlockSpec((ATT_TQ, gw), lambda g, i: (i, g)),
        out_shape=jax.ShapeDtypeStruct((seq, NSA_HEADS * DH), BF16),
        scratch_shapes=[pltpu.VMEM((NSA_REP, ATT_TQ, 2 * DH), BF16),
                        pltpu.VMEM((NSA_REP, ATT_TQ, 2 * DH), F32),
                        pltpu.VMEM((NSA_REP, ATT_TQ, LANES), F32)],
        compiler_params=_params(("arbitrary", "arbitrary")),
        name="nsa_sel_win_attn",
    )(main, mask, main, main, main, main, o_cmp, small)


HALO = 8
GROUP_W = SSM_D_INNER // SSM_GROUPS
HEADS_PER_GROUP = SSM_HEADS // SSM_GROUPS


def _silu(x):
    h = 0.5 * x
    return h + h * jnp.tanh(h)


def _mamba_kernel(z_ref, xs_ref, b_ref, c_ref, dt_ref,
                  cwx_ref, cwb_ref, cwc_ref, cbx_ref, cbb_ref, cbc_ref,
                  dtb_ref, alog_ref, dfull_ref, nw_ref, expand2_ref, tril3_ref,
                  o_ref, state_scr, xs_ext, b_ext, c_ext):
    c = pl.program_id(0)
    L = SSD_CHUNK

    @pl.when(c == 0)
    def _():
        state_scr[...] = jnp.zeros(state_scr.shape, F32)
        for carry in (xs_ext, b_ext, c_ext):
            carry[...] = jnp.zeros(carry.shape, F32)

    def conv_silu(x_ref, carry, w_ref, bias_ref):
        x = x_ref[...].astype(F32)
        first_row = lax.broadcasted_iota(jnp.int32, (HALO, x.shape[1]), 0) == 0
        r = None
        for k in range(CONV_WIDTH - 1):
            t = w_ref[k:k + 1, :] * x
            if r is not None:
                t = t + r
            rolled = pltpu.roll(t, 1, 0)
            top = jnp.where(first_row, carry[k:k + 1, :], rolled[0:HALO])
            carry[k:k + 1, :] = t[L - 1:L, :]
            r = jnp.concatenate([top, rolled[HALO:]], axis=0)
        return _silu(r + w_ref[CONV_WIDTH - 1:CONV_WIDTH, :] * x + bias_ref[...])

    xs = conv_silu(xs_ref, xs_ext, cwx_ref, cbx_ref)
    bm = conv_silu(b_ref, b_ext, cwb_ref, cbb_ref).astype(BF16)
    cm = conv_silu(c_ref, c_ext, cwc_ref, cbc_ref).astype(BF16)

    xdt_raw = dt_ref[...] + dtb_ref[...]
    dt = jnp.maximum(xdt_raw, 0.0) + jnp.log1p(jnp.exp(-jnp.abs(xdt_raw)))
    adt = dt * (-jnp.exp(alog_ref[...]))
    acs = _dot(tril3_ref[...], jnp.concatenate(_split3(adt), axis=0)) * LOG2E
    acs_t = acs.T
    a_last = acs[L - 1:L, :]
    stacked = jnp.concatenate([dt, jnp.exp2(acs), jnp.exp2(a_last - acs)], axis=0)
    wide = _dot(jnp.concatenate(_split2(stacked), axis=1), expand2_ref[...])
    dt_w = wide[0:L]
    in_decay_w = wide[L:2 * L]
    out_decay_w = wide[2 * L:3 * L]

    xdt = xs * dt_w
    xst = xdt * out_decay_w
    causal = lax.broadcasted_iota(jnp.int32, (L, L), 0) >= lax.broadcasted_iota(jnp.int32, (L, L), 1)
    causal_bias = jnp.where(causal, 0.0, MASK_NEG)
    lane = lax.broadcasted_iota(jnp.int32, (L, LANES), 1)
    low_half = lane < SSM_HEAD_DIM
    chunk_decay = jnp.exp2(acs_t[:, L - 1:L])

    for g in range(SSM_GROUPS):
        gs = slice(g * GROUP_W, (g + 1) * GROUP_W)
        bg = bm[:, g * SSM_STATE:(g + 1) * SSM_STATE]
        cg = cm[:, g * SSM_STATE:(g + 1) * SSM_STATE]
        cb = _dot_nt(cg, bg)
        prev = state_scr[gs, :]
        y_off = _dot_nt(cg, prev.astype(BF16)) * in_decay_w[:, gs]
        pieces = []
        for hp in range(HEADS_PER_GROUP // 2):
            lhs = []
            for e in range(2):
                h = g * HEADS_PER_GROUP + 2 * hp + e
                decay = jnp.exp2(acs[:, h:h + 1] - acs_t[h:h + 1, :] + causal_bias)
                lhs.append((cb * decay).astype(BF16))
            slab = xdt[:, g * GROUP_W + hp * LANES:g * GROUP_W + (hp + 1) * LANES]
            rhs = jnp.concatenate([jnp.where(low_half, slab, 0.0), jnp.where(low_half, 0.0, slab)], axis=0)
            pieces.append(_dot(jnp.concatenate(lhs, axis=1), rhs.astype(BF16)))
        y_g = jnp.concatenate(pieces, axis=1) + y_off
        y_g = y_g + dfull_ref[:, gs] * xs[:, gs]
        y_g = y_g * _silu(z_ref[:, gs].astype(F32))
        ms = jnp.mean(y_g * y_g, axis=-1, keepdims=True)
        o_ref[:, gs] = (y_g * lax.rsqrt(ms + EPS) * nw_ref[:, gs]).astype(o_ref.dtype)
        new = _dot(xst[:, gs].T.astype(BF16), bg)
        cd = jnp.broadcast_to(chunk_decay[g * HEADS_PER_GROUP:(g + 1) * HEADS_PER_GROUP], (HEADS_PER_GROUP, LANES))
        cd = jnp.broadcast_to(cd[:, None, :], (HEADS_PER_GROUP, SSM_HEAD_DIM, LANES)).reshape(GROUP_W, LANES)
        state_scr[gs, :] = prev * cd + new


def _mamba(main, small, cwx, cwb, cwc, cbx, cbb, cbc, dtb, alog, dfull, nw, expand2, tril3, seq):
    L = SSD_CHUNK
    bw = SSM_GROUPS * SSM_STATE
    full = lambda a: pl.BlockSpec(a.shape, lambda c: (0,) * a.ndim)
    consts = [cwx, cwb, cwc, cbx, cbb, cbc, dtb, alog, dfull, nw, expand2, tril3]
    return pl.pallas_call(
        _mamba_kernel,
        grid=(seq // L,),
        in_specs=[pl.BlockSpec((L, SSM_D_INNER), lambda c: (c, COL_Z // SSM_D_INNER)),
                  pl.BlockSpec((L, SSM_D_INNER), lambda c: (c, COL_XS // SSM_D_INNER)),
                  pl.BlockSpec((L, bw), lambda c: (c, COL_B // bw)),
                  pl.BlockSpec((L, bw), lambda c: (c, COL_C // bw)),
                  pl.BlockSpec((L, LANES), lambda c: (c, 0))] + [full(a) for a in consts],
        out_specs=pl.BlockSpec((L, SSM_D_INNER), lambda c: (c, 0)),
        out_shape=jax.ShapeDtypeStruct((seq, SSM_D_INNER), BF16),
        scratch_shapes=[pltpu.VMEM((SSM_D_INNER, SSM_STATE), F32),
                        pltpu.VMEM((HALO, SSM_D_INNER), F32),
                        pltpu.VMEM((HALO, bw), F32),
                        pltpu.VMEM((HALO, bw), F32)],
        compiler_params=_params(("arbitrary",)),
        name="mamba2_ssd",
    )(main, main, main, main, small, *consts)


def _merge_kernel(a_ref, b_ref, pa_ref, pb_ref, ga_ref, gb_ref, o_ref):
    ya = _dot(a_ref[...], pa_ref[...])
    yb = _dot(b_ref[...], pb_ref[...])
    o = jax.nn.sigmoid(ga_ref[...].astype(F32)) * ya + jax.nn.sigmoid(gb_ref[...].astype(F32)) * yb
    o_ref[...] = o.astype(o_ref.dtype)


def _merge(o_nsa, o_ssm, pa, pb, main, seq):
    tm = min(seq, 512)
    tn = 1024
    return pl.pallas_call(
        _merge_kernel,
        grid=(seq // tm, D_MODEL // tn),
        in_specs=[pl.BlockSpec((tm, NSA_HEADS * DH), lambda i, j: (i, 0)),
                  pl.BlockSpec((tm, SSM_D_INNER), lambda i, j: (i, 0)),
                  pl.BlockSpec((NSA_HEADS * DH, tn), lambda i, j: (0, j)),
                  pl.BlockSpec((SSM_D_INNER, tn), lambda i, j: (0, j)),
                  pl.BlockSpec((tm, tn), lambda i, j: (i, COL_GA // tn + j)),
                  pl.BlockSpec((tm, tn), lambda i, j: (i, COL_GB // tn + j))],
        out_specs=pl.BlockSpec((tm, tn), lambda i, j: (i, j)),
        out_shape=jax.ShapeDtypeStruct((seq, D_MODEL), BF16),
        compiler_params=_params(("arbitrary", "arbitrary")),
        name="gated_merge",
    )(o_nsa, o_ssm, pa, pb, main, main)


def _outproj_kernel(m_ref, w_ref, x_ref, nw_ref, h_ref, hn_ref):
    h = x_ref[...] + _dot(m_ref[...], w_ref[...])
    h_ref[...] = h
    ms = jnp.mean(h * h, axis=-1, keepdims=True)
    hn_ref[...] = (h * lax.rsqrt(ms + EPS) * nw_ref[...]).astype(hn_ref.dtype)


def _outproj(merged, w_out, x2, norm_w, seq):
    tm = min(seq, 256)
    row = pl.BlockSpec((tm, D_MODEL), lambda i: (i, 0))
    return pl.pallas_call(
        _outproj_kernel,
        grid=(seq // tm,),
        in_specs=[row, pl.BlockSpec((D_MODEL, D_MODEL), lambda i: (0, 0)), row,
                  pl.BlockSpec((1, D_MODEL), lambda i: (0, 0))],
        out_specs=[row, row],
        out_shape=[jax.ShapeDtypeStruct((seq, D_MODEL), F32), jax.ShapeDtypeStruct((seq, D_MODEL), BF16)],
        compiler_params=_params(("arbitrary",)),
        name="outproj_residual_norm",
    )(merged, w_out, x2, norm_w.reshape(1, D_MODEL))


def _mlp_kernel(hn_ref, wu_ref, wd_ref, h_ref, nw_ref, o_ref, acc_scr):
    f = pl.program_id(1)

    @pl.when(f == 0)
    def _():
        acc_scr[...] = jnp.zeros(acc_scr.shape, F32)

    up = jnp.maximum(_dot(hn_ref[...], wu_ref[...]), 0.0)
    acc_scr[...] += _dot((up * up).astype(BF16), wd_ref[...])

    @pl.when(f == pl.num_programs(1) - 1)
    def _():
        h = h_ref[...] + acc_scr[...]
        ms = jnp.mean(h * h, axis=-1, keepdims=True)
        o_ref[...] = h * lax.rsqrt(ms + EPS) * nw_ref[...]


def _mlp(hn, w_up, w_down, h1, norm_w, seq):
    tm = min(seq, 512)
    tf = 1024
    row = lambda i, f: (i, 0)
    return pl.pallas_call(
        _mlp_kernel,
        grid=(seq // tm, D_FF // tf),
        in_specs=[pl.BlockSpec((tm, D_MODEL), row),
                  pl.BlockSpec((D_MODEL, tf), lambda i, f: (0, f)),
                  pl.BlockSpec((tf, D_MODEL), lambda i, f: (f, 0)),
                  pl.BlockSpec((tm, D_MODEL), row),
                  pl.BlockSpec((1, D_MODEL), lambda i, f: (0, 0))],
        out_specs=pl.BlockSpec((tm, D_MODEL), row),
        out_shape=jax.ShapeDtypeStruct((seq, D_MODEL), F32),
        scratch_shapes=[pltpu.VMEM((tm, D_MODEL), F32)],
        compiler_params=_params(("arbitrary", "arbitrary")),
        name="mlp_final_norm",
    )(hn, w_up, w_down, h1, norm_w.reshape(1, D_MODEL))


def _cmp_to_sel(seq):
    n_cmp = (seq - CMP_BLOCK) // CMP_STRIDE + 1
    n_sel = seq // SEL_BLOCK
    c_start = np.arange(n_cmp) * CMP_STRIDE
    s_start = np.arange(n_sel) * SEL_BLOCK
    overlap = np.clip(np.minimum(c_start[:, None] + CMP_BLOCK, s_start[None, :] + SEL_BLOCK)
                      - np.maximum(c_start[:, None], s_start[None, :]), 0, None)
    m = np.zeros((seq // CMP_STRIDE, LANES), np.float32)
    m[:n_cmp, :n_sel] = overlap / CMP_STRIDE
    return jnp.asarray(m.T, BF16)


def kernel(x, positions, norm_mix_w, w_in, cmp_pos_k, cmp_pos_v, cmp_k_w1, cmp_k_w2, cmp_v_w1, cmp_v_w2, conv_w, conv_b, dt_bias, a_log, ssm_d, ssm_norm_w, w_proj_nsa, w_proj_ssm, w_out, norm_mlp_w, w_up, w_down, norm_final_w):
    bsz, seq, _ = x.shape
    assert bsz == 1 and seq % 1024 == 0 and SEL_TOPK <= seq // SEL_BLOCK <= LANES
    assert w_in.shape[0] == 1, "one layer"
    x2 = x.reshape(seq, D_MODEL)

    wi = w_in[0]
    o_gate = NSA_HEADS * DH + 6 * NSA_GROUPS * DH
    o_z = o_gate + 3 * NSA_HEADS
    o_dt = o_z + 2 * SSM_D_INNER + 2 * SSM_GROUPS * SSM_STATE
    o_bg = o_dt + SSM_HEADS
    cols = lambda a, n: wi[:, a:a + n]
    w_att = cols(0, o_gate).astype(BF16)
    w_ssm = jnp.concatenate([cols(o_z, o_dt - o_z), cols(o_bg, 2 * D_MODEL)], axis=1).astype(BF16)
    assert w_att.shape[1] == ATT_W and w_ssm.shape[1] == SSM_W
    zpad = lambda n: jnp.zeros((D_MODEL, n), F32)
    per_group = 3 * NSA_REP
    small_parts = [cols(o_dt, SSM_HEADS), zpad(LANES - SSM_HEADS)]
    for g in range(NSA_GROUPS):
        small_parts += [cols(o_gate + g * per_group, per_group), zpad(LANES - per_group)]
    w_small = jnp.concatenate(small_parts, axis=1).astype(BF16)

    cosf, sinf = _rope_tables(positions, seq)
    att, small, kc_chunks, vc_chunks, u = _inproj(x2, norm_mix_w[0], w_att, w_small, cosf, sinf)
    ssm = _proj_plain(u, w_ssm)

    ncp = seq // CMP_STRIDE
    def pos2(p):
        p = p.reshape(2, CMP_STRIDE * DH)
        return jnp.concatenate([p, jnp.zeros((14, CMP_STRIDE * DH), F32)], axis=0).astype(BF16)
    end_rows = slice(CMP_BLOCK - 1, None, CMP_STRIDE)
    pad_row = lambda a: jnp.concatenate([a[end_rows], jnp.zeros((ncp - a[end_rows].shape[0], DH), F32)], axis=0)
    kc, vc = _compress(kc_chunks, vc_chunks, pos2(cmp_pos_k[0]), pos2(cmp_pos_v[0]),
                       cmp_k_w1[0].astype(BF16), cmp_k_w2[0].astype(BF16),
                       cmp_v_w1[0].astype(BF16), cmp_v_w2[0].astype(BF16), pad_row(cosf), pad_row(sinf))

    o_cmp, sel_bias = _cmp_attn(att, kc, vc, _cmp_to_sel(seq), seq)
    o_nsa = _nsa_attn(att, small, sel_bias, o_cmp, seq)

    cw = conv_w[0]
    cb_ = conv_b[0].reshape(1, -1)
    lane_pad = lambda a: jnp.concatenate([a.reshape(1, -1), jnp.zeros((1, LANES - a.shape[-1]), F32)], axis=1)
    head_of = np.arange(SSM_D_INNER) // SSM_HEAD_DIM
    expand = np.arange(LANES)[:, None] == head_of[None, :]
    expand2 = jnp.asarray(np.concatenate([expand, expand], axis=0), BF16)
    tril = np.tril(np.ones((SSD_CHUNK, SSD_CHUNK), np.float32))
    tril3 = jnp.asarray(np.concatenate([tril, tril, tril], axis=1), BF16)
    o_ssm = _mamba(ssm, small, cw[:, :4096], cw[:, 4096:5120], cw[:, 5120:],
                   cb_[:, :4096], cb_[:, 4096:5120], cb_[:, 5120:],
                   lane_pad(dt_bias[0]), lane_pad(a_log[0]),
                   jnp.repeat(ssm_d[0], SSM_HEAD_DIM).reshape(1, -1), ssm_norm_w[0].reshape(1, -1),
                   expand2, tril3, seq)

    merged = _merge(o_nsa, o_ssm, w_proj_nsa[0].astype(BF16), w_proj_ssm[0].astype(BF16), ssm, seq)
    h1, hn = _outproj(merged, w_out[0].astype(BF16), x2, norm_mlp_w[0], seq)
    out = _mlp(hn, w_up[0].astype(BF16), w_down[0].astype(BF16), h1, norm_final_w, seq)
    return out.reshape(bsz, seq, D_MODEL)
```

```python
import functools
import math

import jax
import jax.numpy as jnp
import numpy as np
from jax import lax
from jax.experimental import pallas as pl
from jax.experimental.pallas import tpu as pltpu

F32 = jnp.float32
BF16 = jnp.bfloat16

D_MODEL = 2048
NSA_HEADS = 16
NSA_GROUPS = 4
NSA_REP = NSA_HEADS // NSA_GROUPS
DH = 128
CMP_BLOCK = 32
CMP_STRIDE = 16
SEL_BLOCK = 64
SEL_TOPK = 16
WINDOW = 512
ROPE_THETA = 10000.0
N_FORCED = 3
SSM_D_INNER = 4096
SSM_HEAD_DIM = 64
SSM_HEADS = 64
SSM_GROUPS = 8
SSM_STATE = 128
CONV_WIDTH = 4
SSD_CHUNK = 128
D_FF = 4 * D_MODEL
EPS = 1e-6
NEG_INF = -1e30
TINY = 1e-30
MASK_NEG = -1e9
TAKEN = -2.0
LOG2E = math.log2(math.e)
LANES = 128
VMEM_LIMIT = 56 * 1024 * 1024

COL_Q = 0
COL_KC = 2048
COL_VC = 2560
COL_KS = 3072
COL_VS = 3584
COL_KW = 4096
COL_VW = 4608
ATT_W = 5120
COL_Z = 0
COL_XS = 4096
COL_B = 8192
COL_C = 9216
COL_GA = 10240
COL_GB = 12288
SSM_W = 14336
SMALL_W = 640


def _dot(a, b):
    return jnp.dot(a, b, preferred_element_type=F32)


def _dot_nt(a, b):
    return lax.dot_general(a, b, (((1,), (1,)), ((), ())), preferred_element_type=F32)


def _dot_tn(a, b):
    return lax.dot_general(a, b, (((0,), (0,)), ((), ())), preferred_element_type=F32)


def _params(sem):
    return pltpu.CompilerParams(dimension_semantics=sem, vmem_limit_bytes=VMEM_LIMIT)


def _rope(x, cos, sin_signed):
    return x * cos + pltpu.roll(x, DH // 2, 1) * sin_signed


def _split2(x):
    hi = x.astype(BF16)
    lo = (x - hi.astype(F32)).astype(BF16)
    return hi, lo


def _split3(x):
    hi = x.astype(BF16)
    r = x - hi.astype(F32)
    mid = r.astype(BF16)
    lo = (r - mid.astype(F32)).astype(BF16)
    return hi, mid, lo


def _rope_table_kernel(pos_ref, inv_ref, sign_ref, cos_ref, sin_ref):
    ang = pos_ref[...].astype(F32) * inv_ref[...]
    cos_ref[...] = jnp.cos(ang)
    sin_ref[...] = jnp.sin(ang) * sign_ref[...]


def _rope_tables(positions):
    seq = positions.shape[0]
    half = DH // 2
    inv = ROPE_THETA ** (-jnp.arange(0, DH, 2, dtype=F32) / DH)
    inv_full = jnp.concatenate([inv, inv]).reshape(1, DH)
    sign = jnp.concatenate([-jnp.ones((half,), F32), jnp.ones((half,), F32)]).reshape(1, DH)
    tm = math.gcd(seq, 1024)
    return pl.pallas_call(
        _rope_table_kernel,
        grid=(seq // tm,),
        in_specs=[pl.BlockSpec((tm, 1), lambda i: (i, 0)),
                  pl.BlockSpec((1, DH), lambda i: (0, 0)),
                  pl.BlockSpec((1, DH), lambda i: (0, 0))],
        out_specs=[pl.BlockSpec((tm, DH), lambda i: (i, 0))] * 2,
        out_shape=[jax.ShapeDtypeStruct((seq, DH), F32)] * 2,
        compiler_params=_params(("arbitrary",)),
        name="rope_tables",
    )(positions.reshape(seq, 1), inv_full, sign)


INPROJ_TN = 512
Q_TILE_LO = COL_Q // INPROJ_TN
Q_TILE_HI = COL_KC // INPROJ_TN
KC_TILE = COL_KC // INPROJ_TN
VC_TILE = COL_VC // INPROJ_TN
KS_TILE = COL_KS // INPROJ_TN
KW_TILE = COL_KW // INPROJ_TN


def _inproj_kernel(x_ref, nw_ref, w_ref, ws_ref, cos_ref, sin_ref, o_ref, small_ref, kcx_ref, vcx_ref,
                   u_ref, acc_scr):
    j = pl.program_id(1)

    @pl.when(j == 0)
    def _():
        xf = x_ref[...]
        ms = jnp.mean(xf * xf, axis=-1, keepdims=True)
        u = (xf * lax.rsqrt(ms + EPS) * nw_ref[...]).astype(BF16)
        u_ref[...] = u
        small_ref[...] = _dot(u, ws_ref[...])

    acc = _dot(u_ref[...], w_ref[...].astype(BF16))
    is_q = jnp.logical_and(j >= Q_TILE_LO, j < Q_TILE_HI)
    is_rope = is_q | (j == KS_TILE) | (j == KW_TILE)

    @pl.when(jnp.logical_not(is_rope))
    def _():
        o_ref[...] = acc.astype(o_ref.dtype)

    @pl.when(is_rope)
    def _():
        cos = cos_ref[...]
        sin = sin_ref[...]
        scale = jnp.where(is_q, DH ** -0.5 * LOG2E, 1.0).astype(F32)
        for h in range(INPROJ_TN // DH):
            xh = acc[:, h * DH:(h + 1) * DH]
            o_ref[:, h * DH:(h + 1) * DH] = (_rope(xh, cos, sin) * scale).astype(o_ref.dtype)

    def chunked(dst_ref):
        n_chunks = acc_scr.shape[1] // CMP_STRIDE
        for g in range(NSA_GROUPS):
            acc_scr[g] = acc[:, g * DH:(g + 1) * DH]
            for l in range(CMP_STRIDE):
                rows_l = acc_scr[g, pl.ds(l, n_chunks, stride=CMP_STRIDE), :]
                dst_ref[g, :, l * DH:(l + 1) * DH] = rows_l.astype(dst_ref.dtype)

    @pl.when(j == KC_TILE)
    def _():
        chunked(kcx_ref)

    @pl.when(j == VC_TILE)
    def _():
        chunked(vcx_ref)


def _inproj(x2, norm_w, w_in, w_small, cosf, sinf):
    seq = x2.shape[0]
    tm = min(seq, 1024)
    tn = INPROJ_TN
    chunk_w = CMP_STRIDE * DH
    chunk_spec = pl.BlockSpec((NSA_GROUPS, tm // CMP_STRIDE, chunk_w), lambda i, j: (0, i, 0))
    chunk_shape = jax.ShapeDtypeStruct((NSA_GROUPS, seq // CMP_STRIDE, chunk_w), BF16)
    return pl.pallas_call(
        _inproj_kernel,
        grid=(seq // tm, ATT_W // tn),
        in_specs=[pl.BlockSpec((tm, D_MODEL), lambda i, j: (i, 0)),
                  pl.BlockSpec((1, D_MODEL), lambda i, j: (0, 0)),
                  pl.BlockSpec((D_MODEL, tn), lambda i, j: (0, j)),
                  pl.BlockSpec((D_MODEL, SMALL_W), lambda i, j: (0, 0)),
                  pl.BlockSpec((tm, DH), lambda i, j: (i, 0)),
                  pl.BlockSpec((tm, DH), lambda i, j: (i, 0))],
        out_specs=[pl.BlockSpec((tm, tn), lambda i, j: (i, j)),
                   pl.BlockSpec((tm, SMALL_W), lambda i, j: (i, 0)),
                   chunk_spec, chunk_spec,
                   pl.BlockSpec((tm, D_MODEL), lambda i, j: (i, 0))],
        out_shape=[jax.ShapeDtypeStruct((seq, ATT_W), BF16),
                   jax.ShapeDtypeStruct((seq, SMALL_W), F32),
                   chunk_shape, chunk_shape,
                   jax.ShapeDtypeStruct((seq, D_MODEL), BF16)],
        scratch_shapes=[pltpu.VMEM((NSA_GROUPS, tm, DH), F32)],
        compiler_params=_params(("arbitrary", "arbitrary")),
        name="inproj_attn",
    )(x2, norm_w.reshape(1, D_MODEL), w_in, w_small, cosf, sinf)


REPACK_TN = 1024
REPACK_BLOCKS = REPACK_TN // LANES
SRC_Z = NSA_HEADS * DH + 6 * NSA_GROUPS * DH + 3 * NSA_HEADS
SRC_BG = SRC_Z + 2 * SSM_D_INNER + 2 * SSM_GROUPS * SSM_STATE + SSM_HEADS
REPACK_SPLIT = COL_GA // REPACK_TN


def _repack_kernel(*refs):
    in_refs, o_ref = refs[:-1], refs[-1]
    j = pl.program_id(0)

    def shifted(off):
        keep = LANES - off
        rolled = [pltpu.roll(r[...], keep, 1) for r in in_refs]
        from_first = lax.broadcasted_iota(jnp.int32, rolled[0].shape, 1) < keep
        for m in range(REPACK_BLOCKS):
            o_ref[:, m * LANES:(m + 1) * LANES] = jnp.where(from_first, rolled[m], rolled[m + 1]).astype(o_ref.dtype)

    @pl.when(j < REPACK_SPLIT)
    def _():
        shifted(SRC_Z % LANES)

    @pl.when(j >= REPACK_SPLIT)
    def _():
        shifted(SRC_BG % LANES)


def _repack_ssm_weights(w_in):
    k = w_in.shape[0]

    def src_block(kk):
        def index(j):
            base = jnp.where(j < REPACK_SPLIT, SRC_Z // LANES + REPACK_BLOCKS * j,
                             SRC_BG // LANES + REPACK_BLOCKS * (j - REPACK_SPLIT))
            return (0, base + kk)
        return pl.BlockSpec((k, LANES), index)

    n_in = REPACK_BLOCKS + 1
    return pl.pallas_call(
        _repack_kernel,
        grid=(SSM_W // REPACK_TN,),
        in_specs=[src_block(kk) for kk in range(n_in)],
        out_specs=pl.BlockSpec((k, REPACK_TN), lambda j: (0, j)),
        out_shape=jax.ShapeDtypeStruct((k, SSM_W), BF16),
        compiler_params=_params(("arbitrary",)),
        name="repack_ssm_weights",
    )(*([w_in] * n_in))


def _plain_matmul_kernel(a_ref, w_ref, o_ref):
    o_ref[...] = _dot(a_ref[...], w_ref[...]).astype(o_ref.dtype)


def _proj_plain(u, w):
    seq, k = u.shape
    n = w.shape[1]
    tm = min(seq, 2048)
    tn = 1024
    return pl.pallas_call(
        _plain_matmul_kernel,
        grid=(seq // tm, n // tn),
        in_specs=[pl.BlockSpec((tm, k), lambda i, j: (i, 0)),
                  pl.BlockSpec((k, tn), lambda i, j: (0, j))],
        out_specs=pl.BlockSpec((tm, tn), lambda i, j: (i, j)),
        out_shape=jax.ShapeDtypeStruct((seq, n), BF16),
        compiler_params=_params(("arbitrary", "arbitrary")),
        name="inproj_ssm",
    )(u, w)


def _compress_kernel(xk_ref, xv_ref, pk_ref, pv_ref, w1k_ref, w2k_ref, w1v_ref, w2v_ref,
                     cos_ref, sin_ref, kc_ref, vc_ref):
    half = CMP_STRIDE * DH
    ncp = xk_ref.shape[1]

    def comp(x_ref, p_ref, w1_ref, w2_ref):
        x = x_ref[0]
        wt = w1_ref[0:half, :]
        wb = w1_ref[half:2 * half, :]
        a = _dot(x, wt)
        b = _dot(x, wb)
        pb = _dot(p_ref[...], wt)[0:1] + _dot(p_ref[...], wb)[1:2]
        h = a + pltpu.roll(b, ncp - 1, 0) + pb
        h = h * jax.nn.sigmoid(h)
        return _dot(h.astype(BF16), w2_ref[...])

    kc = comp(xk_ref, pk_ref, w1k_ref, w2k_ref)
    kc_ref[0] = _rope(kc, cos_ref[...], sin_ref[...]).astype(BF16)
    vc_ref[0] = comp(xv_ref, pv_ref, w1v_ref, w2v_ref).astype(BF16)


def _compress(xk, xv, pk, pv, w1k, w2k, w1v, w2v, cos_c, sin_c):
    g, ncp, width = xk.shape
    full = lambda shape: pl.BlockSpec(shape, lambda i: (0,) * len(shape))
    per_g = pl.BlockSpec((1, ncp, width), lambda i: (i, 0, 0))
    out_g = pl.BlockSpec((1, ncp, DH), lambda i: (i, 0, 0))
    return pl.pallas_call(
        _compress_kernel,
        grid=(g,),
        in_specs=[per_g, per_g, full(pk.shape), full(pv.shape), full(w1k.shape), full(w2k.shape),
                  full(w1v.shape), full(w2v.shape), full(cos_c.shape), full(sin_c.shape)],
        out_specs=[out_g, out_g],
        out_shape=[jax.ShapeDtypeStruct((g, ncp, DH), BF16)] * 2,
        compiler_params=_params(("arbitrary",)),
        name="compress",
    )(xk, xv, pk, pv, w1k, w2k, w1v, w2v, cos_c, sin_c)


CMP_TQ = 128


def _cmp_attn_kernel(q_ref, kc_ref, vc_ref, mt_ref, oc_ref, mask_ref):
    i = pl.program_id(0)
    qbase = i * CMP_TQ
    ncp = kc_ref.shape[1]
    nsel = mt_ref.shape[0]
    row = lax.broadcasted_iota(jnp.int32, (CMP_TQ, ncp), 0)
    col = lax.broadcasted_iota(jnp.int32, (CMP_TQ, ncp), 1)
    bias1 = jnp.where((col * CMP_STRIDE + (CMP_BLOCK - 1)) <= qbase + row, 0.0, MASK_NEG)
    bias = jnp.concatenate([bias1] * NSA_REP, axis=0)
    has_key = jnp.concatenate([bias1[:, 0:1] == 0.0] * NSA_REP, axis=0)
    blk = lax.broadcasted_iota(jnp.int32, (nsel, CMP_TQ), 0)
    cur = (qbase + lax.broadcasted_iota(jnp.int32, (nsel, CMP_TQ), 1)) >> 6
    future = blk > cur
    forced = (blk == 0) | (blk == cur) | (blk == cur - 1)
    blk_f = blk.astype(F32)
    mt = mt_ref[...]
    importance = []
    for g in range(NSA_GROUPS):
        qg = jnp.concatenate(
            [q_ref[:, (g * NSA_REP + r) * DH:(g * NSA_REP + r + 1) * DH] for r in range(NSA_REP)], axis=0)
        s = _dot_nt(qg, kc_ref[g]) + bias
        p = jnp.exp2(s - jnp.max(s, axis=-1, keepdims=True)).astype(BF16)
        ov = _dot(p, jnp.concatenate([vc_ref[g], jnp.ones((ncp, LANES), BF16)], axis=1))
        inv = jnp.where(has_key, 1.0 / ov[:, DH:], 0.0)
        o = ov[:, :DH] * inv
        imp = None
        for r in range(NSA_REP):
            h = g * NSA_REP + r
            rows_r = slice(r * CMP_TQ, (r + 1) * CMP_TQ)
            oc_ref[:, h * DH:(h + 1) * DH] = o[rows_r].astype(oc_ref.dtype)
            part = _dot_nt(p[rows_r], mt) * inv[rows_r]
            imp = part if imp is None else imp + part
        importance.append(imp.T)
    v = jnp.where(future | forced, -1.0, jnp.stack(importance))
    for _ in range(SEL_TOPK - N_FORCED):
        mx = jnp.max(v, axis=1, keepdims=True)
        first = jnp.min(jnp.where(v == mx, blk_f, float(nsel)), axis=1, keepdims=True)
        v = jnp.where(blk_f == first, TAKEN, v)
    bias_t = jnp.where(forced | (v == TAKEN), 0.0, MASK_NEG)
    for g in range(NSA_GROUPS):
        mask_ref[g] = bias_t[g].T.astype(mask_ref.dtype)


def _cmp_attn(main, kc, vc, mt, seq):
    nsel = mt.shape[0]
    ncp = kc.shape[1]
    return pl.pallas_call(
        _cmp_attn_kernel,
        grid=(seq // CMP_TQ,),
        in_specs=[pl.BlockSpec((CMP_TQ, NSA_HEADS * DH), lambda i: (i, COL_Q // (NSA_HEADS * DH))),
                  pl.BlockSpec((NSA_GROUPS, ncp, DH), lambda i: (0, 0, 0)),
                  pl.BlockSpec((NSA_GROUPS, ncp, DH), lambda i: (0, 0, 0)),
                  pl.BlockSpec((nsel, ncp), lambda i: (0, 0))],
        out_specs=[pl.BlockSpec((CMP_TQ, NSA_HEADS * DH), lambda i: (i, 0)),
                   pl.BlockSpec((NSA_GROUPS, CMP_TQ, nsel), lambda i: (0, i, 0))],
        out_shape=[jax.ShapeDtypeStruct((seq, NSA_HEADS * DH), BF16),
                   jax.ShapeDtypeStruct((NSA_GROUPS, seq, nsel), BF16)],
        compiler_params=_params(("arbitrary",)),
        name="cmp_attn_topk",
    )(main, kc, vc, mt)


ATT_TQ = 512
ATT_TK = 512
WIN_SPAN = WINDOW + ATT_TQ


def _nsa_attn_kernel(q_ref, mask_ref, ks_ref, vs_ref, kw_ref, vw_ref, oc_ref, gate_ref, o_ref,
                     qa_scr, acc_scr, m_scr):
    i = pl.program_id(1)
    qbase = i * ATT_TQ
    assert mask_ref.shape[2] == LANES, "the block-selection bias rides in one extra 128-wide contraction slab"
    bias = mask_ref[0]
    for r in range(NSA_REP):
        qa_scr[r] = jnp.concatenate([q_ref[:, r * DH:(r + 1) * DH], bias], axis=1)
    m_scr[...] = jnp.full(m_scr.shape, NEG_INF, F32)
    acc_scr[...] = jnp.zeros(acc_scr.shape, F32)
    krow = lax.broadcasted_iota(jnp.int32, (ATT_TK, LANES), 0)
    kblk = lax.broadcasted_iota(jnp.int32, (ATT_TK, LANES), 1)

    def step(j, diagonal):
        k0 = pl.multiple_of(j * ATT_TK, ATT_TK)
        onehot = jnp.where(((krow + k0) >> 6) == kblk, 1.0, 0.0).astype(BF16)
        k_aug = jnp.concatenate([ks_ref[pl.ds(k0, ATT_TK), :], onehot], axis=1)
        v_aug = jnp.concatenate([vs_ref[pl.ds(k0, ATT_TK), :], jnp.ones((ATT_TK, LANES), BF16)], axis=1)
        if diagonal:
            r_ = lax.broadcasted_iota(jnp.int32, (ATT_TQ, ATT_TK), 0)
            c_ = lax.broadcasted_iota(jnp.int32, (ATT_TQ, ATT_TK), 1)
            causal_bias = jnp.where(c_ <= r_, 0.0, MASK_NEG)
        for r in range(NSA_REP):
            s = _dot_nt(qa_scr[r], k_aug)
            if diagonal:
                s = s + causal_bias
            m_prev = m_scr[r]
            m_new = jnp.maximum(m_prev, jnp.max(s, axis=-1, keepdims=True))
            alpha = jnp.exp2(m_prev - m_new)
            p = jnp.exp2(s - jnp.concatenate([m_new] * (ATT_TK // LANES), axis=1))
            acc_scr[r] = acc_scr[r] * jnp.concatenate([alpha, alpha], axis=1) + _dot(p.astype(BF16), v_aug)
            m_scr[r] = m_new

    def body(jj, carry):
        for u in range(4):
            step(4 * jj + u, False)
        return carry

    lax.fori_loop(0, i // 4, body, 0)
    done = (i // 4) * 4

    @pl.when((i & 2) != 0)
    def _():
        step(done, False)
        step(done + 1, False)

    @pl.when((i & 1) != 0)
    def _():
        step(i - 1, False)

    step(i, True)

    start = pl.multiple_of(jnp.maximum(qbase - WINDOW, 0), ATT_TQ)
    kw = kw_ref[pl.ds(start, WIN_SPAN), :]
    vw_aug = jnp.concatenate([vw_ref[pl.ds(start, WIN_SPAN), :], jnp.ones((WIN_SPAN, LANES), BF16)], axis=1)
    r_ = lax.broadcasted_iota(jnp.int32, (ATT_TQ, WIN_SPAN), 0)
    c_ = lax.broadcasted_iota(jnp.int32, (ATT_TQ, WIN_SPAN), 1)
    diff = (qbase + r_) - (start + c_)
    win_bias = jnp.where((diff >= 0) & (diff < WINDOW), 0.0, MASK_NEG)
    gates = jax.nn.sigmoid(gate_ref[...])
    for r in range(NSA_REP):
        s = _dot_nt(q_ref[:, r * DH:(r + 1) * DH], kw) + win_bias
        p = jnp.exp2(s - jnp.max(s, axis=-1, keepdims=True))
        ow = _dot(p.astype(BF16), vw_aug)
        o_win = ow[:, :DH] / jnp.maximum(ow[:, DH:], TINY)
        acc = acc_scr[r]
        o_sel = acc[:, :DH] / jnp.maximum(acc[:, DH:], TINY)
        o = (gates[:, 3 * r:3 * r + 1] * oc_ref[:, r * DH:(r + 1) * DH].astype(F32)
             + gates[:, 3 * r + 1:3 * r + 2] * o_sel
             + gates[:, 3 * r + 2:3 * r + 3] * o_win)
        o_ref[:, r * DH:(r + 1) * DH] = o.astype(o_ref.dtype)


def _nsa_attn(main, small, mask, o_cmp, seq):
    gw = NSA_REP * DH
    nsel = mask.shape[2]
    kv_spec = lambda col: pl.BlockSpec((seq, DH), lambda g, i: (0, col // DH + g))
    return pl.pallas_call(
        _nsa_attn_kernel,
        grid=(NSA_GROUPS, seq // ATT_TQ),
        in_specs=[pl.BlockSpec((ATT_TQ, gw), lambda g, i: (i, COL_Q // gw + g)),
                  pl.BlockSpec((1, ATT_TQ, nsel), lambda g, i: (g, i, 0)),
                  kv_spec(COL_KS), kv_spec(COL_VS), kv_spec(COL_KW), kv_spec(COL_VW),
                  pl.BlockSpec((ATT_TQ, gw), lambda g, i: (i, g)),
                  pl.BlockSpec((ATT_TQ, LANES), lambda g, i: (i, 1 + g))],
        out_specs=pl.BlockSpec((ATT_TQ, gw), lambda g, i: (i, g)),
        out_shape=jax.ShapeDtypeStruct((seq, NSA_HEADS * DH), BF16),
        scratch_shapes=[pltpu.VMEM((NSA_REP, ATT_TQ, 2 * DH), BF16),
                        pltpu.VMEM((NSA_REP, ATT_TQ, 2 * DH), F32),
                        pltpu.VMEM((NSA_REP, ATT_TQ, LANES), F32)],
        compiler_params=_params(("arbitrary", "arbitrary")),
        name="nsa_sel_win_attn",
    )(main, mask, main, main, main, main, o_cmp, small)


HALO = 8
GROUP_W = SSM_D_INNER // SSM_GROUPS
HEADS_PER_GROUP = SSM_HEADS // SSM_GROUPS


def _silu(x):
    h = 0.5 * x
    return h + h * jnp.tanh(h)


def _mamba_kernel(z_ref, xs_ref, b_ref, c_ref, dt_ref,
                  cwx_ref, cwb_ref, cwc_ref, cbx_ref, cbb_ref, cbc_ref,
                  dtb_ref, alog_ref, dfull_ref, nw_ref, expand2_ref, tril3_ref,
                  o_ref, state_scr, xs_ext, b_ext, c_ext):
    c = pl.program_id(0)
    L = SSD_CHUNK

    @pl.when(c == 0)
    def _():
        state_scr[...] = jnp.zeros(state_scr.shape, F32)
        for carry in (xs_ext, b_ext, c_ext):
            carry[...] = jnp.zeros(carry.shape, F32)

    def conv_silu(x_ref, carry, w_ref, bias_ref):
        x = x_ref[...].astype(F32)
        first_row = lax.broadcasted_iota(jnp.int32, (HALO, x.shape[1]), 0) == 0
        r = None
        for k in range(CONV_WIDTH - 1):
            t = w_ref[k:k + 1, :] * x
            if r is not None:
                t = t + r
            rolled = pltpu.roll(t, 1, 0)
            top = jnp.where(first_row, carry[k:k + 1, :], rolled[0:HALO])
            carry[k:k + 1, :] = t[L - 1:L, :]
            r = jnp.concatenate([top, rolled[HALO:]], axis=0)
        return _silu(r + w_ref[CONV_WIDTH - 1:CONV_WIDTH, :] * x + bias_ref[...])

    xs = conv_silu(xs_ref, xs_ext, cwx_ref, cbx_ref)
    bm = conv_silu(b_ref, b_ext, cwb_ref, cbb_ref).astype(BF16)
    cm = conv_silu(c_ref, c_ext, cwc_ref, cbc_ref).astype(BF16)

    xdt_raw = dt_ref[...] + dtb_ref[...]
    dt = jnp.maximum(xdt_raw, 0.0) + jnp.log1p(jnp.exp(-jnp.abs(xdt_raw)))
    adt = dt * (-jnp.exp(alog_ref[...]))
    acs = _dot(tril3_ref[...], jnp.concatenate(_split3(adt), axis=0)) * LOG2E
    acs_t = acs.T
    a_last = acs[L - 1:L, :]
    stacked = jnp.concatenate([dt, jnp.exp2(acs), jnp.exp2(a_last - acs)], axis=0)
    wide = _dot(jnp.concatenate(_split2(stacked), axis=1), expand2_ref[...])
    dt_w = wide[0:L]
    in_decay_w = wide[L:2 * L]
    out_decay_w = wide[2 * L:3 * L]

    xdt = xs * dt_w
    xst = xdt * out_decay_w
    causal = lax.broadcasted_iota(jnp.int32, (L, L), 0) >= lax.broadcasted_iota(jnp.int32, (L, L), 1)
    causal_bias = jnp.where(causal, 0.0, MASK_NEG)
    lane = lax.broadcasted_iota(jnp.int32, (L, LANES), 1)
    low_half = lane < SSM_HEAD_DIM
    chunk_decay = jnp.exp2(acs_t[:, L - 1:L])

    for g in range(SSM_GROUPS):
        gs = slice(g * GROUP_W, (g + 1) * GROUP_W)
        bg = bm[:, g * SSM_STATE:(g + 1) * SSM_STATE]
        cg = cm[:, g * SSM_STATE:(g + 1) * SSM_STATE]
        cb = _dot_nt(cg, bg)
        prev = state_scr[gs, :]
        y_off = _dot_nt(cg, prev.astype(BF16)) * in_decay_w[:, gs]
        pieces = []
        for hp in range(HEADS_PER_GROUP // 2):
            lhs = []
            for e in range(2):
                h = g * HEADS_PER_GROUP + 2 * hp + e
                decay = jnp.exp2(acs[:, h:h + 1] - acs_t[h:h + 1, :] + causal_bias)
                lhs.append((cb * decay).astype(BF16))
            slab = xdt[:, g * GROUP_W + hp * LANES:g * GROUP_W + (hp + 1) * LANES]
            rhs = jnp.concatenate([jnp.where(low_half, slab, 0.0), jnp.where(low_half, 0.0, slab)], axis=0)
            pieces.append(_dot(jnp.concatenate(lhs, axis=1), rhs.astype(BF16)))
        y_g = jnp.concatenate(pieces, axis=1) + y_off
        y_g = y_g + dfull_ref[:, gs] * xs[:, gs]
        y_g = y_g * _silu(z_ref[:, gs].astype(F32))
        ms = jnp.mean(y_g * y_g, axis=-1, keepdims=True)
        o_ref[:, gs] = (y_g * lax.rsqrt(ms + EPS) * nw_ref[:, gs]).astype(o_ref.dtype)
        new = _dot(xst[:, gs].T.astype(BF16), bg)
        cd = jnp.broadcast_to(chunk_decay[g * HEADS_PER_GROUP:(g + 1) * HEADS_PER_GROUP], (HEADS_PER_GROUP, LANES))
        cd = jnp.broadcast_to(cd[:, None, :], (HEADS_PER_GROUP, SSM_HEAD_DIM, LANES)).reshape(GROUP_W, LANES)
        state_scr[gs, :] = prev * cd + new


def _mamba(main, small, cwx, cwb, cwc, cbx, cbb, cbc, dtb, alog, dfull, nw, expand2, tril3, seq):
    L = SSD_CHUNK
    bw = SSM_GROUPS * SSM_STATE
    full = lambda a: pl.BlockSpec(a.shape, lambda c: (0,) * a.ndim)
    consts = [cwx, cwb, cwc, cbx, cbb, cbc, dtb, alog, dfull, nw, expand2, tril3]
    return pl.pallas_call(
        _mamba_kernel,
        grid=(seq // L,),
        in_specs=[pl.BlockSpec((L, SSM_D_INNER), lambda c: (c, COL_Z // SSM_D_INNER)),
                  pl.BlockSpec((L, SSM_D_INNER), lambda c: (c, COL_XS // SSM_D_INNER)),
                  pl.BlockSpec((L, bw), lambda c: (c, COL_B // bw)),
                  pl.BlockSpec((L, bw), lambda c: (c, COL_C // bw)),
                  pl.BlockSpec((L, LANES), lambda c: (c, 0))] + [full(a) for a in consts],
        out_specs=pl.BlockSpec((L, SSM_D_INNER), lambda c: (c, 0)),
        out_shape=jax.ShapeDtypeStruct((seq, SSM_D_INNER), BF16),
        scratch_shapes=[pltpu.VMEM((SSM_D_INNER, SSM_STATE), F32),
                        pltpu.VMEM((HALO, SSM_D_INNER), F32),
                        pltpu.VMEM((HALO, bw), F32),
                        pltpu.VMEM((HALO, bw), F32)],
        compiler_params=_params(("arbitrary",)),
        name="mamba2_ssd",
    )(main, main, main, main, small, *consts)


def _merge_kernel(a_ref, b_ref, pa_ref, pb_ref, ga_ref, gb_ref, o_ref):
    ya = _dot(a_ref[...], pa_ref[...])
    yb = _dot(b_ref[...], pb_ref[...])
    o = jax.nn.sigmoid(ga_ref[...].astype(F32)) * ya + jax.nn.sigmoid(gb_ref[...].astype(F32)) * yb
    o_ref[...] = o.astype(o_ref.dtype)


def _merge(o_nsa, o_ssm, pa, pb, main, seq):
    tm = min(seq, 512)
    tn = 1024
    return pl.pallas_call(
        _merge_kernel,
        grid=(seq // tm, D_MODEL // tn),
        in_specs=[pl.BlockSpec((tm, NSA_HEADS * DH), lambda i, j: (i, 0)),
                  pl.BlockSpec((tm, SSM_D_INNER), lambda i, j: (i, 0)),
                  pl.BlockSpec((NSA_HEADS * DH, tn), lambda i, j: (0, j)),
                  pl.BlockSpec((SSM_D_INNER, tn), lambda i, j: (0, j)),
                  pl.BlockSpec((tm, tn), lambda i, j: (i, COL_GA // tn + j)),
                  pl.BlockSpec((tm, tn), lambda i, j: (i, COL_GB // tn + j))],
        out_specs=pl.BlockSpec((tm, tn), lambda i, j: (i, j)),
        out_shape=jax.ShapeDtypeStruct((seq, D_MODEL), BF16),
        compiler_params=_params(("arbitrary", "arbitrary")),
        name="gated_merge",
    )(o_nsa, o_ssm, pa, pb, main, main)


def _outproj_kernel(m_ref, w_ref, x_ref, nw_ref, h_ref, hn_ref):
    h = x_ref[...] + _dot(m_ref[...], w_ref[...])
    h_ref[...] = h
    ms = jnp.mean(h * h, axis=-1, keepdims=True)
    hn_ref[...] = (h * lax.rsqrt(ms + EPS) * nw_ref[...]).astype(hn_ref.dtype)


def _outproj(merged, w_out, x2, norm_w, seq):
    tm = min(seq, 256)
    row = pl.BlockSpec((tm, D_MODEL), lambda i: (i, 0))
    return pl.pallas_call(
        _outproj_kernel,
        grid=(seq // tm,),
        in_specs=[row, pl.BlockSpec((D_MODEL, D_MODEL), lambda i: (0, 0)), row,
                  pl.BlockSpec((1, D_MODEL), lambda i: (0, 0))],
        out_specs=[row, row],
        out_shape=[jax.ShapeDtypeStruct((seq, D_MODEL), F32), jax.ShapeDtypeStruct((seq, D_MODEL), BF16)],
        compiler_params=_params(("arbitrary",)),
        name="outproj_residual_norm",
    )(merged, w_out, x2, norm_w.reshape(1, D_MODEL))


def _mlp_kernel(hn_ref, wu_ref, wd_ref, h_ref, nw_ref, o_ref, acc_scr):
    f = pl.program_id(1)

    @pl.when(f == 0)
    def _():
        acc_scr[...] = jnp.zeros(acc_scr.shape, F32)

    up = jnp.maximum(_dot(hn_ref[...], wu_ref[...]), 0.0)
    acc_scr[...] += _dot((up * up).astype(BF16), wd_ref[...])

    @pl.when(f == pl.num_programs(1) - 1)
    def _():
        h = h_ref[...] + acc_scr[...]
        ms = jnp.mean(h * h, axis=-1, keepdims=True)
        o_ref[...] = h * lax.rsqrt(ms + EPS) * nw_ref[...]


def _mlp(hn, w_up, w_down, h1, norm_w, seq):
    tm = min(seq, 512)
    tf = 1024
    row = lambda i, f: (i, 0)
    return pl.pallas_call(
        _mlp_kernel,
        grid=(seq // tm, D_FF // tf),
        in_specs=[pl.BlockSpec((tm, D_MODEL), row),
                  pl.BlockSpec((D_MODEL, tf), lambda i, f: (0, f)),
                  pl.BlockSpec((tf, D_MODEL), lambda i, f: (f, 0)),
                  pl.BlockSpec((tm, D_MODEL), row),
                  pl.BlockSpec((1, D_MODEL), lambda i, f: (0, 0))],
        out_specs=pl.BlockSpec((tm, D_MODEL), row),
        out_shape=jax.ShapeDtypeStruct((seq, D_MODEL), F32),
        scratch_shapes=[pltpu.VMEM((tm, D_MODEL), F32)],
        compiler_params=_params(("arbitrary", "arbitrary")),
        name="mlp_final_norm",
    )(hn, w_up, w_down, h1, norm_w.reshape(1, D_MODEL))


def _cmp_to_sel(seq):
    n_cmp = (seq - CMP_BLOCK) // CMP_STRIDE + 1
    n_sel = seq // SEL_BLOCK
    c_start = np.arange(n_cmp) * CMP_STRIDE
    s_start = np.arange(n_sel) * SEL_BLOCK
    overlap = np.clip(np.minimum(c_start[:, None] + CMP_BLOCK, s_start[None, :] + SEL_BLOCK)
                      - np.maximum(c_start[:, None], s_start[None, :]), 0, None)
    m = np.zeros((seq // CMP_STRIDE, LANES), np.float32)
    m[:n_cmp, :n_sel] = overlap / CMP_STRIDE
    return jnp.asarray(m.T, BF16)


def kernel(x, positions, norm_mix_w, w_in, cmp_pos_k, cmp_pos_v, cmp_k_w1, cmp_k_w2, cmp_v_w1, cmp_v_w2, conv_w, conv_b, dt_bias, a_log, ssm_d, ssm_norm_w, w_proj_nsa, w_proj_ssm, w_out, norm_mlp_w, w_up, w_down, norm_final_w):
    bsz, seq, _ = x.shape
    assert bsz == 1 and seq % 1024 == 0 and SEL_TOPK <= seq // SEL_BLOCK <= LANES
    assert w_in.shape[0] == 1, "one layer"
    x2 = x.reshape(seq, D_MODEL)

    wi = w_in[0]
    o_gate = NSA_HEADS * DH + 6 * NSA_GROUPS * DH
    o_z = o_gate + 3 * NSA_HEADS
    o_dt = o_z + 2 * SSM_D_INNER + 2 * SSM_GROUPS * SSM_STATE
    o_bg = o_dt + SSM_HEADS
    cols = lambda a, n: wi[:, a:a + n]
    assert o_z == SRC_Z and o_bg == SRC_BG and o_gate == ATT_W
    w_ssm = _repack_ssm_weights(wi)
    zpad = lambda n: jnp.zeros((D_MODEL, n), F32)
    per_group = 3 * NSA_REP
    small_parts = [cols(o_dt, SSM_HEADS), zpad(LANES - SSM_HEADS)]
    for g in range(NSA_GROUPS):
        small_parts += [cols(o_gate + g * per_group, per_group), zpad(LANES - per_group)]
    w_small = jnp.concatenate(small_parts, axis=1).astype(BF16)

    pos = positions.reshape(seq)
    cosf, sinf = _rope_tables(pos)
    att, small, kc_chunks, vc_chunks, u = _inproj(x2, norm_mix_w[0], wi, w_small, cosf, sinf)
    ssm = _proj_plain(u, w_ssm)

    ncp = seq // CMP_STRIDE
    def pos2(p):
        p = p.reshape(2, CMP_STRIDE * DH)
        return jnp.concatenate([p, jnp.zeros((14, CMP_STRIDE * DH), F32)], axis=0).astype(BF16)
    pos_end = pos[CMP_BLOCK - 1::CMP_STRIDE]
    cos_c, sin_c = _rope_tables(jnp.concatenate([pos_end, jnp.zeros((ncp - pos_end.shape[0],), pos.dtype)]))
    kc, vc = _compress(kc_chunks, vc_chunks, pos2(cmp_pos_k[0]), pos2(cmp_pos_v[0]),
                       cmp_k_w1[0].astype(BF16), cmp_k_w2[0].astype(BF16),
                       cmp_v_w1[0].astype(BF16), cmp_v_w2[0].astype(BF16), cos_c, sin_c)

    o_cmp, sel_bias = _cmp_attn(att, kc, vc, _cmp_to_sel(seq), seq)
    o_nsa = _nsa_attn(att, small, sel_bias, o_cmp, seq)

    cw = conv_w[0]
    cb_ = conv_b[0].reshape(1, -1)
    lane_pad = lambda a: jnp.concatenate([a.reshape(1, -1), jnp.zeros((1, LANES - a.shape[-1]), F32)], axis=1)
    head_of = np.arange(SSM_D_INNER) // SSM_HEAD_DIM
    expand = np.arange(LANES)[:, None] == head_of[None, :]
    expand2 = jnp.asarray(np.concatenate([expand, expand], axis=0), BF16)
    tril = np.tril(np.ones((SSD_CHUNK, SSD_CHUNK), np.float32))
    tril3 = jnp.asarray(np.concatenate([tril, tril, tril], axis=1), BF16)
    o_ssm = _mamba(ssm, small, cw[:, :4096], cw[:, 4096:5120], cw[:, 5120:],
                   cb_[:, :4096], cb_[:, 4096:5120], cb_[:, 5120:],
                   lane_pad(dt_bias[0]), lane_pad(a_log[0]),
                   jnp.repeat(ssm_d[0], SSM_HEAD_DIM).reshape(1, -1), ssm_norm_w[0].reshape(1, -1),
                   expand2, tril3, seq)

    merged = _merge(o_nsa, o_ssm, w_proj_nsa[0].astype(BF16), w_proj_ssm[0].astype(BF16), ssm, seq)
    h1, hn = _outproj(merged, w_out[0].astype(BF16), x2, norm_mlp_w[0], seq)
    out = _mlp(hn, w_up[0].astype(BF16), w_down[0].astype(BF16), h1, norm_final_w, seq)
    return out.reshape(bsz, seq, D_MODEL)
```

```python
import functools
import math

import jax
import jax.numpy as jnp
import numpy as np
from jax import lax
from jax.experimental import pallas as pl
from jax.experimental.pallas import tpu as pltpu

F32 = jnp.float32
BF16 = jnp.bfloat16

D_MODEL = 2048
NSA_HEADS = 16
NSA_GROUPS = 4
NSA_REP = NSA_HEADS // NSA_GROUPS
DH = 128
CMP_BLOCK = 32
CMP_STRIDE = 16
SEL_BLOCK = 64
SEL_TOPK = 16
WINDOW = 512
ROPE_THETA = 10000.0
N_FORCED = 3
SSM_D_INNER = 4096
SSM_HEAD_DIM = 64
SSM_HEADS = 64
SSM_GROUPS = 8
SSM_STATE = 128
CONV_WIDTH = 4
SSD_CHUNK = 128
D_FF = 4 * D_MODEL
EPS = 1e-6
NEG_INF = -1e30
TINY = 1e-30
MASK_NEG = -1e9
TAKEN = -2.0
LOG2E = math.log2(math.e)
LANES = 128
VMEM_LIMIT = 56 * 1024 * 1024

COL_Q = 0
COL_KC = 2048
COL_VC = 2560
COL_KS = 3072
COL_VS = 3584
COL_KW = 4096
COL_VW = 4608
ATT_W = 5120
COL_Z = 0
COL_XS = 4096
COL_B = 8192
COL_C = 9216
COL_GA = 10240
COL_GB = 12288
SSM_W = 14336
SMALL_W = 640


def _dot(a, b):
    return jnp.dot(a, b, preferred_element_type=F32)


def _dot_nt(a, b):
    return lax.dot_general(a, b, (((1,), (1,)), ((), ())), preferred_element_type=F32)


def _dot_tn(a, b):
    return lax.dot_general(a, b, (((0,), (0,)), ((), ())), preferred_element_type=F32)


def _params(sem):
    return pltpu.CompilerParams(dimension_semantics=sem, vmem_limit_bytes=VMEM_LIMIT)


def _rope(x, cos, sin_signed):
    return x * cos + pltpu.roll(x, DH // 2, 1) * sin_signed


def _split2(x):
    hi = x.astype(BF16)
    lo = (x - hi.astype(F32)).astype(BF16)
    return hi, lo


def _split3(x):
    hi = x.astype(BF16)
    r = x - hi.astype(F32)
    mid = r.astype(BF16)
    lo = (r - mid.astype(F32)).astype(BF16)
    return hi, mid, lo


def _rope_table_kernel(pos_ref, inv_ref, sign_ref, cos_ref, sin_ref):
    ang = pos_ref[...].astype(F32) * inv_ref[...]
    cos_ref[...] = jnp.cos(ang)
    sin_ref[...] = jnp.sin(ang) * sign_ref[...]


def _rope_tables(positions):
    seq = positions.shape[0]
    half = DH // 2
    inv = ROPE_THETA ** (-jnp.arange(0, DH, 2, dtype=F32) / DH)
    inv_full = jnp.concatenate([inv, inv]).reshape(1, DH)
    sign = jnp.concatenate([-jnp.ones((half,), F32), jnp.ones((half,), F32)]).reshape(1, DH)
    tm = math.gcd(seq, 1024)
    return pl.pallas_call(
        _rope_table_kernel,
        grid=(seq // tm,),
        in_specs=[pl.BlockSpec((tm, 1), lambda i: (i, 0)),
                  pl.BlockSpec((1, DH), lambda i: (0, 0)),
                  pl.BlockSpec((1, DH), lambda i: (0, 0))],
        out_specs=[pl.BlockSpec((tm, DH), lambda i: (i, 0))] * 2,
        out_shape=[jax.ShapeDtypeStruct((seq, DH), F32)] * 2,
        compiler_params=_params(("arbitrary",)),
        name="rope_tables",
    )(positions.reshape(seq, 1), inv_full, sign)


INPROJ_TN = 512
Q_TILE_LO = COL_Q // INPROJ_TN
Q_TILE_HI = COL_KC // INPROJ_TN
KC_TILE = COL_KC // INPROJ_TN
VC_TILE = COL_VC // INPROJ_TN
KS_TILE = COL_KS // INPROJ_TN
KW_TILE = COL_KW // INPROJ_TN


def _inproj_kernel(x_ref, nw_ref, w_ref, cos_ref, sin_ref, o_ref, kcx_ref, vcx_ref, u_ref, acc_scr):
    j = pl.program_id(1)

    @pl.when(j == 0)
    def _():
        xf = x_ref[...]
        ms = jnp.mean(xf * xf, axis=-1, keepdims=True)
        u_ref[...] = (xf * lax.rsqrt(ms + EPS) * nw_ref[...]).astype(BF16)

    acc = _dot(u_ref[...], w_ref[...].astype(BF16))
    is_q = jnp.logical_and(j >= Q_TILE_LO, j < Q_TILE_HI)
    is_rope = is_q | (j == KS_TILE) | (j == KW_TILE)

    @pl.when(jnp.logical_not(is_rope))
    def _():
        o_ref[...] = acc.astype(o_ref.dtype)

    @pl.when(is_rope)
    def _():
        cos = cos_ref[...]
        sin = sin_ref[...]
        scale = jnp.where(is_q, DH ** -0.5 * LOG2E, 1.0).astype(F32)
        for h in range(INPROJ_TN // DH):
            xh = acc[:, h * DH:(h + 1) * DH]
            o_ref[:, h * DH:(h + 1) * DH] = (_rope(xh, cos, sin) * scale).astype(o_ref.dtype)

    def chunked(dst_ref):
        n_chunks = acc_scr.shape[1] // CMP_STRIDE
        for g in range(NSA_GROUPS):
            acc_scr[g] = acc[:, g * DH:(g + 1) * DH]
            for l in range(CMP_STRIDE):
                rows_l = acc_scr[g, pl.ds(l, n_chunks, stride=CMP_STRIDE), :]
                dst_ref[g, :, l * DH:(l + 1) * DH] = rows_l.astype(dst_ref.dtype)

    @pl.when(j == KC_TILE)
    def _():
        chunked(kcx_ref)

    @pl.when(j == VC_TILE)
    def _():
        chunked(vcx_ref)


def _inproj(x2, norm_w, w_in, cosf, sinf):
    seq = x2.shape[0]
    tm = min(seq, 1024)
    tn = INPROJ_TN
    chunk_w = CMP_STRIDE * DH
    chunk_spec = pl.BlockSpec((NSA_GROUPS, tm // CMP_STRIDE, chunk_w), lambda i, j: (0, i, 0))
    chunk_shape = jax.ShapeDtypeStruct((NSA_GROUPS, seq // CMP_STRIDE, chunk_w), BF16)
    return pl.pallas_call(
        _inproj_kernel,
        grid=(seq // tm, ATT_W // tn),
        in_specs=[pl.BlockSpec((tm, D_MODEL), lambda i, j: (i, 0)),
                  pl.BlockSpec((1, D_MODEL), lambda i, j: (0, 0)),
                  pl.BlockSpec((None, D_MODEL, tn), lambda i, j: (0, 0, j)),
                  pl.BlockSpec((tm, DH), lambda i, j: (i, 0)),
                  pl.BlockSpec((tm, DH), lambda i, j: (i, 0))],
        out_specs=[pl.BlockSpec((tm, tn), lambda i, j: (i, j)),
                   chunk_spec, chunk_spec,
                   pl.BlockSpec((tm, D_MODEL), lambda i, j: (i, 0))],
        out_shape=[jax.ShapeDtypeStruct((seq, ATT_W), BF16),
                   chunk_shape, chunk_shape,
                   jax.ShapeDtypeStruct((seq, D_MODEL), BF16)],
        scratch_shapes=[pltpu.VMEM((NSA_GROUPS, tm, DH), F32)],
        compiler_params=_params(("arbitrary", "arbitrary")),
        name="inproj_attn",
    )(x2, norm_w.reshape(1, D_MODEL), w_in, cosf, sinf)


REPACK_TN = 1024
REPACK_BLOCKS = REPACK_TN // LANES
SRC_Z = NSA_HEADS * DH + 6 * NSA_GROUPS * DH + 3 * NSA_HEADS
SRC_BG = SRC_Z + 2 * SSM_D_INNER + 2 * SSM_GROUPS * SSM_STATE + SSM_HEADS
REPACK_SPLIT = COL_GA // REPACK_TN


def _repack_kernel(*refs):
    in_refs, o_ref = refs[:-1], refs[-1]
    j = pl.program_id(0)

    def shifted(off):
        keep = LANES - off
        rolled = [pltpu.roll(r[...], keep, 1) for r in in_refs]
        from_first = lax.broadcasted_iota(jnp.int32, rolled[0].shape, 1) < keep
        for m in range(REPACK_BLOCKS):
            o_ref[:, m * LANES:(m + 1) * LANES] = jnp.where(from_first, rolled[m], rolled[m + 1]).astype(o_ref.dtype)

    @pl.when(j < REPACK_SPLIT)
    def _():
        shifted(SRC_Z % LANES)

    @pl.when(j >= REPACK_SPLIT)
    def _():
        shifted(SRC_BG % LANES)


def _repack_ssm_weights(w_in):
    k = w_in.shape[1]

    def src_block(kk):
        def index(j):
            base = jnp.where(j < REPACK_SPLIT, SRC_Z // LANES + REPACK_BLOCKS * j,
                             SRC_BG // LANES + REPACK_BLOCKS * (j - REPACK_SPLIT))
            return (0, 0, base + kk)
        return pl.BlockSpec((None, k, LANES), index)

    n_in = REPACK_BLOCKS + 1
    return pl.pallas_call(
        _repack_kernel,
        grid=(SSM_W // REPACK_TN,),
        in_specs=[src_block(kk) for kk in range(n_in)],
        out_specs=pl.BlockSpec((k, REPACK_TN), lambda j: (0, j)),
        out_shape=jax.ShapeDtypeStruct((k, SSM_W), BF16),
        compiler_params=_params(("arbitrary",)),
        name="repack_ssm_weights",
    )(*([w_in] * n_in))


def _plain_matmul_kernel(a_ref, w_ref, o_ref):
    o_ref[...] = _dot(a_ref[...], w_ref[...].astype(BF16)).astype(o_ref.dtype)


def _proj_plain(u, w, out_dtype=BF16, tm=2048, tn=1024):
    seq, k = u.shape
    n = w.shape[1]
    tm = min(seq, tm)
    tn = min(n, tn)
    return pl.pallas_call(
        _plain_matmul_kernel,
        grid=(seq // tm, n // tn),
        in_specs=[pl.BlockSpec((tm, k), lambda i, j: (i, 0)),
                  pl.BlockSpec((k, tn), lambda i, j: (0, j))],
        out_specs=pl.BlockSpec((tm, tn), lambda i, j: (i, j)),
        out_shape=jax.ShapeDtypeStruct((seq, n), out_dtype),
        compiler_params=_params(("arbitrary", "arbitrary")),
        name="inproj_plain",
    )(u, w)


def _compress_kernel(xk_ref, xv_ref, pk_ref, pv_ref, w1k_ref, w2k_ref, w1v_ref, w2v_ref,
                     cos_ref, sin_ref, kc_ref, vc_ref):
    half = CMP_STRIDE * DH
    ncp = xk_ref.shape[1]

    def comp(x_ref, p_ref, w1_ref, w2_ref):
        x = x_ref[0]
        wt = w1_ref[0:half, :]
        wb = w1_ref[half:2 * half, :]
        a = _dot(x, wt)
        b = _dot(x, wb)
        pb = _dot(p_ref[...], wt)[0:1] + _dot(p_ref[...], wb)[1:2]
        h = a + pltpu.roll(b, ncp - 1, 0) + pb
        h = h * jax.nn.sigmoid(h)
        return _dot(h.astype(BF16), w2_ref[...])

    kc = comp(xk_ref, pk_ref, w1k_ref, w2k_ref)
    kc_ref[0] = _rope(kc, cos_ref[...], sin_ref[...]).astype(BF16)
    vc_ref[0] = comp(xv_ref, pv_ref, w1v_ref, w2v_ref).astype(BF16)


def _compress(xk, xv, pk, pv, w1k, w2k, w1v, w2v, cos_c, sin_c):
    g, ncp, width = xk.shape
    full = lambda shape: pl.BlockSpec(shape, lambda i: (0,) * len(shape))
    per_g = pl.BlockSpec((1, ncp, width), lambda i: (i, 0, 0))
    out_g = pl.BlockSpec((1, ncp, DH), lambda i: (i, 0, 0))
    return pl.pallas_call(
        _compress_kernel,
        grid=(g,),
        in_specs=[per_g, per_g, full(pk.shape), full(pv.shape), full(w1k.shape), full(w2k.shape),
                  full(w1v.shape), full(w2v.shape), full(cos_c.shape), full(sin_c.shape)],
        out_specs=[out_g, out_g],
        out_shape=[jax.ShapeDtypeStruct((g, ncp, DH), BF16)] * 2,
        compiler_params=_params(("arbitrary",)),
        name="compress",
    )(xk, xv, pk, pv, w1k, w2k, w1v, w2v, cos_c, sin_c)


CMP_TQ = 128


def _cmp_attn_kernel(q_ref, kc_ref, vc_ref, mt_ref, oc_ref, mask_ref):
    i = pl.program_id(0)
    qbase = i * CMP_TQ
    ncp = kc_ref.shape[1]
    nsel = mt_ref.shape[0]
    row = lax.broadcasted_iota(jnp.int32, (CMP_TQ, ncp), 0)
    col = lax.broadcasted_iota(jnp.int32, (CMP_TQ, ncp), 1)
    bias1 = jnp.where((col * CMP_STRIDE + (CMP_BLOCK - 1)) <= qbase + row, 0.0, MASK_NEG)
    bias = jnp.concatenate([bias1] * NSA_REP, axis=0)
    has_key = jnp.concatenate([bias1[:, 0:1] == 0.0] * NSA_REP, axis=0)
    blk = lax.broadcasted_iota(jnp.int32, (nsel, CMP_TQ), 0)
    cur = (qbase + lax.broadcasted_iota(jnp.int32, (nsel, CMP_TQ), 1)) >> 6
    future = blk > cur
    forced = (blk == 0) | (blk == cur) | (blk == cur - 1)
    blk_f = blk.astype(F32)
    mt = mt_ref[...]
    importance = []
    for g in range(NSA_GROUPS):
        qg = jnp.concatenate(
            [q_ref[:, (g * NSA_REP + r) * DH:(g * NSA_REP + r + 1) * DH] for r in range(NSA_REP)], axis=0)
        s = _dot_nt(qg, kc_ref[g]) + bias
        p = jnp.exp2(s - jnp.max(s, axis=-1, keepdims=True)).astype(BF16)
        ov = _dot(p, jnp.concatenate([vc_ref[g], jnp.ones((ncp, LANES), BF16)], axis=1))
        inv = jnp.where(has_key, 1.0 / ov[:, DH:], 0.0)
        o = ov[:, :DH] * inv
        imp = None
        for r in range(NSA_REP):
            h = g * NSA_REP + r
            rows_r = slice(r * CMP_TQ, (r + 1) * CMP_TQ)
            oc_ref[:, h * DH:(h + 1) * DH] = o[rows_r].astype(oc_ref.dtype)
            part = _dot_nt(p[rows_r], mt) * inv[rows_r]
            imp = part if imp is None else imp + part
        importance.append(imp.T)
    v = jnp.where(future | forced, -1.0, jnp.stack(importance))
    for _ in range(SEL_TOPK - N_FORCED):
        mx = jnp.max(v, axis=1, keepdims=True)
        first = jnp.min(jnp.where(v == mx, blk_f, float(nsel)), axis=1, keepdims=True)
        v = jnp.where(blk_f == first, TAKEN, v)
    bias_t = jnp.where(forced | (v == TAKEN), 0.0, MASK_NEG)
    for g in range(NSA_GROUPS):
        mask_ref[g] = bias_t[g].T.astype(mask_ref.dtype)


def _cmp_attn(main, kc, vc, mt, seq):
    nsel = mt.shape[0]
    ncp = kc.shape[1]
    return pl.pallas_call(
        _cmp_attn_kernel,
        grid=(seq // CMP_TQ,),
        in_specs=[pl.BlockSpec((CMP_TQ, NSA_HEADS * DH), lambda i: (i, COL_Q // (NSA_HEADS * DH))),
                  pl.BlockSpec((NSA_GROUPS, ncp, DH), lambda i: (0, 0, 0)),
                  pl.BlockSpec((NSA_GROUPS, ncp, DH), lambda i: (0, 0, 0)),
                  pl.BlockSpec((nsel, ncp), lambda i: (0, 0))],
        out_specs=[pl.BlockSpec((CMP_TQ, NSA_HEADS * DH), lambda i: (i, 0)),
                   pl.BlockSpec((NSA_GROUPS, CMP_TQ, nsel), lambda i: (0, i, 0))],
        out_shape=[jax.ShapeDtypeStruct((seq, NSA_HEADS * DH), BF16),
                   jax.ShapeDtypeStruct((NSA_GROUPS, seq, nsel), BF16)],
        compiler_params=_params(("arbitrary",)),
        name="cmp_attn_topk",
    )(main, kc, vc, mt)


ATT_TQ = 512
ATT_TK = 512
WIN_SPAN = WINDOW + ATT_TQ


def _nsa_attn_kernel(q_ref, mask_ref, ks_ref, vs_ref, kw_ref, vw_ref, oc_ref, gate_ref, o_ref,
                     qa_scr, acc_scr, m_scr):
    i = pl.program_id(1)
    qbase = i * ATT_TQ
    assert mask_ref.shape[2] == LANES, "the block-selection bias rides in one extra 128-wide contraction slab"
    bias = mask_ref[0]
    for r in range(NSA_REP):
        qa_scr[r] = jnp.concatenate([q_ref[:, r * DH:(r + 1) * DH], bias], axis=1)
    m_scr[...] = jnp.full(m_scr.shape, NEG_INF, F32)
    acc_scr[...] = jnp.zeros(acc_scr.shape, F32)
    krow = lax.broadcasted_iota(jnp.int32, (ATT_TK, LANES), 0)
    kblk = lax.broadcasted_iota(jnp.int32, (ATT_TK, LANES), 1)

    def step(j, diagonal):
        k0 = pl.multiple_of(j * ATT_TK, ATT_TK)
        onehot = jnp.where(((krow + k0) >> 6) == kblk, 1.0, 0.0).astype(BF16)
        k_aug = jnp.concatenate([ks_ref[pl.ds(k0, ATT_TK), :], onehot], axis=1)
        v_aug = jnp.concatenate([vs_ref[pl.ds(k0, ATT_TK), :], jnp.ones((ATT_TK, LANES), BF16)], axis=1)
        if diagonal:
            r_ = lax.broadcasted_iota(jnp.int32, (ATT_TQ, ATT_TK), 0)
            c_ = lax.broadcasted_iota(jnp.int32, (ATT_TQ, ATT_TK), 1)
            causal_bias = jnp.where(c_ <= r_, 0.0, MASK_NEG)
        for r in range(NSA_REP):
            s = _dot_nt(qa_scr[r], k_aug)
            if diagonal:
                s = s + causal_bias
            m_prev = m_scr[r]
            m_new = jnp.maximum(m_prev, jnp.max(s, axis=-1, keepdims=True))
            alpha = jnp.exp2(m_prev - m_new)
            p = jnp.exp2(s - jnp.concatenate([m_new] * (ATT_TK // LANES), axis=1))
            acc_scr[r] = acc_scr[r] * jnp.concatenate([alpha, alpha], axis=1) + _dot(p.astype(BF16), v_aug)
            m_scr[r] = m_new

    def body(jj, carry):
        for u in range(4):
            step(4 * jj + u, False)
        return carry

    lax.fori_loop(0, i // 4, body, 0)
    done = (i // 4) * 4

    @pl.when((i & 2) != 0)
    def _():
        step(done, False)
        step(done + 1, False)

    @pl.when((i & 1) != 0)
    def _():
        step(i - 1, False)

    step(i, True)

    start = pl.multiple_of(jnp.maximum(qbase - WINDOW, 0), ATT_TQ)
    kw = kw_ref[pl.ds(start, WIN_SPAN), :]
    vw_aug = jnp.concatenate([vw_ref[pl.ds(start, WIN_SPAN), :], jnp.ones((WIN_SPAN, LANES), BF16)], axis=1)
    r_ = lax.broadcasted_iota(jnp.int32, (ATT_TQ, WIN_SPAN), 0)
    c_ = lax.broadcasted_iota(jnp.int32, (ATT_TQ, WIN_SPAN), 1)
    diff = (qbase + r_) - (start + c_)
    win_bias = jnp.where((diff >= 0) & (diff < WINDOW), 0.0, MASK_NEG)
    gates = jax.nn.sigmoid(gate_ref[...])
    for r in range(NSA_REP):
        s = _dot_nt(q_ref[:, r * DH:(r + 1) * DH], kw) + win_bias
        p = jnp.exp2(s - jnp.max(s, axis=-1, keepdims=True))
        ow = _dot(p.astype(BF16), vw_aug)
        o_win = ow[:, :DH] / jnp.maximum(ow[:, DH:], TINY)
        acc = acc_scr[r]
        o_sel = acc[:, :DH] / jnp.maximum(acc[:, DH:], TINY)
        o = (gates[:, 3 * r:3 * r + 1] * oc_ref[:, r * DH:(r + 1) * DH].astype(F32)
             + gates[:, 3 * r + 1:3 * r + 2] * o_sel
             + gates[:, 3 * r + 2:3 * r + 3] * o_win)
        o_ref[:, r * DH:(r + 1) * DH] = o.astype(o_ref.dtype)


def _nsa_attn(main, small, mask, o_cmp, seq):
    gw = NSA_REP * DH
    nsel = mask.shape[2]
    kv_spec = lambda col: pl.BlockSpec((seq, DH), lambda g, i: (0, col // DH + g))
    return pl.pallas_call(
        _nsa_attn_kernel,
        grid=(NSA_GROUPS, seq // ATT_TQ),
        in_specs=[pl.BlockSpec((ATT_TQ, gw), lambda g, i: (i, COL_Q // gw + g)),
                  pl.BlockSpec((1, ATT_TQ, nsel), lambda g, i: (g, i, 0)),
                  kv_spec(COL_KS), kv_spec(COL_VS), kv_spec(COL_KW), kv_spec(COL_VW),
                  pl.BlockSpec((ATT_TQ, gw), lambda g, i: (i, g)),
                  pl.BlockSpec((ATT_TQ, LANES), lambda g, i: (i, 1 + g))],
        out_specs=pl.BlockSpec((ATT_TQ, gw), lambda g, i: (i, g)),
        out_shape=jax.ShapeDtypeStruct((seq, NSA_HEADS * DH), BF16),
        scratch_shapes=[pltpu.VMEM((NSA_REP, ATT_TQ, 2 * DH), BF16),
                        pltpu.VMEM((NSA_REP, ATT_TQ, 2 * DH), F32),
                        pltpu.VMEM((NSA_REP, ATT_TQ, LANES), F32)],
        compiler_params=_params(("arbitrary", "arbitrary")),
        name="nsa_sel_win_attn",
    )(main, mask, main, main, main, main, o_cmp, small)


HALO = 8
GROUP_W = SSM_D_INNER // SSM_GROUPS
HEADS_PER_GROUP = SSM_HEADS // SSM_GROUPS


def _silu(x):
    h = 0.5 * x
    return h + h * jnp.tanh(h)


def _mamba_kernel(z_ref, xs_ref, b_ref, c_ref, dt_ref,
                  cwx_ref, cwb_ref, cwc_ref, cbx_ref, cbb_ref, cbc_ref,
                  dtb_ref, alog_ref, dfull_ref, nw_ref, expand2_ref, tril3_ref,
                  o_ref, state_scr, xs_ext, b_ext, c_ext):
    c = pl.program_id(0)
    L = SSD_CHUNK

    @pl.when(c == 0)
    def _():
        state_scr[...] = jnp.zeros(state_scr.shape, F32)
        for carry in (xs_ext, b_ext, c_ext):
            carry[...] = jnp.zeros(carry.shape, F32)

    def conv_silu(x_ref, carry, w_ref, bias_ref):
        x = x_ref[...].astype(F32)
        first_row = lax.broadcasted_iota(jnp.int32, (HALO, x.shape[1]), 0) == 0
        r = None
        for k in range(CONV_WIDTH - 1):
            t = w_ref[k:k + 1, :] * x
            if r is not None:
                t = t + r
            rolled = pltpu.roll(t, 1, 0)
            top = jnp.where(first_row, carry[k:k + 1, :], rolled[0:HALO])
            carry[k:k + 1, :] = t[L - 1:L, :]
            r = jnp.concatenate([top, rolled[HALO:]], axis=0)
        return _silu(r + w_ref[CONV_WIDTH - 1:CONV_WIDTH, :] * x + bias_ref[...])

    xs = conv_silu(xs_ref, xs_ext, cwx_ref, cbx_ref)
    bm = conv_silu(b_ref, b_ext, cwb_ref, cbb_ref).astype(BF16)
    cm = conv_silu(c_ref, c_ext, cwc_ref, cbc_ref).astype(BF16)

    xdt_raw = dt_ref[...] + dtb_ref[...]
    dt = jnp.maximum(xdt_raw, 0.0) + jnp.log1p(jnp.exp(-jnp.abs(xdt_raw)))
    adt = dt * (-jnp.exp(alog_ref[...]))
    acs = _dot(tril3_ref[...], jnp.concatenate(_split3(adt), axis=0)) * LOG2E
    acs_t = acs.T
    a_last = acs[L - 1:L, :]
    stacked = jnp.concatenate([dt, jnp.exp2(acs), jnp.exp2(a_last - acs)], axis=0)
    wide = _dot(jnp.concatenate(_split2(stacked), axis=1), expand2_ref[...])
    dt_w = wide[0:L]
    in_decay_w = wide[L:2 * L]
    out_decay_w = wide[2 * L:3 * L]

    xdt = xs * dt_w
    xst = xdt * out_decay_w
    causal = lax.broadcasted_iota(jnp.int32, (L, L), 0) >= lax.broadcasted_iota(jnp.int32, (L, L), 1)
    causal_bias = jnp.where(causal, 0.0, MASK_NEG)
    lane = lax.broadcasted_iota(jnp.int32, (L, LANES), 1)
    low_half = lane < SSM_HEAD_DIM
    chunk_decay = jnp.exp2(acs_t[:, L - 1:L])

    for g in range(SSM_GROUPS):
        gs = slice(g * GROUP_W, (g + 1) * GROUP_W)
        bg = bm[:, g * SSM_STATE:(g + 1) * SSM_STATE]
        cg = cm[:, g * SSM_STATE:(g + 1) * SSM_STATE]
        cb = _dot_nt(cg, bg)
        prev = state_scr[gs, :]
        y_off = _dot_nt(cg, prev.astype(BF16)) * in_decay_w[:, gs]
        pieces = []
        for hp in range(HEADS_PER_GROUP // 2):
            lhs = []
            for e in range(2):
                h = g * HEADS_PER_GROUP + 2 * hp + e
                decay = jnp.exp2(acs[:, h:h + 1] - acs_t[h:h + 1, :] + causal_bias)
                lhs.append((cb * decay).astype(BF16))
            slab = xdt[:, g * GROUP_W + hp * LANES:g * GROUP_W + (hp + 1) * LANES]
            rhs = jnp.concatenate([jnp.where(low_half, slab, 0.0), jnp.where(low_half, 0.0, slab)], axis=0)
            pieces.append(_dot(jnp.concatenate(lhs, axis=1), rhs.astype(BF16)))
        y_g = jnp.concatenate(pieces, axis=1) + y_off
        y_g = y_g + dfull_ref[:, gs] * xs[:, gs]
        y_g = y_g * _silu(z_ref[:, gs].astype(F32))
        ms = jnp.mean(y_g * y_g, axis=-1, keepdims=True)
        o_ref[:, gs] = (y_g * lax.rsqrt(ms + EPS) * nw_ref[:, gs]).astype(o_ref.dtype)
        new = _dot(xst[:, gs].T.astype(BF16), bg)
        cd = jnp.broadcast_to(chunk_decay[g * HEADS_PER_GROUP:(g + 1) * HEADS_PER_GROUP], (HEADS_PER_GROUP, LANES))
        cd = jnp.broadcast_to(cd[:, None, :], (HEADS_PER_GROUP, SSM_HEAD_DIM, LANES)).reshape(GROUP_W, LANES)
        state_scr[gs, :] = prev * cd + new


def _mamba(main, small, cwx, cwb, cwc, cbx, cbb, cbc, dtb, alog, dfull, nw, expand2, tril3, seq):
    L = SSD_CHUNK
    bw = SSM_GROUPS * SSM_STATE
    full = lambda a: pl.BlockSpec(a.shape, lambda c: (0,) * a.ndim)
    consts = [cwx, cwb, cwc, cbx, cbb, cbc, dtb, alog, dfull, nw, expand2, tril3]
    return pl.pallas_call(
        _mamba_kernel,
        grid=(seq // L,),
        in_specs=[pl.BlockSpec((L, SSM_D_INNER), lambda c: (c, COL_Z // SSM_D_INNER)),
                  pl.BlockSpec((L, SSM_D_INNER), lambda c: (c, COL_XS // SSM_D_INNER)),
                  pl.BlockSpec((L, bw), lambda c: (c, COL_B // bw)),
                  pl.BlockSpec((L, bw), lambda c: (c, COL_C // bw)),
                  pl.BlockSpec((L, LANES), lambda c: (c, 0))] + [full(a) for a in consts],
        out_specs=pl.BlockSpec((L, SSM_D_INNER), lambda c: (c, 0)),
        out_shape=jax.ShapeDtypeStruct((seq, SSM_D_INNER), BF16),
        scratch_shapes=[pltpu.VMEM((SSM_D_INNER, SSM_STATE), F32),
                        pltpu.VMEM((HALO, SSM_D_INNER), F32),
                        pltpu.VMEM((HALO, bw), F32),
                        pltpu.VMEM((HALO, bw), F32)],
        compiler_params=_params(("arbitrary",)),
        name="mamba2_ssd",
    )(main, main, main, main, small, *consts)


def _merge_kernel(a_ref, b_ref, pa_ref, pb_ref, ga_ref, gb_ref, o_ref):
    ya = _dot(a_ref[...], pa_ref[...])
    yb = _dot(b_ref[...], pb_ref[...])
    o = jax.nn.sigmoid(ga_ref[...].astype(F32)) * ya + jax.nn.sigmoid(gb_ref[...].astype(F32)) * yb
    o_ref[...] = o.astype(o_ref.dtype)


def _merge(o_nsa, o_ssm, pa, pb, main, seq):
    tm = min(seq, 512)
    tn = 1024
    return pl.pallas_call(
        _merge_kernel,
        grid=(seq // tm, D_MODEL // tn),
        in_specs=[pl.BlockSpec((tm, NSA_HEADS * DH), lambda i, j: (i, 0)),
                  pl.BlockSpec((tm, SSM_D_INNER), lambda i, j: (i, 0)),
                  pl.BlockSpec((NSA_HEADS * DH, tn), lambda i, j: (0, j)),
                  pl.BlockSpec((SSM_D_INNER, tn), lambda i, j: (0, j)),
                  pl.BlockSpec((tm, tn), lambda i, j: (i, COL_GA // tn + j)),
                  pl.BlockSpec((tm, tn), lambda i, j: (i, COL_GB // tn + j))],
        out_specs=pl.BlockSpec((tm, tn), lambda i, j: (i, j)),
        out_shape=jax.ShapeDtypeStruct((seq, D_MODEL), BF16),
        compiler_params=_params(("arbitrary", "arbitrary")),
        name="gated_merge",
    )(o_nsa, o_ssm, pa, pb, main, main)


def _outproj_kernel(m_ref, w_ref, x_ref, nw_ref, h_ref, hn_ref):
    h = x_ref[...] + _dot(m_ref[...], w_ref[...])
    h_ref[...] = h
    ms = jnp.mean(h * h, axis=-1, keepdims=True)
    hn_ref[...] = (h * lax.rsqrt(ms + EPS) * nw_ref[...]).astype(hn_ref.dtype)


def _outproj(merged, w_out, x2, norm_w, seq):
    tm = min(seq, 512)
    row = pl.BlockSpec((tm, D_MODEL), lambda i: (i, 0))
    return pl.pallas_call(
        _outproj_kernel,
        grid=(seq // tm,),
        in_specs=[row, pl.BlockSpec((D_MODEL, D_MODEL), lambda i: (0, 0), pipeline_mode=pl.Buffered(1)), row,
                  pl.BlockSpec((1, D_MODEL), lambda i: (0, 0))],
        out_specs=[row, row],
        out_shape=[jax.ShapeDtypeStruct((seq, D_MODEL), F32), jax.ShapeDtypeStruct((seq, D_MODEL), BF16)],
        compiler_params=_params(("arbitrary",)),
        name="outproj_residual_norm",
    )(merged, w_out, x2, norm_w.reshape(1, D_MODEL))


def _mlp_kernel(hn_ref, wu_ref, wd_ref, h_ref, nw_ref, o_ref, acc_scr):
    f = pl.program_id(1)

    @pl.when(f == 0)
    def _():
        acc_scr[...] = jnp.zeros(acc_scr.shape, F32)

    up = jnp.maximum(_dot(hn_ref[...], wu_ref[...]), 0.0)
    acc_scr[...] += _dot((up * up).astype(BF16), wd_ref[...])

    @pl.when(f == pl.num_programs(1) - 1)
    def _():
        h = h_ref[...] + acc_scr[...]
        ms = jnp.mean(h * h, axis=-1, keepdims=True)
        o_ref[...] = h * lax.rsqrt(ms + EPS) * nw_ref[...]


def _mlp(hn, w_up, w_down, h1, norm_w, seq):
    tm = min(seq, 512)
    tf = 1024
    row = lambda i, f: (i, 0)
    return pl.pallas_call(
        _mlp_kernel,
        grid=(seq // tm, D_FF // tf),
        in_specs=[pl.BlockSpec((tm, D_MODEL), row),
                  pl.BlockSpec((D_MODEL, tf), lambda i, f: (0, f)),
                  pl.BlockSpec((tf, D_MODEL), lambda i, f: (f, 0)),
                  pl.BlockSpec((tm, D_MODEL), row),
                  pl.BlockSpec((1, D_MODEL), lambda i, f: (0, 0))],
        out_specs=pl.BlockSpec((tm, D_MODEL), row),
        out_shape=jax.ShapeDtypeStruct((seq, D_MODEL), F32),
        scratch_shapes=[pltpu.VMEM((tm, D_MODEL), F32)],
        compiler_params=_params(("arbitrary", "arbitrary")),
        name="mlp_final_norm",
    )(hn, w_up, w_down, h1, norm_w.reshape(1, D_MODEL))


def _cmp_to_sel(seq):
    n_cmp = (seq - CMP_BLOCK) // CMP_STRIDE + 1
    n_sel = seq // SEL_BLOCK
    c_start = np.arange(n_cmp) * CMP_STRIDE
    s_start = np.arange(n_sel) * SEL_BLOCK
    overlap = np.clip(np.minimum(c_start[:, None] + CMP_BLOCK, s_start[None, :] + SEL_BLOCK)
                      - np.maximum(c_start[:, None], s_start[None, :]), 0, None)
    m = np.zeros((seq // CMP_STRIDE, LANES), np.float32)
    m[:n_cmp, :n_sel] = overlap / CMP_STRIDE
    return jnp.asarray(m.T, BF16)


def kernel(x, positions, norm_mix_w, w_in, cmp_pos_k, cmp_pos_v, cmp_k_w1, cmp_k_w2, cmp_v_w1, cmp_v_w2, conv_w, conv_b, dt_bias, a_log, ssm_d, ssm_norm_w, w_proj_nsa, w_proj_ssm, w_out, norm_mlp_w, w_up, w_down, norm_final_w):
    bsz, seq, _ = x.shape
    assert bsz == 1 and seq % 1024 == 0 and SEL_TOPK <= seq // SEL_BLOCK <= LANES
    assert w_in.shape[0] == 1, "one layer"
    x2 = x.reshape(seq, D_MODEL)

    o_gate = NSA_HEADS * DH + 6 * NSA_GROUPS * DH
    o_z = o_gate + 3 * NSA_HEADS
    o_dt = o_z + 2 * SSM_D_INNER + 2 * SSM_GROUPS * SSM_STATE
    o_bg = o_dt + SSM_HEADS
    cols = lambda a, n: w_in[0, :, a:a + n]
    assert o_z == SRC_Z and o_bg == SRC_BG and o_gate == ATT_W
    w_ssm = _repack_ssm_weights(w_in)
    zpad = lambda n: jnp.zeros((D_MODEL, n), F32)
    per_group = 3 * NSA_REP
    small_parts = [cols(o_dt, SSM_HEADS), zpad(LANES - SSM_HEADS)]
    for g in range(NSA_GROUPS):
        small_parts += [cols(o_gate + g * per_group, per_group), zpad(LANES - per_group)]
    w_small = jnp.concatenate(small_parts, axis=1)

    pos = positions.reshape(seq)
    cosf, sinf = _rope_tables(pos)
    att, kc_chunks, vc_chunks, u = _inproj(x2, norm_mix_w[0], w_in, cosf, sinf)
    ssm = _proj_plain(u, w_ssm)
    small = _proj_plain(u, w_small, out_dtype=F32, tm=1024)

    ncp = seq // CMP_STRIDE
    def pos2(p):
        p = p.reshape(2, CMP_STRIDE * DH)
        return jnp.concatenate([p, jnp.zeros((14, CMP_STRIDE * DH), F32)], axis=0).astype(BF16)
    pos_end = pos[CMP_BLOCK - 1::CMP_STRIDE]
    cos_c, sin_c = _rope_tables(jnp.concatenate([pos_end, jnp.zeros((ncp - pos_end.shape[0],), pos.dtype)]))
    kc, vc = _compress(kc_chunks, vc_chunks, pos2(cmp_pos_k[0]), pos2(cmp_pos_v[0]),
                       cmp_k_w1[0].astype(BF16), cmp_k_w2[0].astype(BF16),
                       cmp_v_w1[0].astype(BF16), cmp_v_w2[0].astype(BF16), cos_c, sin_c)

    o_cmp, sel_bias = _cmp_attn(att, kc, vc, _cmp_to_sel(seq), seq)
    o_nsa = _nsa_attn(att, small, sel_bias, o_cmp, seq)

    cw = conv_w[0]
    cb_ = conv_b[0].reshape(1, -1)
    lane_pad = lambda a: jnp.concatenate([a.reshape(1, -1), jnp.zeros((1, LANES - a.shape[-1]), F32)], axis=1)
    head_of = np.arange(SSM_D_INNER) // SSM_HEAD_DIM
    expand = np.arange(LANES)[:, None] == head_of[None, :]
    expand2 = jnp.asarray(np.concatenate([expand, expand], axis=0), BF16)
    tril = np.tril(np.ones((SSD_CHUNK, SSD_CHUNK), np.float32))
    tril3 = jnp.asarray(np.concatenate([tril, tril, tril], axis=1), BF16)
    o_ssm = _mamba(ssm, small, cw[:, :4096], cw[:, 4096:5120], cw[:, 5120:],
                   cb_[:, :4096], cb_[:, 4096:5120], cb_[:, 5120:],
                   lane_pad(dt_bias[0]), lane_pad(a_log[0]),
                   jnp.repeat(ssm_d[0], SSM_HEAD_DIM).reshape(1, -1), ssm_norm_w[0].reshape(1, -1),
                   expand2, tril3, seq)

    merged = _merge(o_nsa, o_ssm, w_proj_nsa[0].astype(BF16), w_proj_ssm[0].astype(BF16), ssm, seq)
    h1, hn = _outproj(merged, w_out[0].astype(BF16), x2, norm_mlp_w[0], seq)
    out = _mlp(hn, w_up[0].astype(BF16), w_down[0].astype(BF16), h1, norm_final_w, seq)
    return out.reshape(bsz, seq, D_MODEL)
```

```python
import math

import jax
import jax.numpy as jnp
import numpy as np
from jax import lax
from jax.experimental import pallas as pl
from jax.experimental.pallas import tpu as pltpu

F32 = jnp.float32
BF16 = jnp.bfloat16

D_MODEL = 2048
NSA_HEADS = 16
NSA_GROUPS = 4
NSA_REP = NSA_HEADS // NSA_GROUPS
DH = 128
CMP_BLOCK = 32
CMP_STRIDE = 16
SEL_BLOCK = 64
SEL_TOPK = 16
WINDOW = 512
ROPE_THETA = 10000.0
N_FORCED = 3
SSM_D_INNER = 4096
SSM_HEAD_DIM = 64
SSM_HEADS = 64
SSM_GROUPS = 8
SSM_STATE = 128
CONV_WIDTH = 4
SSD_CHUNK = 128
D_FF = 4 * D_MODEL
EPS = 1e-6
NEG_INF = -1e30
TINY = 1e-30
MASK_NEG = -1e9
TAKEN = -2.0
LOG2E = math.log2(math.e)
LANES = 128
VMEM_LIMIT = 56 * 1024 * 1024

COL_Q = 0
COL_KC = 2048
COL_VC = 2560
COL_KS = 3072
COL_VS = 3584
COL_KW = 4096
COL_VW = 4608
ATT_W = 5120
COL_Z = 0
COL_XS = 4096
COL_B = 8192
COL_C = 9216
COL_GA = 10240
COL_GB = 12288
SSM_W = 14336
SMALL_W = 640


def _dot(a, b):
    return jnp.dot(a, b, preferred_element_type=F32)


def _dot_nt(a, b):
    return lax.dot_general(a, b, (((1,), (1,)), ((), ())), preferred_element_type=F32)


def _dot_tn(a, b):
    return lax.dot_general(a, b, (((0,), (0,)), ((), ())), preferred_element_type=F32)


def _params(sem):
    return pltpu.CompilerParams(dimension_semantics=sem, vmem_limit_bytes=VMEM_LIMIT)


def _rope(x, cos, sin_signed):
    return x * cos + pltpu.roll(x, DH // 2, 1) * sin_signed


def _split2(x):
    hi = x.astype(BF16)
    lo = (x - hi.astype(F32)).astype(BF16)
    return hi, lo


def _split3(x):
    hi = x.astype(BF16)
    r = x - hi.astype(F32)
    mid = r.astype(BF16)
    lo = (r - mid.astype(F32)).astype(BF16)
    return hi, mid, lo


def _rope_table_kernel(pos_ref, inv_ref, sign_ref, cos_ref, sin_ref):
    ang = pos_ref[...].astype(F32) * inv_ref[...]
    cos_ref[...] = jnp.cos(ang)
    sin_ref[...] = jnp.sin(ang) * sign_ref[...]


def _rope_tables(positions):
    seq = positions.shape[0]
    half = DH // 2
    inv = ROPE_THETA ** (-jnp.arange(0, DH, 2, dtype=F32) / DH)
    inv_full = jnp.concatenate([inv, inv]).reshape(1, DH)
    sign = jnp.concatenate([-jnp.ones((half,), F32), jnp.ones((half,), F32)]).reshape(1, DH)
    tm = math.gcd(seq, 1024)
    return pl.pallas_call(
        _rope_table_kernel,
        grid=(seq // tm,),
        in_specs=[pl.BlockSpec((tm, 1), lambda i: (i, 0)),
                  pl.BlockSpec((1, DH), lambda i: (0, 0)),
                  pl.BlockSpec((1, DH), lambda i: (0, 0))],
        out_specs=[pl.BlockSpec((tm, DH), lambda i: (i, 0))] * 2,
        out_shape=[jax.ShapeDtypeStruct((seq, DH), F32)] * 2,
        compiler_params=_params(("arbitrary",)),
        name="rope_tables",
    )(positions.reshape(seq, 1), inv_full, sign)


INPROJ_TN = 512
Q_TILE_LO = COL_Q // INPROJ_TN
Q_TILE_HI = COL_KC // INPROJ_TN
KC_TILE = COL_KC // INPROJ_TN
VC_TILE = COL_VC // INPROJ_TN
KS_TILE = COL_KS // INPROJ_TN
KW_TILE = COL_KW // INPROJ_TN


def _inproj_kernel(x_ref, nw_ref, w_ref, cos_ref, sin_ref, o_ref, kcx_ref, vcx_ref, u_ref, acc_scr):
    j = pl.program_id(1)

    @pl.when(j == 0)
    def _():
        xf = x_ref[...]
        ms = jnp.mean(xf * xf, axis=-1, keepdims=True)
        u_ref[...] = (xf * lax.rsqrt(ms + EPS) * nw_ref[...]).astype(BF16)

    acc = _dot_nt(u_ref[...], w_ref[...].astype(BF16))
    is_q = jnp.logical_and(j >= Q_TILE_LO, j < Q_TILE_HI)
    is_rope = is_q | (j == KS_TILE) | (j == KW_TILE)

    @pl.when(jnp.logical_not(is_rope))
    def _():
        o_ref[...] = acc.astype(o_ref.dtype)

    @pl.when(is_rope)
    def _():
        cos = cos_ref[...]
        sin = sin_ref[...]
        scale = jnp.where(is_q, DH ** -0.5 * LOG2E, 1.0).astype(F32)
        for h in range(INPROJ_TN // DH):
            xh = acc[:, h * DH:(h + 1) * DH]
            o_ref[:, h * DH:(h + 1) * DH] = (_rope(xh, cos, sin) * scale).astype(o_ref.dtype)

    def chunked(dst_ref):
        n_chunks = acc_scr.shape[1] // CMP_STRIDE
        for g in range(NSA_GROUPS):
            acc_scr[g] = acc[:, g * DH:(g + 1) * DH]
            for l in range(CMP_STRIDE):
                rows_l = acc_scr[g, pl.ds(l, n_chunks, stride=CMP_STRIDE), :]
                dst_ref[g, :, l * DH:(l + 1) * DH] = rows_l.astype(dst_ref.dtype)

    @pl.when(j == KC_TILE)
    def _():
        chunked(kcx_ref)

    @pl.when(j == VC_TILE)
    def _():
        chunked(vcx_ref)


def _inproj(x2, norm_w, w_t, cosf, sinf):
    seq = x2.shape[0]
    tm = min(seq, 1024)
    tn = INPROJ_TN
    chunk_w = CMP_STRIDE * DH
    chunk_spec = pl.BlockSpec((NSA_GROUPS, tm // CMP_STRIDE, chunk_w), lambda i, j: (0, i, 0))
    chunk_shape = jax.ShapeDtypeStruct((NSA_GROUPS, seq // CMP_STRIDE, chunk_w), BF16)
    return pl.pallas_call(
        _inproj_kernel,
        grid=(seq // tm, ATT_W // tn),
        in_specs=[pl.BlockSpec((tm, D_MODEL), lambda i, j: (i, 0)),
                  pl.BlockSpec((1, D_MODEL), lambda i, j: (0, 0)),
                  pl.BlockSpec((tn, D_MODEL), lambda i, j: (j, 0)),
                  pl.BlockSpec((tm, DH), lambda i, j: (i, 0)),
                  pl.BlockSpec((tm, DH), lambda i, j: (i, 0))],
        out_specs=[pl.BlockSpec((tm, tn), lambda i, j: (i, j)),
                   chunk_spec, chunk_spec,
                   pl.BlockSpec((tm, D_MODEL), lambda i, j: (i, 0))],
        out_shape=[jax.ShapeDtypeStruct((seq, ATT_W), BF16),
                   chunk_shape, chunk_shape,
                   jax.ShapeDtypeStruct((seq, D_MODEL), BF16)],
        scratch_shapes=[pltpu.VMEM((NSA_GROUPS, tm, DH), F32)],
        compiler_params=_params(("arbitrary", "arbitrary")),
        name="inproj_attn",
    )(x2, norm_w.reshape(1, D_MODEL), w_t, cosf, sinf)


SSM_TN = 1024
SRC_Z = NSA_HEADS * DH + 6 * NSA_GROUPS * DH + 3 * NSA_HEADS
SRC_BG = SRC_Z + 2 * SSM_D_INNER + 2 * SSM_GROUPS * SSM_STATE + SSM_HEADS
SSM_SPLIT = COL_GA // SSM_TN


def _nt_matmul_kernel(a_ref, w_ref, o_ref):
    o_ref[...] = _dot_nt(a_ref[...], w_ref[...].astype(BF16)).astype(o_ref.dtype)


def _proj_ssm(u, w_t):
    seq, k = u.shape
    tm = min(seq, 2048)

    def src_rows(i, j):
        jump = (SRC_BG - SRC_Z - SSM_TN * SSM_SPLIT) // 16
        return ((SRC_Z // 16 + (SSM_TN // 16) * j + jump * (j // SSM_SPLIT)) * 16, 0)

    return pl.pallas_call(
        _nt_matmul_kernel,
        grid=(seq // tm, SSM_W // SSM_TN),
        in_specs=[pl.BlockSpec((tm, k), lambda i, j: (i, 0)),
                  pl.BlockSpec((pl.Element(SSM_TN), pl.Element(k)), src_rows)],
        out_specs=pl.BlockSpec((tm, SSM_TN), lambda i, j: (i, j)),
        out_shape=jax.ShapeDtypeStruct((seq, SSM_W), BF16),
        compiler_params=_params(("arbitrary", "arbitrary")),
        name="inproj_ssm",
    )(u, w_t)


def _proj_small(u, w_small_t):
    seq, k = u.shape
    n = w_small_t.shape[0]
    tm = min(seq, 1024)
    return pl.pallas_call(
        _nt_matmul_kernel,
        grid=(seq // tm,),
        in_specs=[pl.BlockSpec((tm, k), lambda i: (i, 0)),
                  pl.BlockSpec((n, k), lambda i: (0, 0))],
        out_specs=pl.BlockSpec((tm, n), lambda i: (i, 0)),
        out_shape=jax.ShapeDtypeStruct((seq, n), F32),
        compiler_params=_params(("arbitrary",)),
        name="inproj_small",
    )(u, w_small_t)


def _compress_kernel(xk_ref, xv_ref, pk_ref, pv_ref, w1k_ref, w2k_ref, w1v_ref, w2v_ref,
                     cos_ref, sin_ref, kc_ref, vc_ref):
    half = CMP_STRIDE * DH
    ncp = xk_ref.shape[1]

    def comp(x_ref, p_ref, w1_ref, w2_ref):
        x = x_ref[0]
        wt = w1_ref[0:half, :]
        wb = w1_ref[half:2 * half, :]
        a = _dot(x, wt)
        b = _dot(x, wb)
        pb = _dot(p_ref[...], wt)[0:1] + _dot(p_ref[...], wb)[1:2]
        h = a + pltpu.roll(b, ncp - 1, 0) + pb
        h = h * jax.nn.sigmoid(h)
        return _dot(h.astype(BF16), w2_ref[...])

    kc = comp(xk_ref, pk_ref, w1k_ref, w2k_ref)
    kc_ref[0] = _rope(kc, cos_ref[...], sin_ref[...]).astype(BF16)
    vc_ref[0] = comp(xv_ref, pv_ref, w1v_ref, w2v_ref).astype(BF16)


def _compress(xk, xv, pk, pv, w1k, w2k, w1v, w2v, cos_c, sin_c):
    g, ncp, width = xk.shape
    full = lambda shape: pl.BlockSpec(shape, lambda i: (0,) * len(shape))
    per_g = pl.BlockSpec((1, ncp, width), lambda i: (i, 0, 0))
    out_g = pl.BlockSpec((1, ncp, DH), lambda i: (i, 0, 0))
    return pl.pallas_call(
        _compress_kernel,
        grid=(g,),
        in_specs=[per_g, per_g, full(pk.shape), full(pv.shape), full(w1k.shape), full(w2k.shape),
                  full(w1v.shape), full(w2v.shape), full(cos_c.shape), full(sin_c.shape)],
        out_specs=[out_g, out_g],
        out_shape=[jax.ShapeDtypeStruct((g, ncp, DH), BF16)] * 2,
        compiler_params=_params(("arbitrary",)),
        name="compress",
    )(xk, xv, pk, pv, w1k, w2k, w1v, w2v, cos_c, sin_c)


CMP_TQ = 128


def _cmp_attn_kernel(q_ref, kc_ref, vc_ref, mt_ref, oc_ref, mask_ref):
    i = pl.program_id(0)
    qbase = i * CMP_TQ
    ncp = kc_ref.shape[1]
    nsel = mt_ref.shape[0]
    row = lax.broadcasted_iota(jnp.int32, (CMP_TQ, ncp), 0)
    col = lax.broadcasted_iota(jnp.int32, (CMP_TQ, ncp), 1)
    bias1 = jnp.where((col * CMP_STRIDE + (CMP_BLOCK - 1)) <= qbase + row, 0.0, MASK_NEG)
    bias = jnp.concatenate([bias1] * NSA_REP, axis=0)
    has_key = jnp.concatenate([bias1[:, 0:1] == 0.0] * NSA_REP, axis=0)
    blk = lax.broadcasted_iota(jnp.int32, (nsel, CMP_TQ), 0)
    cur = (qbase + lax.broadcasted_iota(jnp.int32, (nsel, CMP_TQ), 1)) >> 6
    future = blk > cur
    forced = (blk == 0) | (blk == cur) | (blk == cur - 1)
    blk_f = blk.astype(F32)
    mt = mt_ref[...]
    importance = []
    for g in range(NSA_GROUPS):
        qg = jnp.concatenate(
            [q_ref[:, (g * NSA_REP + r) * DH:(g * NSA_REP + r + 1) * DH] for r in range(NSA_REP)], axis=0)
        s = _dot_nt(qg, kc_ref[g]) + bias
        p = jnp.exp2(s - jnp.max(s, axis=-1, keepdims=True)).astype(BF16)
        ov = _dot(p, jnp.concatenate([vc_ref[g], jnp.ones((ncp, LANES), BF16)], axis=1))
        inv = jnp.where(has_key, 1.0 / ov[:, DH:], 0.0)
        o = ov[:, :DH] * inv
        imp = None
        for r in range(NSA_REP):
            h = g * NSA_REP + r
            rows_r = slice(r * CMP_TQ, (r + 1) * CMP_TQ)
            oc_ref[:, h * DH:(h + 1) * DH] = o[rows_r].astype(oc_ref.dtype)
            part = _dot_nt(p[rows_r], mt) * inv[rows_r]
            imp = part if imp is None else imp + part
        importance.append(imp.T)
    v = jnp.where(future | forced, -1.0, jnp.stack(importance))
    for _ in range(SEL_TOPK - N_FORCED):
        mx = jnp.max(v, axis=1, keepdims=True)
        first = jnp.min(jnp.where(v == mx, blk_f, float(nsel)), axis=1, keepdims=True)
        v = jnp.where(blk_f == first, TAKEN, v)
    bias_t = jnp.where(forced | (v == TAKEN), 0.0, MASK_NEG)
    for g in range(NSA_GROUPS):
        mask_ref[g] = bias_t[g].T.astype(mask_ref.dtype)


def _cmp_attn(main, kc, vc, mt, seq):
    nsel = mt.shape[0]
    ncp = kc.shape[1]
    return pl.pallas_call(
        _cmp_attn_kernel,
        grid=(seq // CMP_TQ,),
        in_specs=[pl.BlockSpec((CMP_TQ, NSA_HEADS * DH), lambda i: (i, COL_Q // (NSA_HEADS * DH))),
                  pl.BlockSpec((NSA_GROUPS, ncp, DH), lambda i: (0, 0, 0)),
                  pl.BlockSpec((NSA_GROUPS, ncp, DH), lambda i: (0, 0, 0)),
                  pl.BlockSpec((nsel, ncp), lambda i: (0, 0))],
        out_specs=[pl.BlockSpec((CMP_TQ, NSA_HEADS * DH), lambda i: (i, 0)),
                   pl.BlockSpec((NSA_GROUPS, CMP_TQ, nsel), lambda i: (0, i, 0))],
        out_shape=[jax.ShapeDtypeStruct((seq, NSA_HEADS * DH), BF16),
                   jax.ShapeDtypeStruct((NSA_GROUPS, seq, nsel), BF16)],
        compiler_params=_params(("arbitrary",)),
        name="cmp_attn_topk",
    )(main, kc, vc, mt)


ATT_TQ = 512
ATT_TK = 512
WIN_SPAN = WINDOW + ATT_TQ


def _nsa_attn_kernel(q_ref, mask_ref, ks_ref, vs_ref, kw_ref, vw_ref, oc_ref, gate_ref, o_ref,
                     qa_scr, acc_scr, m_scr):
    i = pl.program_id(1)
    qbase = i * ATT_TQ
    assert mask_ref.shape[2] == LANES, "the block-selection bias rides in one extra 128-wide contraction slab"
    bias = mask_ref[0]
    for r in range(NSA_REP):
        qa_scr[r] = jnp.concatenate([q_ref[:, r * DH:(r + 1) * DH], bias], axis=1)
    m_scr[...] = jnp.full(m_scr.shape, NEG_INF, F32)
    acc_scr[...] = jnp.zeros(acc_scr.shape, F32)
    krow = lax.broadcasted_iota(jnp.int32, (ATT_TK, LANES), 0)
    kblk = lax.broadcasted_iota(jnp.int32, (ATT_TK, LANES), 1)

    def step(j, diagonal):
        k0 = pl.multiple_of(j * ATT_TK, ATT_TK)
        onehot = jnp.where(((krow + k0) >> 6) == kblk, 1.0, 0.0).astype(BF16)
        k_aug = jnp.concatenate([ks_ref[pl.ds(k0, ATT_TK), :], onehot], axis=1)
        v_aug = jnp.concatenate([vs_ref[pl.ds(k0, ATT_TK), :], jnp.ones((ATT_TK, LANES), BF16)], axis=1)
        if diagonal:
            r_ = lax.broadcasted_iota(jnp.int32, (ATT_TQ, ATT_TK), 0)
            c_ = lax.broadcasted_iota(jnp.int32, (ATT_TQ, ATT_TK), 1)
            causal_bias = jnp.where(c_ <= r_, 0.0, MASK_NEG)
        for r in range(NSA_REP):
            s = _dot_nt(qa_scr[r], k_aug)
            if diagonal:
                s = s + causal_bias
            m_prev = m_scr[r]
            m_new = jnp.maximum(m_prev, jnp.max(s, axis=-1, keepdims=True))
            alpha = jnp.exp2(m_prev - m_new)
            p = jnp.exp2(s - jnp.concatenate([m_new] * (ATT_TK // LANES), axis=1))
            acc_scr[r] = acc_scr[r] * jnp.concatenate([alpha, alpha], axis=1) + _dot(p.astype(BF16), v_aug)
            m_scr[r] = m_new

    def body(jj, carry):
        for u in range(4):
            step(4 * jj + u, False)
        return carry

    lax.fori_loop(0, i // 4, body, 0)
    done = (i // 4) * 4

    @pl.when((i & 2) != 0)
    def _():
        step(done, False)
        step(done + 1, False)

    @pl.when((i & 1) != 0)
    def _():
        step(i - 1, False)

    step(i, True)

    start = pl.multiple_of(jnp.maximum(qbase - WINDOW, 0), ATT_TQ)
    kw = kw_ref[pl.ds(start, WIN_SPAN), :]
    vw_aug = jnp.concatenate([vw_ref[pl.ds(start, WIN_SPAN), :], jnp.ones((WIN_SPAN, LANES), BF16)], axis=1)
    r_ = lax.broadcasted_iota(jnp.int32, (ATT_TQ, WIN_SPAN), 0)
    c_ = lax.broadcasted_iota(jnp.int32, (ATT_TQ, WIN_SPAN), 1)
    diff = (qbase + r_) - (start + c_)
    win_bias = jnp.where((diff >= 0) & (diff < WINDOW), 0.0, MASK_NEG)
    gates = jax.nn.sigmoid(gate_ref[...])
    for r in range(NSA_REP):
        s = _dot_nt(q_ref[:, r * DH:(r + 1) * DH], kw) + win_bias
        p = jnp.exp2(s - jnp.max(s, axis=-1, keepdims=True))
        ow = _dot(p.astype(BF16), vw_aug)
        o_win = ow[:, :DH] / jnp.maximum(ow[:, DH:], TINY)
        acc = acc_scr[r]
        o_sel = acc[:, :DH] / jnp.maximum(acc[:, DH:], TINY)
        o = (gates[:, 3 * r:3 * r + 1] * oc_ref[:, r * DH:(r + 1) * DH].astype(F32)
             + gates[:, 3 * r + 1:3 * r + 2] * o_sel
             + gates[:, 3 * r + 2:3 * r + 3] * o_win)
        o_ref[:, r * DH:(r + 1) * DH] = o.astype(o_ref.dtype)


def _nsa_attn(main, small, mask, o_cmp, seq):
    gw = NSA_REP * DH
    nsel = mask.shape[2]
    kv_spec = lambda col: pl.BlockSpec((seq, DH), lambda g, i: (0, col // DH + g))
    return pl.pallas_call(
        _nsa_attn_kernel,
        grid=(NSA_GROUPS, seq // ATT_TQ),
        in_specs=[pl.BlockSpec((ATT_TQ, gw), lambda g, i: (i, COL_Q // gw + g)),
                  pl.BlockSpec((1, ATT_TQ, nsel), lambda g, i: (g, i, 0)),
                  kv_spec(COL_KS), kv_spec(COL_VS), kv_spec(COL_KW), kv_spec(COL_VW),
                  pl.BlockSpec((ATT_TQ, gw), lambda g, i: (i, g)),
                  pl.BlockSpec((ATT_TQ, LANES), lambda g, i: (i, 1 + g))],
        out_specs=pl.BlockSpec((ATT_TQ, gw), lambda g, i: (i, g)),
        out_shape=jax.ShapeDtypeStruct((seq, NSA_HEADS * DH), BF16),
        scratch_shapes=[pltpu.VMEM((NSA_REP, ATT_TQ, 2 * DH), BF16),
                        pltpu.VMEM((NSA_REP, ATT_TQ, 2 * DH), F32),
                        pltpu.VMEM((NSA_REP, ATT_TQ, LANES), F32)],
        compiler_params=_params(("arbitrary", "arbitrary")),
        name="nsa_sel_win_attn",
    )(main, mask, main, main, main, main, o_cmp, small)


HALO = 8
GROUP_W = SSM_D_INNER // SSM_GROUPS
HEADS_PER_GROUP = SSM_HEADS // SSM_GROUPS


def _silu(x):
    h = 0.5 * x
    return h + h * jnp.tanh(h)


def _mamba_kernel(z_ref, xs_ref, b_ref, c_ref, dt_ref,
                  cwx_ref, cwb_ref, cwc_ref, cbx_ref, cbb_ref, cbc_ref,
                  dtb_ref, alog_ref, dfull_ref, nw_ref, expand2_ref, tril3_ref,
                  o_ref, state_scr, xs_ext, b_ext, c_ext):
    c = pl.program_id(0)
    L = SSD_CHUNK

    @pl.when(c == 0)
    def _():
        state_scr[...] = jnp.zeros(state_scr.shape, F32)
        for carry in (xs_ext, b_ext, c_ext):
            carry[...] = jnp.zeros(carry.shape, F32)

    def conv_silu(x_ref, carry, w_ref, bias_ref):
        x = x_ref[...].astype(F32)
        first_row = lax.broadcasted_iota(jnp.int32, (HALO, x.shape[1]), 0) == 0
        r = None
        for k in range(CONV_WIDTH - 1):
            t = w_ref[k:k + 1, :] * x
            if r is not None:
                t = t + r
            rolled = pltpu.roll(t, 1, 0)
            top = jnp.where(first_row, carry[k:k + 1, :], rolled[0:HALO])
            carry[k:k + 1, :] = t[L - 1:L, :]
            r = jnp.concatenate([top, rolled[HALO:]], axis=0)
        return _silu(r + w_ref[CONV_WIDTH - 1:CONV_WIDTH, :] * x + bias_ref[...])

    xs = conv_silu(xs_ref, xs_ext, cwx_ref, cbx_ref)
    bm = conv_silu(b_ref, b_ext, cwb_ref, cbb_ref).astype(BF16)
    cm = conv_silu(c_ref, c_ext, cwc_ref, cbc_ref).astype(BF16)

    xdt_raw = dt_ref[...] + dtb_ref[...]
    dt = jnp.maximum(xdt_raw, 0.0) + jnp.log1p(jnp.exp(-jnp.abs(xdt_raw)))
    adt = dt * (-jnp.exp(alog_ref[...]))
    acs = _dot(tril3_ref[...], jnp.concatenate(_split3(adt), axis=0)) * LOG2E
    acs_t = acs.T
    a_last = acs[L - 1:L, :]
    stacked = jnp.concatenate([dt, jnp.exp2(acs), jnp.exp2(a_last - acs)], axis=0)
    wide = _dot(jnp.concatenate(_split2(stacked), axis=1), expand2_ref[...])
    dt_w = wide[0:L]
    in_decay_w = wide[L:2 * L]
    out_decay_w = wide[2 * L:3 * L]

    xdt = xs * dt_w
    xst = xdt * out_decay_w
    causal = lax.broadcasted_iota(jnp.int32, (L, L), 0) >= lax.broadcasted_iota(jnp.int32, (L, L), 1)
    causal_bias = jnp.where(causal, 0.0, MASK_NEG)
    lane = lax.broadcasted_iota(jnp.int32, (L, LANES), 1)
    low_half = lane < SSM_HEAD_DIM
    chunk_decay = jnp.exp2(acs_t[:, L - 1:L])

    for g in range(SSM_GROUPS):
        gs = slice(g * GROUP_W, (g + 1) * GROUP_W)
        bg = bm[:, g * SSM_STATE:(g + 1) * SSM_STATE]
        cg = cm[:, g * SSM_STATE:(g + 1) * SSM_STATE]
        cb = _dot_nt(cg, bg)
        prev = state_scr[gs, :]
        y_off = _dot_nt(cg, prev.astype(BF16)) * in_decay_w[:, gs]
        pieces = []
        for hp in range(HEADS_PER_GROUP // 2):
            lhs = []
            for e in range(2):
                h = g * HEADS_PER_GROUP + 2 * hp + e
                decay = jnp.exp2(acs[:, h:h + 1] - acs_t[h:h + 1, :] + causal_bias)
                lhs.append((cb * decay).astype(BF16))
            slab = xdt[:, g * GROUP_W + hp * LANES:g * GROUP_W + (hp + 1) * LANES]
            rhs = jnp.concatenate([jnp.where(low_half, slab, 0.0), jnp.where(low_half, 0.0, slab)], axis=0)
            pieces.append(_dot(jnp.concatenate(lhs, axis=1), rhs.astype(BF16)))
        y_g = jnp.concatenate(pieces, axis=1) + y_off
        y_g = y_g + dfull_ref[:, gs] * xs[:, gs]
        y_g = y_g * _silu(z_ref[:, gs].astype(F32))
        ms = jnp.mean(y_g * y_g, axis=-1, keepdims=True)
        o_ref[:, gs] = (y_g * lax.rsqrt(ms + EPS) * nw_ref[:, gs]).astype(o_ref.dtype)
        new = _dot(xst[:, gs].T.astype(BF16), bg)
        cd = jnp.broadcast_to(chunk_decay[g * HEADS_PER_GROUP:(g + 1) * HEADS_PER_GROUP], (HEADS_PER_GROUP, LANES))
        cd = jnp.broadcast_to(cd[:, None, :], (HEADS_PER_GROUP, SSM_HEAD_DIM, LANES)).reshape(GROUP_W, LANES)
        state_scr[gs, :] = prev * cd + new


def _mamba(main, small, cwx, cwb, cwc, cbx, cbb, cbc, dtb, alog, dfull, nw, expand2, tril3, seq):
    L = SSD_CHUNK
    bw = SSM_GROUPS * SSM_STATE
    full = lambda a: pl.BlockSpec(a.shape, lambda c: (0,) * a.ndim)
    consts = [cwx, cwb, cwc, cbx, cbb, cbc, dtb, alog, dfull, nw, expand2, tril3]
    return pl.pallas_call(
        _mamba_kernel,
        grid=(seq // L,),
        in_specs=[pl.BlockSpec((L, SSM_D_INNER), lambda c: (c, COL_Z // SSM_D_INNER)),
                  pl.BlockSpec((L, SSM_D_INNER), lambda c: (c, COL_XS // SSM_D_INNER)),
                  pl.BlockSpec((L, bw), lambda c: (c, COL_B // bw)),
                  pl.BlockSpec((L, bw), lambda c: (c, COL_C // bw)),
                  pl.BlockSpec((L, LANES), lambda c: (c, 0))] + [full(a) for a in consts],
        out_specs=pl.BlockSpec((L, SSM_D_INNER), lambda c: (c, 0)),
        out_shape=jax.ShapeDtypeStruct((seq, SSM_D_INNER), BF16),
        scratch_shapes=[pltpu.VMEM((SSM_D_INNER, SSM_STATE), F32),
                        pltpu.VMEM((HALO, SSM_D_INNER), F32),
                        pltpu.VMEM((HALO, bw), F32),
                        pltpu.VMEM((HALO, bw), F32)],
        compiler_params=_params(("arbitrary",)),
        name="mamba2_ssd",
    )(main, main, main, main, small, *consts)


def _merge_kernel(a_ref, b_ref, pa_ref, pb_ref, ga_ref, gb_ref, o_ref):
    ya = _dot(a_ref[...], pa_ref[...])
    yb = _dot(b_ref[...], pb_ref[...])
    o = jax.nn.sigmoid(ga_ref[...].astype(F32)) * ya + jax.nn.sigmoid(gb_ref[...].astype(F32)) * yb
    o_ref[...] = o.astype(o_ref.dtype)


def _merge(o_nsa, o_ssm, pa, pb, main, seq):
    tm = min(seq, 512)
    tn = 1024
    return pl.pallas_call(
        _merge_kernel,
        grid=(seq // tm, D_MODEL // tn),
        in_specs=[pl.BlockSpec((tm, NSA_HEADS * DH), lambda i, j: (i, 0)),
                  pl.BlockSpec((tm, SSM_D_INNER), lambda i, j: (i, 0)),
                  pl.BlockSpec((NSA_HEADS * DH, tn), lambda i, j: (0, j)),
                  pl.BlockSpec((SSM_D_INNER, tn), lambda i, j: (0, j)),
                  pl.BlockSpec((tm, tn), lambda i, j: (i, COL_GA // tn + j)),
                  pl.BlockSpec((tm, tn), lambda i, j: (i, COL_GB // tn + j))],
        out_specs=pl.BlockSpec((tm, tn), lambda i, j: (i, j)),
        out_shape=jax.ShapeDtypeStruct((seq, D_MODEL), BF16),
        compiler_params=_params(("arbitrary", "arbitrary")),
        name="gated_merge",
    )(o_nsa, o_ssm, pa, pb, main, main)


def _outproj_kernel(m_ref, w_ref, x_ref, nw_ref, h_ref, hn_ref):
    h = x_ref[...] + _dot(m_ref[...], w_ref[...])
    h_ref[...] = h
    ms = jnp.mean(h * h, axis=-1, keepdims=True)
    hn_ref[...] = (h * lax.rsqrt(ms + EPS) * nw_ref[...]).astype(hn_ref.dtype)


def _outproj(merged, w_out, x2, norm_w, seq):
    tm = min(seq, 512)
    row = pl.BlockSpec((tm, D_MODEL), lambda i: (i, 0))
    return pl.pallas_call(
        _outproj_kernel,
        grid=(seq // tm,),
        in_specs=[row, pl.BlockSpec((D_MODEL, D_MODEL), lambda i: (0, 0), pipeline_mode=pl.Buffered(1)), row,
                  pl.BlockSpec((1, D_MODEL), lambda i: (0, 0))],
        out_specs=[row, row],
        out_shape=[jax.ShapeDtypeStruct((seq, D_MODEL), F32), jax.ShapeDtypeStruct((seq, D_MODEL), BF16)],
        compiler_params=_params(("arbitrary",)),
        name="outproj_residual_norm",
    )(merged, w_out, x2, norm_w.reshape(1, D_MODEL))


def _mlp_kernel(hn_ref, wu_ref, wd_ref, h_ref, nw_ref, o_ref, acc_scr):
    f = pl.program_id(1)

    @pl.when(f == 0)
    def _():
        acc_scr[...] = jnp.zeros(acc_scr.shape, F32)

    up = jnp.maximum(_dot(hn_ref[...], wu_ref[...]), 0.0)
    acc_scr[...] += _dot((up * up).astype(BF16), wd_ref[...])

    @pl.when(f == pl.num_programs(1) - 1)
    def _():
        h = h_ref[...] + acc_scr[...]
        ms = jnp.mean(h * h, axis=-1, keepdims=True)
        o_ref[...] = h * lax.rsqrt(ms + EPS) * nw_ref[...]


def _mlp(hn, w_up, w_down, h1, norm_w, seq):
    tm = min(seq, 512)
    tf = 1024
    row = lambda i, f: (i, 0)
    return pl.pallas_call(
        _mlp_kernel,
        grid=(seq // tm, D_FF // tf),
        in_specs=[pl.BlockSpec((tm, D_MODEL), row),
                  pl.BlockSpec((D_MODEL, tf), lambda i, f: (0, f)),
                  pl.BlockSpec((tf, D_MODEL), lambda i, f: (f, 0)),
                  pl.BlockSpec((tm, D_MODEL), row),
                  pl.BlockSpec((1, D_MODEL), lambda i, f: (0, 0))],
        out_specs=pl.BlockSpec((tm, D_MODEL), row),
        out_shape=jax.ShapeDtypeStruct((seq, D_MODEL), F32),
        scratch_shapes=[pltpu.VMEM((tm, D_MODEL), F32)],
        compiler_params=_params(("arbitrary", "arbitrary")),
        name="mlp_final_norm",
    )(hn, w_up, w_down, h1, norm_w.reshape(1, D_MODEL))


def _cmp_to_sel(seq):
    n_cmp = (seq - CMP_BLOCK) // CMP_STRIDE + 1
    n_sel = seq // SEL_BLOCK
    c_start = np.arange(n_cmp) * CMP_STRIDE
    s_start = np.arange(n_sel) * SEL_BLOCK
    overlap = np.clip(np.minimum(c_start[:, None] + CMP_BLOCK, s_start[None, :] + SEL_BLOCK)
                      - np.maximum(c_start[:, None], s_start[None, :]), 0, None)
    m = np.zeros((seq // CMP_STRIDE, LANES), np.float32)
    m[:n_cmp, :n_sel] = overlap / CMP_STRIDE
    return jnp.asarray(m.T, BF16)


def kernel(x, positions, norm_mix_w, w_in, cmp_pos_k, cmp_pos_v, cmp_k_w1, cmp_k_w2, cmp_v_w1, cmp_v_w2, conv_w, conv_b, dt_bias, a_log, ssm_d, ssm_norm_w, w_proj_nsa, w_proj_ssm, w_out, norm_mlp_w, w_up, w_down, norm_final_w):
    bsz, seq, _ = x.shape
    assert bsz == 1 and seq % 1024 == 0 and SEL_TOPK <= seq // SEL_BLOCK <= LANES
    assert w_in.shape[0] == 1, "one layer"
    x2 = x.reshape(seq, D_MODEL)

    o_gate = NSA_HEADS * DH + 6 * NSA_GROUPS * DH
    o_z = o_gate + 3 * NSA_HEADS
    o_dt = o_z + 2 * SSM_D_INNER + 2 * SSM_GROUPS * SSM_STATE
    o_bg = o_dt + SSM_HEADS
    assert o_z == SRC_Z and o_bg == SRC_BG and o_gate == ATT_W
    w_t = jnp.swapaxes(w_in, 1, 2)[0]
    rows = lambda a, n: w_t[a:a + n, :]
    zrows = lambda n: jnp.zeros((n, D_MODEL), F32)
    per_group = 3 * NSA_REP
    small_parts = [rows(o_dt, SSM_HEADS), zrows(LANES - SSM_HEADS)]
    for g in range(NSA_GROUPS):
        small_parts += [rows(o_gate + g * per_group, per_group), zrows(LANES - per_group)]
    w_small_t = jnp.concatenate(small_parts, axis=0)

    pos = positions.reshape(seq)
    cosf, sinf = _rope_tables(pos)
    att, kc_chunks, vc_chunks, u = _inproj(x2, norm_mix_w[0], w_t, cosf, sinf)
    ssm = _proj_ssm(u, w_t)
    small = _proj_small(u, w_small_t)

    ncp = seq // CMP_STRIDE
    def pos2(p):
        p = p.reshape(2, CMP_STRIDE * DH)
        return jnp.concatenate([p, jnp.zeros((14, CMP_STRIDE * DH), F32)], axis=0).astype(BF16)
    pos_end = pos[CMP_BLOCK - 1::CMP_STRIDE]
    cos_c, sin_c = _rope_tables(jnp.concatenate([pos_end, jnp.zeros((ncp - pos_end.shape[0],), pos.dtype)]))
    kc, vc = _compress(kc_chunks, vc_chunks, pos2(cmp_pos_k[0]), pos2(cmp_pos_v[0]),
                       cmp_k_w1[0].astype(BF16), cmp_k_w2[0].astype(BF16),
                       cmp_v_w1[0].astype(BF16), cmp_v_w2[0].astype(BF16), cos_c, sin_c)

    o_cmp, sel_bias = _cmp_attn(att, kc, vc, _cmp_to_sel(seq), seq)
    o_nsa = _nsa_attn(att, small, sel_bias, o_cmp, seq)

    cw = conv_w[0]
    cb_ = conv_b[0].reshape(1, -1)
    lane_pad = lambda a: jnp.concatenate([a.reshape(1, -1), jnp.zeros((1, LANES - a.shape[-1]), F32)], axis=1)
    head_of = np.arange(SSM_D_INNER) // SSM_HEAD_DIM
    expand = np.arange(LANES)[:, None] == head_of[None, :]
    expand2 = jnp.asarray(np.concatenate([expand, expand], axis=0), BF16)
    tril = np.tril(np.ones((SSD_CHUNK, SSD_CHUNK), np.float32))
    tril3 = jnp.asarray(np.concatenate([tril, tril, tril], axis=1), BF16)
    o_ssm = _mamba(ssm, small, cw[:, :4096], cw[:, 4096:5120], cw[:, 5120:],
                   cb_[:, :4096], cb_[:, 4096:5120], cb_[:, 5120:],
                   lane_pad(dt_bias[0]), lane_pad(a_log[0]),
                   jnp.repeat(ssm_d[0], SSM_HEAD_DIM).reshape(1, -1), ssm_norm_w[0].reshape(1, -1),
                   expand2, tril3, seq)

    merged = _merge(o_nsa, o_ssm, w_proj_nsa[0].astype(BF16), w_proj_ssm[0].astype(BF16), ssm, seq)
    h1, hn = _outproj(merged, w_out[0].astype(BF16), x2, norm_mlp_w[0], seq)
    out = _mlp(hn, w_up[0].astype(BF16), w_down[0].astype(BF16), h1, norm_final_w, seq)
    return out.reshape(bsz, seq, D_MODEL)
```

```python
import math

import jax
import jax.numpy as jnp
import numpy as np
from jax import lax
from jax.experimental import pallas as pl
from jax.experimental.pallas import tpu as pltpu

F32 = jnp.float32
BF16 = jnp.bfloat16

D_MODEL = 2048
NSA_HEADS = 16
NSA_GROUPS = 4
NSA_REP = NSA_HEADS // NSA_GROUPS
DH = 128
CMP_BLOCK = 32
CMP_STRIDE = 16
SEL_BLOCK = 64
SEL_TOPK = 16
WINDOW = 512
ROPE_THETA = 10000.0
N_FORCED = 3
SSM_D_INNER = 4096
SSM_HEAD_DIM = 64
SSM_HEADS = 64
SSM_GROUPS = 8
SSM_STATE = 128
CONV_WIDTH = 4
SSD_CHUNK = 128
D_FF = 4 * D_MODEL
EPS = 1e-6
NEG_INF = -1e30
TINY = 1e-30
MASK_NEG = -1e9
TAKEN = -2.0
LOG2E = math.log2(math.e)
LANES = 128
VMEM_LIMIT = 56 * 1024 * 1024

COL_Q = 0
COL_KC = 2048
COL_VC = 2560
COL_KS = 3072
COL_VS = 3584
COL_KW = 4096
COL_VW = 4608
ATT_W = 5120
COL_Z = 0
COL_XS = 4096
COL_B = 8192
COL_C = 9216
COL_GA = 10240
COL_GB = 12288
SSM_W = 14336
SMALL_W = 128


def _dot(a, b):
    return jnp.dot(a, b, preferred_element_type=F32)


def _dot_nt(a, b):
    return lax.dot_general(a, b, (((1,), (1,)), ((), ())), preferred_element_type=F32)


def _dot_tn(a, b):
    return lax.dot_general(a, b, (((0,), (0,)), ((), ())), preferred_element_type=F32)


def _params(sem):
    return pltpu.CompilerParams(dimension_semantics=sem, vmem_limit_bytes=VMEM_LIMIT)


def _rope(x, cos, sin_signed):
    return x * cos + pltpu.roll(x, DH // 2, 1) * sin_signed


def _split2(x):
    hi = x.astype(BF16)
    lo = (x - hi.astype(F32)).astype(BF16)
    return hi, lo


def _split3(x):
    hi = x.astype(BF16)
    r = x - hi.astype(F32)
    mid = r.astype(BF16)
    lo = (r - mid.astype(F32)).astype(BF16)
    return hi, mid, lo


def _rope_table_kernel(pos_ref, inv_ref, sign_ref, cos_ref, sin_ref):
    ang = pos_ref[...].astype(F32) * inv_ref[...]
    cos_ref[...] = jnp.cos(ang)
    sin_ref[...] = jnp.sin(ang) * sign_ref[...]


def _rope_tables(positions):
    seq = positions.shape[0]
    half = DH // 2
    inv = ROPE_THETA ** (-jnp.arange(0, DH, 2, dtype=F32) / DH)
    inv_full = jnp.concatenate([inv, inv]).reshape(1, DH)
    sign = jnp.concatenate([-jnp.ones((half,), F32), jnp.ones((half,), F32)]).reshape(1, DH)
    tm = math.gcd(seq, 1024)
    return pl.pallas_call(
        _rope_table_kernel,
        grid=(seq // tm,),
        in_specs=[pl.BlockSpec((tm, 1), lambda i: (i, 0)),
                  pl.BlockSpec((1, DH), lambda i: (0, 0)),
                  pl.BlockSpec((1, DH), lambda i: (0, 0))],
        out_specs=[pl.BlockSpec((tm, DH), lambda i: (i, 0))] * 2,
        out_shape=[jax.ShapeDtypeStruct((seq, DH), F32)] * 2,
        compiler_params=_params(("arbitrary",)),
        name="rope_tables",
    )(positions.reshape(seq, 1), inv_full, sign)


INPROJ_TN = 512
Q_TILE_LO = COL_Q // INPROJ_TN
Q_TILE_HI = COL_KC // INPROJ_TN
KC_TILE = COL_KC // INPROJ_TN
VC_TILE = COL_VC // INPROJ_TN
KS_TILE = COL_KS // INPROJ_TN
KW_TILE = COL_KW // INPROJ_TN


def _inproj_kernel(x_ref, nw_ref, w_ref, cos_ref, sin_ref, o_ref, kcx_ref, vcx_ref, u_ref, acc_scr):
    j = pl.program_id(1)

    @pl.when(j == 0)
    def _():
        xf = x_ref[...]
        ms = jnp.mean(xf * xf, axis=-1, keepdims=True)
        u_ref[...] = (xf * lax.rsqrt(ms + EPS) * nw_ref[...]).astype(BF16)

    acc = _dot_nt(u_ref[...], w_ref[...].astype(BF16))
    is_q = jnp.logical_and(j >= Q_TILE_LO, j < Q_TILE_HI)
    is_rope = is_q | (j == KS_TILE) | (j == KW_TILE)

    @pl.when(jnp.logical_not(is_rope))
    def _():
        o_ref[...] = acc.astype(o_ref.dtype)

    @pl.when(is_rope)
    def _():
        cos = cos_ref[...]
        sin = sin_ref[...]
        scale = jnp.where(is_q, DH ** -0.5 * LOG2E, 1.0).astype(F32)
        for h in range(INPROJ_TN // DH):
            xh = acc[:, h * DH:(h + 1) * DH]
            o_ref[:, h * DH:(h + 1) * DH] = (_rope(xh, cos, sin) * scale).astype(o_ref.dtype)

    def chunked(dst_ref):
        n_chunks = acc_scr.shape[1] // CMP_STRIDE
        for g in range(NSA_GROUPS):
            acc_scr[g] = acc[:, g * DH:(g + 1) * DH]
            for l in range(CMP_STRIDE):
                rows_l = acc_scr[g, pl.ds(l, n_chunks, stride=CMP_STRIDE), :]
                dst_ref[g, :, l * DH:(l + 1) * DH] = rows_l.astype(dst_ref.dtype)

    @pl.when(j == KC_TILE)
    def _():
        chunked(kcx_ref)

    @pl.when(j == VC_TILE)
    def _():
        chunked(vcx_ref)


def _inproj(x2, norm_w, w_t, cosf, sinf):
    seq = x2.shape[0]
    tm = min(seq, 1024)
    tn = INPROJ_TN
    chunk_w = CMP_STRIDE * DH
    chunk_spec = pl.BlockSpec((NSA_GROUPS, tm // CMP_STRIDE, chunk_w), lambda i, j: (0, i, 0))
    chunk_shape = jax.ShapeDtypeStruct((NSA_GROUPS, seq // CMP_STRIDE, chunk_w), BF16)
    return pl.pallas_call(
        _inproj_kernel,
        grid=(seq // tm, ATT_W // tn),
        in_specs=[pl.BlockSpec((tm, D_MODEL), lambda i, j: (i, 0)),
                  pl.BlockSpec((1, D_MODEL), lambda i, j: (0, 0)),
                  pl.BlockSpec((tn, D_MODEL), lambda i, j: (j, 0)),
                  pl.BlockSpec((tm, DH), lambda i, j: (i, 0)),
                  pl.BlockSpec((tm, DH), lambda i, j: (i, 0))],
        out_specs=[pl.BlockSpec((tm, tn), lambda i, j: (i, j)),
                   chunk_spec, chunk_spec,
                   pl.BlockSpec((tm, D_MODEL), lambda i, j: (i, 0))],
        out_shape=[jax.ShapeDtypeStruct((seq, ATT_W), BF16),
                   chunk_shape, chunk_shape,
                   jax.ShapeDtypeStruct((seq, D_MODEL), BF16)],
        scratch_shapes=[pltpu.VMEM((NSA_GROUPS, tm, DH), F32)],
        compiler_params=_params(("arbitrary", "arbitrary")),
        name="inproj_attn",
    )(x2, norm_w.reshape(1, D_MODEL), w_t, cosf, sinf)


SSM_TN = 1024
SRC_Z = NSA_HEADS * DH + 6 * NSA_GROUPS * DH + 3 * NSA_HEADS
SRC_BG = SRC_Z + 2 * SSM_D_INNER + 2 * SSM_GROUPS * SSM_STATE + SSM_HEADS
SSM_SPLIT = COL_GA // SSM_TN


def _nt_matmul_kernel(a_ref, w_ref, o_ref):
    o_ref[...] = _dot_nt(a_ref[...], w_ref[...].astype(BF16)).astype(o_ref.dtype)


def _proj_ssm(u, w_t):
    seq, k = u.shape
    tm = min(seq, 2048)

    def src_rows(i, j):
        jump = (SRC_BG - SRC_Z - SSM_TN * SSM_SPLIT) // 16
        return ((SRC_Z // 16 + (SSM_TN // 16) * j + jump * (j // SSM_SPLIT)) * 16, 0)

    return pl.pallas_call(
        _nt_matmul_kernel,
        grid=(seq // tm, SSM_W // SSM_TN),
        in_specs=[pl.BlockSpec((tm, k), lambda i, j: (i, 0)),
                  pl.BlockSpec((pl.Element(SSM_TN), pl.Element(k)), src_rows)],
        out_specs=pl.BlockSpec((tm, SSM_TN), lambda i, j: (i, j)),
        out_shape=jax.ShapeDtypeStruct((seq, SSM_W), BF16),
        compiler_params=_params(("arbitrary", "arbitrary")),
        name="inproj_ssm",
    )(u, w_t)


def _proj_small(u, w_small_t):
    seq, k = u.shape
    n = w_small_t.shape[0]
    tm = min(seq, 1024)
    return pl.pallas_call(
        _nt_matmul_kernel,
        grid=(seq // tm,),
        in_specs=[pl.BlockSpec((tm, k), lambda i: (i, 0)),
                  pl.BlockSpec((n, k), lambda i: (0, 0))],
        out_specs=pl.BlockSpec((tm, n), lambda i: (i, 0)),
        out_shape=jax.ShapeDtypeStruct((seq, n), F32),
        compiler_params=_params(("arbitrary",)),
        name="inproj_small",
    )(u, w_small_t)


def _compress_kernel(xk_ref, xv_ref, pk_ref, pv_ref, w1k_ref, w2k_ref, w1v_ref, w2v_ref,
                     cos_ref, sin_ref, kc_ref, vc_ref):
    half = CMP_STRIDE * DH
    ncp = xk_ref.shape[1]

    def comp(x_ref, p_ref, w1_ref, w2_ref):
        x = x_ref[0]
        wt = w1_ref[0:half, :]
        wb = w1_ref[half:2 * half, :]
        a = _dot(x, wt)
        b = _dot(x, wb)
        pb = _dot(p_ref[...], wt)[0:1] + _dot(p_ref[...], wb)[1:2]
        h = a + pltpu.roll(b, ncp - 1, 0) + pb
        h = h * jax.nn.sigmoid(h)
        return _dot(h.astype(BF16), w2_ref[...])

    kc = comp(xk_ref, pk_ref, w1k_ref, w2k_ref)
    kc_ref[0] = _rope(kc, cos_ref[...], sin_ref[...]).astype(BF16)
    vc_ref[0] = comp(xv_ref, pv_ref, w1v_ref, w2v_ref).astype(BF16)


def _compress(xk, xv, pk, pv, w1k, w2k, w1v, w2v, cos_c, sin_c):
    g, ncp, width = xk.shape
    full = lambda shape: pl.BlockSpec(shape, lambda i: (0,) * len(shape))
    per_g = pl.BlockSpec((1, ncp, width), lambda i: (i, 0, 0))
    out_g = pl.BlockSpec((1, ncp, DH), lambda i: (i, 0, 0))
    return pl.pallas_call(
        _compress_kernel,
        grid=(g,),
        in_specs=[per_g, per_g, full(pk.shape), full(pv.shape), full(w1k.shape), full(w2k.shape),
                  full(w1v.shape), full(w2v.shape), full(cos_c.shape), full(sin_c.shape)],
        out_specs=[out_g, out_g],
        out_shape=[jax.ShapeDtypeStruct((g, ncp, DH), BF16)] * 2,
        compiler_params=_params(("arbitrary",)),
        name="compress",
    )(xk, xv, pk, pv, w1k, w2k, w1v, w2v, cos_c, sin_c)


CMP_TQ = 128


def _cmp_attn_kernel(q_ref, kc_ref, vc_ref, mt_ref, oc_ref, mask_ref):
    i = pl.program_id(0)
    qbase = i * CMP_TQ
    ncp = kc_ref.shape[1]
    nsel = mt_ref.shape[0]
    row = lax.broadcasted_iota(jnp.int32, (CMP_TQ, ncp), 0)
    col = lax.broadcasted_iota(jnp.int32, (CMP_TQ, ncp), 1)
    bias1 = jnp.where((col * CMP_STRIDE + (CMP_BLOCK - 1)) <= qbase + row, 0.0, MASK_NEG)
    bias = jnp.concatenate([bias1] * NSA_REP, axis=0)
    has_key = jnp.concatenate([bias1[:, 0:1] == 0.0] * NSA_REP, axis=0)
    blk = lax.broadcasted_iota(jnp.int32, (nsel, CMP_TQ), 0)
    cur = (qbase + lax.broadcasted_iota(jnp.int32, (nsel, CMP_TQ), 1)) >> 6
    future = blk > cur
    forced = (blk == 0) | (blk == cur) | (blk == cur - 1)
    blk_f = blk.astype(F32)
    mt = mt_ref[...]
    importance = []
    for g in range(NSA_GROUPS):
        qg = jnp.concatenate(
            [q_ref[:, (g * NSA_REP + r) * DH:(g * NSA_REP + r + 1) * DH] for r in range(NSA_REP)], axis=0)
        s = _dot_nt(qg, kc_ref[g]) + bias
        p = jnp.exp2(s - jnp.max(s, axis=-1, keepdims=True)).astype(BF16)
        ov = _dot(p, jnp.concatenate([vc_ref[g], jnp.ones((ncp, LANES), BF16)], axis=1))
        inv = jnp.where(has_key, 1.0 / ov[:, DH:], 0.0)
        o = ov[:, :DH] * inv
        imp = None
        for r in range(NSA_REP):
            h = g * NSA_REP + r
            rows_r = slice(r * CMP_TQ, (r + 1) * CMP_TQ)
            oc_ref[:, h * DH:(h + 1) * DH] = o[rows_r].astype(oc_ref.dtype)
            part = _dot_nt(p[rows_r], mt) * inv[rows_r]
            imp = part if imp is None else imp + part
        importance.append(imp.T)
    v = jnp.where(future | forced, -1.0, jnp.stack(importance))
    for _ in range(SEL_TOPK - N_FORCED):
        mx = jnp.max(v, axis=1, keepdims=True)
        first = jnp.min(jnp.where(v == mx, blk_f, float(nsel)), axis=1, keepdims=True)
        v = jnp.where(blk_f == first, TAKEN, v)
    bias_t = jnp.where(forced | (v == TAKEN), 0.0, MASK_NEG)
    for g in range(NSA_GROUPS):
        mask_ref[g] = bias_t[g].T.astype(mask_ref.dtype)


def _cmp_attn(main, kc, vc, mt, seq):
    nsel = mt.shape[0]
    ncp = kc.shape[1]
    return pl.pallas_call(
        _cmp_attn_kernel,
        grid=(seq // CMP_TQ,),
        in_specs=[pl.BlockSpec((CMP_TQ, NSA_HEADS * DH), lambda i: (i, COL_Q // (NSA_HEADS * DH))),
                  pl.BlockSpec((NSA_GROUPS, ncp, DH), lambda i: (0, 0, 0)),
                  pl.BlockSpec((NSA_GROUPS, ncp, DH), lambda i: (0, 0, 0)),
                  pl.BlockSpec((nsel, ncp), lambda i: (0, 0))],
        out_specs=[pl.BlockSpec((CMP_TQ, NSA_HEADS * DH), lambda i: (i, 0)),
                   pl.BlockSpec((NSA_GROUPS, CMP_TQ, nsel), lambda i: (0, i, 0))],
        out_shape=[jax.ShapeDtypeStruct((seq, NSA_HEADS * DH), BF16),
                   jax.ShapeDtypeStruct((NSA_GROUPS, seq, nsel), BF16)],
        compiler_params=_params(("arbitrary",)),
        name="cmp_attn_topk",
    )(main, kc, vc, mt)


ATT_TQ = 512
ATT_TK = 512
WIN_SPAN = WINDOW + ATT_TQ // 2


def _nsa_attn_kernel(q_ref, mask_ref, ks_ref, vs_ref, kw_ref, vw_ref, oc_ref, gate_ref, o_ref,
                     qa_scr, acc_scr, m_scr):
    i = pl.program_id(1)
    qbase = i * ATT_TQ
    assert mask_ref.shape[2] == LANES, "the block-selection bias rides in one extra 128-wide contraction slab"
    bias = mask_ref[0]
    for r in range(NSA_REP):
        qa_scr[r] = jnp.concatenate([q_ref[:, r * DH:(r + 1) * DH], bias], axis=1)
    m_scr[...] = jnp.full(m_scr.shape, NEG_INF, F32)
    acc_scr[...] = jnp.zeros(acc_scr.shape, F32)
    krow = lax.broadcasted_iota(jnp.int32, (ATT_TK, LANES), 0)
    kblk = lax.broadcasted_iota(jnp.int32, (ATT_TK, LANES), 1)

    half = ATT_TQ // 2

    def key_tile(j):
        k0 = pl.multiple_of(j * ATT_TK, ATT_TK)
        onehot = jnp.where(((krow + k0) >> 6) == kblk, 1.0, 0.0).astype(BF16)
        k_aug = jnp.concatenate([ks_ref[pl.ds(k0, ATT_TK), :], onehot], axis=1)
        v_aug = jnp.concatenate([vs_ref[pl.ds(k0, ATT_TK), :], jnp.ones((ATT_TK, LANES), BF16)], axis=1)
        return k_aug, v_aug

    def online_update(r, rows, s, v_aug):
        m_prev = m_scr[r, rows, :]
        m_new = jnp.maximum(m_prev, jnp.max(s, axis=-1, keepdims=True))
        alpha = jnp.exp2(m_prev - m_new)
        p = jnp.exp2(s - jnp.concatenate([m_new] * (s.shape[1] // LANES), axis=1))
        acc_scr[r, rows, :] = (acc_scr[r, rows, :] * jnp.concatenate([alpha, alpha], axis=1)
                               + _dot(p.astype(BF16), v_aug))
        m_scr[r, rows, :] = m_new

    def step(j, diagonal):
        k_aug, v_aug = key_tile(j)
        if not diagonal:
            for r in range(NSA_REP):
                online_update(r, slice(0, ATT_TQ), _dot_nt(qa_scr[r], k_aug), v_aug)
            return
        r_ = lax.broadcasted_iota(jnp.int32, (half, half), 0)
        c_ = lax.broadcasted_iota(jnp.int32, (half, half), 1)
        tri_bias = jnp.where(c_ <= r_, 0.0, MASK_NEG)
        for r in range(NSA_REP):
            s0 = _dot_nt(qa_scr[r, 0:half, :], k_aug[:half]) + tri_bias
            online_update(r, slice(0, half), s0, v_aug[:half])
            s1 = _dot_nt(qa_scr[r, half:ATT_TQ, :], k_aug)
            s1 = jnp.concatenate([s1[:, :half], s1[:, half:] + tri_bias], axis=1)
            online_update(r, slice(half, ATT_TQ), s1, v_aug)

    def body(jj, carry):
        for u in range(4):
            step(4 * jj + u, False)
        return carry

    lax.fori_loop(0, i // 4, body, 0)
    done = (i // 4) * 4

    @pl.when((i & 2) != 0)
    def _():
        step(done, False)
        step(done + 1, False)

    @pl.when((i & 1) != 0)
    def _():
        step(i - 1, False)

    step(i, True)

    first_gate = SSM_HEADS + 3 * NSA_REP * pl.program_id(0)
    gates = jax.nn.sigmoid(pltpu.roll(gate_ref[...], SMALL_W - first_gate, 1))
    r_ = lax.broadcasted_iota(jnp.int32, (half, WIN_SPAN), 0)
    c_ = lax.broadcasted_iota(jnp.int32, (half, WIN_SPAN), 1)
    for hh in range(2):
        rows = slice(hh * half, (hh + 1) * half)
        q0 = qbase + hh * half
        start = pl.multiple_of(jnp.maximum(q0 - WINDOW, 0), half)
        kw = kw_ref[pl.ds(start, WIN_SPAN), :]
        vw_aug = jnp.concatenate([vw_ref[pl.ds(start, WIN_SPAN), :], jnp.ones((WIN_SPAN, LANES), BF16)], axis=1)
        diff = (q0 + r_) - (start + c_)
        win_bias = jnp.where((diff >= 0) & (diff < WINDOW), 0.0, MASK_NEG)
        for r in range(NSA_REP):
            s = _dot_nt(q_ref[rows, r * DH:(r + 1) * DH], kw) + win_bias
            p = jnp.exp2(s - jnp.max(s, axis=-1, keepdims=True))
            ow = _dot(p.astype(BF16), vw_aug)
            o_win = ow[:, :DH] / jnp.maximum(ow[:, DH:], TINY)
            acc = acc_scr[r, rows, :]
            o_sel = acc[:, :DH] / jnp.maximum(acc[:, DH:], TINY)
            g_r = gates[rows]
            o = (g_r[:, 3 * r:3 * r + 1] * oc_ref[rows, r * DH:(r + 1) * DH].astype(F32)
                 + g_r[:, 3 * r + 1:3 * r + 2] * o_sel
                 + g_r[:, 3 * r + 2:3 * r + 3] * o_win)
            o_ref[rows, r * DH:(r + 1) * DH] = o.astype(o_ref.dtype)


def _nsa_attn(main, small, mask, o_cmp, seq):
    gw = NSA_REP * DH
    nsel = mask.shape[2]
    kv_spec = lambda col: pl.BlockSpec((seq, DH), lambda g, i: (0, col // DH + g))
    return pl.pallas_call(
        _nsa_attn_kernel,
        grid=(NSA_GROUPS, seq // ATT_TQ),
        in_specs=[pl.BlockSpec((ATT_TQ, gw), lambda g, i: (i, COL_Q // gw + g)),
                  pl.BlockSpec((1, ATT_TQ, nsel), lambda g, i: (g, i, 0)),
                  kv_spec(COL_KS), kv_spec(COL_VS), kv_spec(COL_KW), kv_spec(COL_VW),
                  pl.BlockSpec((ATT_TQ, gw), lambda g, i: (i, g)),
                  pl.BlockSpec((ATT_TQ, SMALL_W), lambda g, i: (i, 0))],
        out_specs=pl.BlockSpec((ATT_TQ, gw), lambda g, i: (i, g)),
        out_shape=jax.ShapeDtypeStruct((seq, NSA_HEADS * DH), BF16),
        scratch_shapes=[pltpu.VMEM((NSA_REP, ATT_TQ, 2 * DH), BF16),
                        pltpu.VMEM((NSA_REP, ATT_TQ, 2 * DH), F32),
                        pltpu.VMEM((NSA_REP, ATT_TQ, LANES), F32)],
        compiler_params=_params(("arbitrary", "arbitrary")),
        name="nsa_sel_win_attn",
    )(main, mask, main, main, main, main, o_cmp, small)


HALO = 8
GROUP_W = SSM_D_INNER // SSM_GROUPS
HEADS_PER_GROUP = SSM_HEADS // SSM_GROUPS


def _silu(x):
    h = 0.5 * x
    return h + h * jnp.tanh(h)


def _mamba_kernel(z_ref, xs_ref, b_ref, c_ref, dt_ref,
                  cwx_ref, cwb_ref, cwc_ref, cbx_ref, cbb_ref, cbc_ref,
                  dtb_ref, alog_ref, dfull_ref, nw_ref, expand2_ref, tril3_ref,
                  o_ref, state_scr, xs_ext, b_ext, c_ext):
    c = pl.program_id(0)
    L = SSD_CHUNK

    @pl.when(c == 0)
    def _():
        state_scr[...] = jnp.zeros(state_scr.shape, F32)
        for carry in (xs_ext, b_ext, c_ext):
            carry[...] = jnp.zeros(carry.shape, F32)

    def conv_silu(x_ref, carry, w_ref, bias_ref):
        x = x_ref[...].astype(F32)
        first_row = lax.broadcasted_iota(jnp.int32, (HALO, x.shape[1]), 0) == 0
        r = None
        for k in range(CONV_WIDTH - 1):
            t = w_ref[k:k + 1, :] * x
            if r is not None:
                t = t + r
            rolled = pltpu.roll(t, 1, 0)
            top = jnp.where(first_row, carry[k:k + 1, :], rolled[0:HALO])
            carry[k:k + 1, :] = t[L - 1:L, :]
            r = jnp.concatenate([top, rolled[HALO:]], axis=0)
        return _silu(r + w_ref[CONV_WIDTH - 1:CONV_WIDTH, :] * x + bias_ref[...])

    xs = conv_silu(xs_ref, xs_ext, cwx_ref, cbx_ref)
    bm = conv_silu(b_ref, b_ext, cwb_ref, cbb_ref).astype(BF16)
    cm = conv_silu(c_ref, c_ext, cwc_ref, cbc_ref).astype(BF16)

    xdt_raw = dt_ref[...] + dtb_ref[...]
    dt = jnp.maximum(xdt_raw, 0.0) + jnp.log1p(jnp.exp(-jnp.abs(xdt_raw)))
    adt = dt * (-jnp.exp(alog_ref[...]))
    acs = _dot(tril3_ref[...], jnp.concatenate(_split3(adt), axis=0)) * LOG2E
    acs_t = acs.T
    a_last = acs[L - 1:L, :]
    stacked = jnp.concatenate([dt, jnp.exp2(acs), jnp.exp2(a_last - acs)], axis=0)
    wide = _dot(jnp.concatenate(_split2(stacked), axis=1), expand2_ref[...])
    dt_w = wide[0:L]
    in_decay_w = wide[L:2 * L]
    out_decay_w = wide[2 * L:3 * L]

    xdt = xs * dt_w
    xst = xdt * out_decay_w
    causal = lax.broadcasted_iota(jnp.int32, (L, L), 0) >= lax.broadcasted_iota(jnp.int32, (L, L), 1)
    causal_bias = jnp.where(causal, 0.0, MASK_NEG)
    lane = lax.broadcasted_iota(jnp.int32, (L, LANES), 1)
    low_half = lane < SSM_HEAD_DIM
    chunk_decay = jnp.exp2(acs_t[:, L - 1:L])

    for g in range(SSM_GROUPS):
        gs = slice(g * GROUP_W, (g + 1) * GROUP_W)
        bg = bm[:, g * SSM_STATE:(g + 1) * SSM_STATE]
        cg = cm[:, g * SSM_STATE:(g + 1) * SSM_STATE]
        cb = _dot_nt(cg, bg)
        prev = state_scr[gs, :]
        y_off = _dot_nt(cg, prev.astype(BF16)) * in_decay_w[:, gs]
        pieces = []
        for hp in range(HEADS_PER_GROUP // 2):
            lhs = []
            for e in range(2):
                h = g * HEADS_PER_GROUP + 2 * hp + e
                decay = jnp.exp2(acs[:, h:h + 1] - acs_t[h:h + 1, :] + causal_bias)
                lhs.append((cb * decay).astype(BF16))
            slab = xdt[:, g * GROUP_W + hp * LANES:g * GROUP_W + (hp + 1) * LANES]
            rhs = jnp.concatenate([jnp.where(low_half, slab, 0.0), jnp.where(low_half, 0.0, slab)], axis=0)
            pieces.append(_dot(jnp.concatenate(lhs, axis=1), rhs.astype(BF16)))
        y_g = jnp.concatenate(pieces, axis=1) + y_off
        y_g = y_g + dfull_ref[:, gs] * xs[:, gs]
        y_g = y_g * _silu(z_ref[:, gs].astype(F32))
        ms = jnp.mean(y_g * y_g, axis=-1, keepdims=True)
        o_ref[:, gs] = (y_g * lax.rsqrt(ms + EPS) * nw_ref[:, gs]).astype(o_ref.dtype)
        new = _dot(xst[:, gs].T.astype(BF16), bg)
        cd = jnp.broadcast_to(chunk_decay[g * HEADS_PER_GROUP:(g + 1) * HEADS_PER_GROUP], (HEADS_PER_GROUP, LANES))
        cd = jnp.broadcast_to(cd[:, None, :], (HEADS_PER_GROUP, SSM_HEAD_DIM, LANES)).reshape(GROUP_W, LANES)
        state_scr[gs, :] = prev * cd + new


def _mamba(main, small, cwx, cwb, cwc, cbx, cbb, cbc, dtb, alog, dfull, nw, expand2, tril3, seq):
    L = SSD_CHUNK
    bw = SSM_GROUPS * SSM_STATE
    full = lambda a: pl.BlockSpec(a.shape, lambda c: (0,) * a.ndim)
    consts = [cwx, cwb, cwc, cbx, cbb, cbc, dtb, alog, dfull, nw, expand2, tril3]
    return pl.pallas_call(
        _mamba_kernel,
        grid=(seq // L,),
        in_specs=[pl.BlockSpec((L, SSM_D_INNER), lambda c: (c, COL_Z // SSM_D_INNER)),
                  pl.BlockSpec((L, SSM_D_INNER), lambda c: (c, COL_XS // SSM_D_INNER)),
                  pl.BlockSpec((L, bw), lambda c: (c, COL_B // bw)),
                  pl.BlockSpec((L, bw), lambda c: (c, COL_C // bw)),
                  pl.BlockSpec((L, LANES), lambda c: (c, 0))] + [full(a) for a in consts],
        out_specs=pl.BlockSpec((L, SSM_D_INNER), lambda c: (c, 0)),
        out_shape=jax.ShapeDtypeStruct((seq, SSM_D_INNER), BF16),
        scratch_shapes=[pltpu.VMEM((SSM_D_INNER, SSM_STATE), F32),
                        pltpu.VMEM((HALO, SSM_D_INNER), F32),
                        pltpu.VMEM((HALO, bw), F32),
                        pltpu.VMEM((HALO, bw), F32)],
        compiler_params=_params(("arbitrary",)),
        name="mamba2_ssd",
    )(main, main, main, main, small, *consts)


def _merge_kernel(a_ref, b_ref, pa_ref, pb_ref, ga_ref, gb_ref, o_ref):
    ya = _dot(a_ref[...], pa_ref[...])
    yb = _dot(b_ref[...], pb_ref[...])
    o = jax.nn.sigmoid(ga_ref[...].astype(F32)) * ya + jax.nn.sigmoid(gb_ref[...].astype(F32)) * yb
    o_ref[...] = o.astype(o_ref.dtype)


def _merge(o_nsa, o_ssm, pa, pb, main, seq):
    tm = min(seq, 512)
    tn = 1024
    return pl.pallas_call(
        _merge_kernel,
        grid=(seq // tm, D_MODEL // tn),
        in_specs=[pl.BlockSpec((tm, NSA_HEADS * DH), lambda i, j: (i, 0)),
                  pl.BlockSpec((tm, SSM_D_INNER), lambda i, j: (i, 0)),
                  pl.BlockSpec((NSA_HEADS * DH, tn), lambda i, j: (0, j)),
                  pl.BlockSpec((SSM_D_INNER, tn), lambda i, j: (0, j)),
                  pl.BlockSpec((tm, tn), lambda i, j: (i, COL_GA // tn + j)),
                  pl.BlockSpec((tm, tn), lambda i, j: (i, COL_GB // tn + j))],
        out_specs=pl.BlockSpec((tm, tn), lambda i, j: (i, j)),
        out_shape=jax.ShapeDtypeStruct((seq, D_MODEL), BF16),
        compiler_params=_params(("arbitrary", "arbitrary")),
        name="gated_merge",
    )(o_nsa, o_ssm, pa, pb, main, main)


def _outproj_kernel(m_ref, w_ref, x_ref, nw_ref, h_ref, hn_ref):
    h = x_ref[...] + _dot(m_ref[...], w_ref[...])
    h_ref[...] = h
    ms = jnp.mean(h * h, axis=-1, keepdims=True)
    hn_ref[...] = (h * lax.rsqrt(ms + EPS) * nw_ref[...]).astype(hn_ref.dtype)


def _outproj(merged, w_out, x2, norm_w, seq):
    tm = min(seq, 512)
    row = pl.BlockSpec((tm, D_MODEL), lambda i: (i, 0))
    return pl.pallas_call(
        _outproj_kernel,
        grid=(seq // tm,),
        in_specs=[row, pl.BlockSpec((D_MODEL, D_MODEL), lambda i: (0, 0), pipeline_mode=pl.Buffered(1)), row,
                  pl.BlockSpec((1, D_MODEL), lambda i: (0, 0))],
        out_specs=[row, row],
        out_shape=[jax.ShapeDtypeStruct((seq, D_MODEL), F32), jax.ShapeDtypeStruct((seq, D_MODEL), BF16)],
        compiler_params=_params(("arbitrary",)),
        name="outproj_residual_norm",
    )(merged, w_out, x2, norm_w.reshape(1, D_MODEL))


def _mlp_kernel(hn_ref, wu_ref, wd_ref, h_ref, nw_ref, o_ref, acc_scr):
    f = pl.program_id(1)

    @pl.when(f == 0)
    def _():
        acc_scr[...] = jnp.zeros(acc_scr.shape, F32)

    up = jnp.maximum(_dot(hn_ref[...], wu_ref[...]), 0.0)
    acc_scr[...] += _dot((up * up).astype(BF16), wd_ref[...])

    @pl.when(f == pl.num_programs(1) - 1)
    def _():
        h = h_ref[...] + acc_scr[...]
        ms = jnp.mean(h * h, axis=-1, keepdims=True)
        o_ref[...] = h * lax.rsqrt(ms + EPS) * nw_ref[...]


def _mlp(hn, w_up, w_down, h1, norm_w, seq):
    tm = min(seq, 512)
    tf = 1024
    row = lambda i, f: (i, 0)
    return pl.pallas_call(
        _mlp_kernel,
        grid=(seq // tm, D_FF // tf),
        in_specs=[pl.BlockSpec((tm, D_MODEL), row),
                  pl.BlockSpec((D_MODEL, tf), lambda i, f: (0, f)),
                  pl.BlockSpec((tf, D_MODEL), lambda i, f: (f, 0)),
                  pl.BlockSpec((tm, D_MODEL), row),
                  pl.BlockSpec((1, D_MODEL), lambda i, f: (0, 0))],
        out_specs=pl.BlockSpec((tm, D_MODEL), row),
        out_shape=jax.ShapeDtypeStruct((seq, D_MODEL), F32),
        scratch_shapes=[pltpu.VMEM((tm, D_MODEL), F32)],
        compiler_params=_params(("arbitrary", "arbitrary")),
        name="mlp_final_norm",
    )(hn, w_up, w_down, h1, norm_w.reshape(1, D_MODEL))


def _cmp_to_sel(seq):
    n_cmp = (seq - CMP_BLOCK) // CMP_STRIDE + 1
    n_sel = seq // SEL_BLOCK
    c_start = np.arange(n_cmp) * CMP_STRIDE
    s_start = np.arange(n_sel) * SEL_BLOCK
    overlap = np.clip(np.minimum(c_start[:, None] + CMP_BLOCK, s_start[None, :] + SEL_BLOCK)
                      - np.maximum(c_start[:, None], s_start[None, :]), 0, None)
    m = np.zeros((seq // CMP_STRIDE, LANES), np.float32)
    m[:n_cmp, :n_sel] = overlap / CMP_STRIDE
    return jnp.asarray(m.T, BF16)


def kernel(x, positions, norm_mix_w, w_in, cmp_pos_k, cmp_pos_v, cmp_k_w1, cmp_k_w2, cmp_v_w1, cmp_v_w2, conv_w, conv_b, dt_bias, a_log, ssm_d, ssm_norm_w, w_proj_nsa, w_proj_ssm, w_out, norm_mlp_w, w_up, w_down, norm_final_w):
    bsz, seq, _ = x.shape
    assert bsz == 1 and seq % 1024 == 0 and SEL_TOPK <= seq // SEL_BLOCK <= LANES
    assert w_in.shape[0] == 1, "one layer"
    x2 = x.reshape(seq, D_MODEL)

    o_gate = NSA_HEADS * DH + 6 * NSA_GROUPS * DH
    o_z = o_gate + 3 * NSA_HEADS
    o_dt = o_z + 2 * SSM_D_INNER + 2 * SSM_GROUPS * SSM_STATE
    o_bg = o_dt + SSM_HEADS
    assert o_z == SRC_Z and o_bg == SRC_BG and o_gate == ATT_W
    w_t = jnp.swapaxes(w_in, 1, 2)[0]
    n_gate = 3 * NSA_HEADS
    w_small_t = jnp.concatenate([w_t[o_dt:o_dt + SSM_HEADS], w_t[o_gate:o_gate + n_gate],
                                 jnp.zeros((SMALL_W - SSM_HEADS - n_gate, D_MODEL), F32)], axis=0)

    pos = positions.reshape(seq)
    cosf, sinf = _rope_tables(pos)
    att, kc_chunks, vc_chunks, u = _inproj(x2, norm_mix_w[0], w_t, cosf, sinf)
    ssm = _proj_ssm(u, w_t)
    small = _proj_small(u, w_small_t)

    ncp = seq // CMP_STRIDE
    def pos2(p):
        p = p.reshape(2, CMP_STRIDE * DH)
        return jnp.concatenate([p, jnp.zeros((14, CMP_STRIDE * DH), F32)], axis=0).astype(BF16)
    pos_end = pos[CMP_BLOCK - 1::CMP_STRIDE]
    cos_c, sin_c = _rope_tables(jnp.concatenate([pos_end, jnp.zeros((ncp - pos_end.shape[0],), pos.dtype)]))
    kc, vc = _compress(kc_chunks, vc_chunks, pos2(cmp_pos_k[0]), pos2(cmp_pos_v[0]),
                       cmp_k_w1[0].astype(BF16), cmp_k_w2[0].astype(BF16),
                       cmp_v_w1[0].astype(BF16), cmp_v_w2[0].astype(BF16), cos_c, sin_c)

    o_cmp, sel_bias = _cmp_attn(att, kc, vc, _cmp_to_sel(seq), seq)
    o_nsa = _nsa_attn(att, small, sel_bias, o_cmp, seq)

    cw = conv_w[0]
    cb_ = conv_b[0].reshape(1, -1)
    lane_pad = lambda a: jnp.concatenate([a.reshape(1, -1), jnp.zeros((1, LANES - a.shape[-1]), F32)], axis=1)
    head_of = np.arange(SSM_D_INNER) // SSM_HEAD_DIM
    expand = np.arange(LANES)[:, None] == head_of[None, :]
    expand2 = jnp.asarray(np.concatenate([expand, expand], axis=0), BF16)
    tril = np.tril(np.ones((SSD_CHUNK, SSD_CHUNK), np.float32))
    tril3 = jnp.asarray(np.concatenate([tril, tril, tril], axis=1), BF16)
    o_ssm = _mamba(ssm, small, cw[:, :4096], cw[:, 4096:5120], cw[:, 5120:],
                   cb_[:, :4096], cb_[:, 4096:5120], cb_[:, 5120:],
                   lane_pad(dt_bias[0]), lane_pad(a_log[0]),
                   jnp.repeat(ssm_d[0], SSM_HEAD_DIM).reshape(1, -1), ssm_norm_w[0].reshape(1, -1),
                   expand2, tril3, seq)

    merged = _merge(o_nsa, o_ssm, w_proj_nsa[0].astype(BF16), w_proj_ssm[0].astype(BF16), ssm, seq)
    h1, hn = _outproj(merged, w_out[0].astype(BF16), x2, norm_mlp_w[0], seq)
    out = _mlp(hn, w_up[0].astype(BF16), w_down[0].astype(BF16), h1, norm_final_w, seq)
    return out.reshape(bsz, seq, D_MODEL)
```

```python
import math

import jax
import jax.numpy as jnp
import numpy as np
from jax import lax
from jax.experimental import pallas as pl
from jax.experimental.pallas import tpu as pltpu

F32 = jnp.float32
BF16 = jnp.bfloat16

D_MODEL = 2048
NSA_HEADS = 16
NSA_GROUPS = 4
NSA_REP = NSA_HEADS // NSA_GROUPS
DH = 128
CMP_BLOCK = 32
CMP_STRIDE = 16
SEL_BLOCK = 64
SEL_TOPK = 16
WINDOW = 512
ROPE_THETA = 10000.0
N_FORCED = 3
SSM_D_INNER = 4096
SSM_HEAD_DIM = 64
SSM_HEADS = 64
SSM_GROUPS = 8
SSM_STATE = 128
CONV_WIDTH = 4
SSD_CHUNK = 128
D_FF = 4 * D_MODEL
EPS = 1e-6
NEG_INF = -1e30
TINY = 1e-30
MASK_NEG = -1e9
TAKEN = -2.0
LOG2E = math.log2(math.e)
LANES = 128
VMEM_LIMIT = 56 * 1024 * 1024

COL_Q = 0
COL_KC = 2048
COL_VC = 2560
COL_KS = 3072
COL_VS = 3584
COL_KW = 4096
COL_VW = 4608
ATT_W = 5120
COL_Z = 0
COL_XS = 4096
COL_B = 8192
COL_C = 9216
COL_GA = 10240
COL_GB = 12288
SSM_W = 14336
SMALL_W = 128


def _dot(a, b):
    return jnp.dot(a, b, preferred_element_type=F32)


def _dot_nt(a, b):
    return lax.dot_general(a, b, (((1,), (1,)), ((), ())), preferred_element_type=F32)


def _dot_tn(a, b):
    return lax.dot_general(a, b, (((0,), (0,)), ((), ())), preferred_element_type=F32)


def _params(sem):
    return pltpu.CompilerParams(dimension_semantics=sem, vmem_limit_bytes=VMEM_LIMIT)


def _rope(x, cos, sin_signed):
    return x * cos + pltpu.roll(x, DH // 2, 1) * sin_signed


def _split2(x):
    hi = x.astype(BF16)
    lo = (x - hi.astype(F32)).astype(BF16)
    return hi, lo


def _split3(x):
    hi = x.astype(BF16)
    r = x - hi.astype(F32)
    mid = r.astype(BF16)
    lo = (r - mid.astype(F32)).astype(BF16)
    return hi, mid, lo


def _rope_table_kernel(pos_ref, inv_ref, sign_ref, cos_ref, sin_ref):
    ang = pos_ref[...].astype(F32) * inv_ref[...]
    cos_ref[...] = jnp.cos(ang)
    sin_ref[...] = jnp.sin(ang) * sign_ref[...]


def _rope_tables(positions):
    seq = positions.shape[0]
    half = DH // 2
    inv = ROPE_THETA ** (-jnp.arange(0, DH, 2, dtype=F32) / DH)
    inv_full = jnp.concatenate([inv, inv]).reshape(1, DH)
    sign = jnp.concatenate([-jnp.ones((half,), F32), jnp.ones((half,), F32)]).reshape(1, DH)
    tm = math.gcd(seq, 1024)
    return pl.pallas_call(
        _rope_table_kernel,
        grid=(seq // tm,),
        in_specs=[pl.BlockSpec((tm, 1), lambda i: (i, 0)),
                  pl.BlockSpec((1, DH), lambda i: (0, 0)),
                  pl.BlockSpec((1, DH), lambda i: (0, 0))],
        out_specs=[pl.BlockSpec((tm, DH), lambda i: (i, 0))] * 2,
        out_shape=[jax.ShapeDtypeStruct((seq, DH), F32)] * 2,
        compiler_params=_params(("arbitrary",)),
        name="rope_tables",
    )(positions.reshape(seq, 1), inv_full, sign)


INPROJ_TN = 512
Q_TILE_LO = COL_Q // INPROJ_TN
Q_TILE_HI = COL_KC // INPROJ_TN
KC_TILE = COL_KC // INPROJ_TN
VC_TILE = COL_VC // INPROJ_TN
KS_TILE = COL_KS // INPROJ_TN
KW_TILE = COL_KW // INPROJ_TN


def _inproj_kernel(x_ref, nw_ref, w_ref, cos_ref, sin_ref, o_ref, kcx_ref, vcx_ref, u_ref, acc_scr):
    j = pl.program_id(1)

    @pl.when(j == 0)
    def _():
        xf = x_ref[...]
        ms = jnp.mean(xf * xf, axis=-1, keepdims=True)
        u_ref[...] = (xf * lax.rsqrt(ms + EPS) * nw_ref[...]).astype(BF16)

    acc = _dot_nt(u_ref[...], w_ref[...].astype(BF16))
    is_q = jnp.logical_and(j >= Q_TILE_LO, j < Q_TILE_HI)
    is_rope = is_q | (j == KS_TILE) | (j == KW_TILE)

    @pl.when(jnp.logical_not(is_rope))
    def _():
        o_ref[...] = acc.astype(o_ref.dtype)

    @pl.when(is_rope)
    def _():
        cos = cos_ref[...]
        sin = sin_ref[...]
        scale = jnp.where(is_q, DH ** -0.5 * LOG2E, 1.0).astype(F32)
        for h in range(INPROJ_TN // DH):
            xh = acc[:, h * DH:(h + 1) * DH]
            o_ref[:, h * DH:(h + 1) * DH] = (_rope(xh, cos, sin) * scale).astype(o_ref.dtype)

    def chunked(dst_ref):
        n_chunks = acc_scr.shape[1] // CMP_STRIDE
        for g in range(NSA_GROUPS):
            acc_scr[g] = acc[:, g * DH:(g + 1) * DH]
            for l in range(CMP_STRIDE):
                rows_l = acc_scr[g, pl.ds(l, n_chunks, stride=CMP_STRIDE), :]
                dst_ref[g, :, l * DH:(l + 1) * DH] = rows_l.astype(dst_ref.dtype)

    @pl.when(j == KC_TILE)
    def _():
        chunked(kcx_ref)

    @pl.when(j == VC_TILE)
    def _():
        chunked(vcx_ref)


def _inproj(x2, norm_w, w_t, cosf, sinf):
    seq = x2.shape[0]
    tm = min(seq, 1024)
    tn = INPROJ_TN
    chunk_w = CMP_STRIDE * DH
    chunk_spec = pl.BlockSpec((NSA_GROUPS, tm // CMP_STRIDE, chunk_w), lambda i, j: (0, i, 0))
    chunk_shape = jax.ShapeDtypeStruct((NSA_GROUPS, seq // CMP_STRIDE, chunk_w), BF16)
    return pl.pallas_call(
        _inproj_kernel,
        grid=(seq // tm, ATT_W // tn),
        in_specs=[pl.BlockSpec((tm, D_MODEL), lambda i, j: (i, 0)),
                  pl.BlockSpec((1, D_MODEL), lambda i, j: (0, 0)),
                  pl.BlockSpec((tn, D_MODEL), lambda i, j: (j, 0)),
                  pl.BlockSpec((tm, DH), lambda i, j: (i, 0)),
                  pl.BlockSpec((tm, DH), lambda i, j: (i, 0))],
        out_specs=[pl.BlockSpec((tm, tn), lambda i, j: (i, j)),
                   chunk_spec, chunk_spec,
                   pl.BlockSpec((tm, D_MODEL), lambda i, j: (i, 0))],
        out_shape=[jax.ShapeDtypeStruct((seq, ATT_W), BF16),
                   chunk_shape, chunk_shape,
                   jax.ShapeDtypeStruct((seq, D_MODEL), BF16)],
        scratch_shapes=[pltpu.VMEM((NSA_GROUPS, tm, DH), F32)],
        compiler_params=_params(("arbitrary", "arbitrary")),
        name="inproj_attn",
    )(x2, norm_w.reshape(1, D_MODEL), w_t, cosf, sinf)


SSM_TN = 1024
SRC_Z = NSA_HEADS * DH + 6 * NSA_GROUPS * DH + 3 * NSA_HEADS
SRC_BG = SRC_Z + 2 * SSM_D_INNER + 2 * SSM_GROUPS * SSM_STATE + SSM_HEADS
SSM_SPLIT = COL_GA // SSM_TN


def _nt_matmul_kernel(a_ref, w_ref, o_ref):
    o_ref[...] = _dot_nt(a_ref[...], w_ref[...].astype(BF16)).astype(o_ref.dtype)


def _proj_ssm(u, w_t):
    seq, k = u.shape
    tm = min(seq, 2048)

    def src_rows(i, j):
        jump = (SRC_BG - SRC_Z - SSM_TN * SSM_SPLIT) // 16
        return ((SRC_Z // 16 + (SSM_TN // 16) * j + jump * (j // SSM_SPLIT)) * 16, 0)

    return pl.pallas_call(
        _nt_matmul_kernel,
        grid=(seq // tm, SSM_W // SSM_TN),
        in_specs=[pl.BlockSpec((tm, k), lambda i, j: (i, 0)),
                  pl.BlockSpec((pl.Element(SSM_TN), pl.Element(k)), src_rows)],
        out_specs=pl.BlockSpec((tm, SSM_TN), lambda i, j: (i, j)),
        out_shape=jax.ShapeDtypeStruct((seq, SSM_W), BF16),
        compiler_params=_params(("arbitrary", "arbitrary")),
        name="inproj_ssm",
    )(u, w_t)


def _proj_small(u, w_small_t):
    seq, k = u.shape
    n = w_small_t.shape[0]
    tm = min(seq, 1024)
    return pl.pallas_call(
        _nt_matmul_kernel,
        grid=(seq // tm,),
        in_specs=[pl.BlockSpec((tm, k), lambda i: (i, 0)),
                  pl.BlockSpec((n, k), lambda i: (0, 0))],
        out_specs=pl.BlockSpec((tm, n), lambda i: (i, 0)),
        out_shape=jax.ShapeDtypeStruct((seq, n), F32),
        compiler_params=_params(("arbitrary",)),
        name="inproj_small",
    )(u, w_small_t)


def _compress_kernel(xk_ref, xv_ref, pk_ref, pv_ref, w1k_ref, w2k_ref, w1v_ref, w2v_ref,
                     cos_ref, sin_ref, kc_ref, vc_ref):
    half = CMP_STRIDE * DH
    ncp = xk_ref.shape[1]

    def comp(x_ref, p_ref, w1_ref, w2_ref):
        x = x_ref[0]
        wt = w1_ref[0:half, :]
        wb = w1_ref[half:2 * half, :]
        a = _dot(x, wt)
        b = _dot(x, wb)
        pb = _dot(p_ref[...], wt)[0:1] + _dot(p_ref[...], wb)[1:2]
        h = a + pltpu.roll(b, ncp - 1, 0) + pb
        h = h * jax.nn.sigmoid(h)
        return _dot(h.astype(BF16), w2_ref[...])

    kc = comp(xk_ref, pk_ref, w1k_ref, w2k_ref)
    kc_ref[0] = _rope(kc, cos_ref[...], sin_ref[...]).astype(BF16)
    vc_ref[0] = comp(xv_ref, pv_ref, w1v_ref, w2v_ref).astype(BF16)


def _compress(xk, xv, pk, pv, w1k, w2k, w1v, w2v, cos_c, sin_c):
    g, ncp, width = xk.shape
    full = lambda shape: pl.BlockSpec(shape, lambda i: (0,) * len(shape))
    per_g = pl.BlockSpec((1, ncp, width), lambda i: (i, 0, 0))
    out_g = pl.BlockSpec((1, ncp, DH), lambda i: (i, 0, 0))
    return pl.pallas_call(
        _compress_kernel,
        grid=(g,),
        in_specs=[per_g, per_g, full(pk.shape), full(pv.shape), full(w1k.shape), full(w2k.shape),
                  full(w1v.shape), full(w2v.shape), full(cos_c.shape), full(sin_c.shape)],
        out_specs=[out_g, out_g],
        out_shape=[jax.ShapeDtypeStruct((g, ncp, DH), BF16)] * 2,
        compiler_params=_params(("arbitrary",)),
        name="compress",
    )(xk, xv, pk, pv, w1k, w2k, w1v, w2v, cos_c, sin_c)


CMP_TQ = 256


def _cmp_attn_kernel(q_ref, kc_ref, vc_ref, mt_ref, oc_ref, mask_ref):
    i = pl.program_id(0)
    qbase = i * CMP_TQ
    ncp = kc_ref.shape[1]
    nsel = mt_ref.shape[0]
    row = lax.broadcasted_iota(jnp.int32, (CMP_TQ, ncp), 0)
    col = lax.broadcasted_iota(jnp.int32, (CMP_TQ, ncp), 1)
    bias1 = jnp.where((col * CMP_STRIDE + (CMP_BLOCK - 1)) <= qbase + row, 0.0, MASK_NEG)
    bias = jnp.concatenate([bias1] * NSA_REP, axis=0)
    has_key = jnp.concatenate([bias1[:, 0:1] == 0.0] * NSA_REP, axis=0)
    blk = lax.broadcasted_iota(jnp.int32, (nsel, CMP_TQ), 0)
    cur = (qbase + lax.broadcasted_iota(jnp.int32, (nsel, CMP_TQ), 1)) >> 6
    future = blk > cur
    forced = (blk == 0) | (blk == cur) | (blk == cur - 1)
    blk_f = blk.astype(F32)
    mt = mt_ref[...]
    importance = []
    for g in range(NSA_GROUPS):
        qg = jnp.concatenate(
            [q_ref[:, (g * NSA_REP + r) * DH:(g * NSA_REP + r + 1) * DH] for r in range(NSA_REP)], axis=0)
        s = _dot_nt(qg, kc_ref[g]) + bias
        p = jnp.exp2(s - jnp.max(s, axis=-1, keepdims=True)).astype(BF16)
        ov = _dot(p, jnp.concatenate([vc_ref[g], jnp.ones((ncp, LANES), BF16)], axis=1))
        inv = jnp.where(has_key, 1.0 / ov[:, DH:], 0.0)
        o = ov[:, :DH] * inv
        imp = None
        for r in range(NSA_REP):
            h = g * NSA_REP + r
            rows_r = slice(r * CMP_TQ, (r + 1) * CMP_TQ)
            oc_ref[:, h * DH:(h + 1) * DH] = o[rows_r].astype(oc_ref.dtype)
            part = _dot_nt(p[rows_r], mt) * inv[rows_r]
            imp = part if imp is None else imp + part
        importance.append(imp.T)
    v = jnp.where(future | forced, -1.0, jnp.stack(importance))
    for _ in range(SEL_TOPK - N_FORCED):
        mx = jnp.max(v, axis=1, keepdims=True)
        first = jnp.min(jnp.where(v == mx, blk_f, float(nsel)), axis=1, keepdims=True)
        v = jnp.where(blk_f == first, TAKEN, v)
    bias_t = jnp.where(forced | (v == TAKEN), 0.0, MASK_NEG)
    for g in range(NSA_GROUPS):
        mask_ref[g] = bias_t[g].T.astype(mask_ref.dtype)


def _cmp_attn(main, kc, vc, mt, seq):
    nsel = mt.shape[0]
    ncp = kc.shape[1]
    return pl.pallas_call(
        _cmp_attn_kernel,
        grid=(seq // CMP_TQ,),
        in_specs=[pl.BlockSpec((CMP_TQ, NSA_HEADS * DH), lambda i: (i, COL_Q // (NSA_HEADS * DH))),
                  pl.BlockSpec((NSA_GROUPS, ncp, DH), lambda i: (0, 0, 0)),
                  pl.BlockSpec((NSA_GROUPS, ncp, DH), lambda i: (0, 0, 0)),
                  pl.BlockSpec((nsel, ncp), lambda i: (0, 0))],
        out_specs=[pl.BlockSpec((CMP_TQ, NSA_HEADS * DH), lambda i: (i, 0)),
                   pl.BlockSpec((NSA_GROUPS, CMP_TQ, nsel), lambda i: (0, i, 0))],
        out_shape=[jax.ShapeDtypeStruct((seq, NSA_HEADS * DH), BF16),
                   jax.ShapeDtypeStruct((NSA_GROUPS, seq, nsel), BF16)],
        compiler_params=_params(("arbitrary",)),
        name="cmp_attn_topk",
    )(main, kc, vc, mt)


ATT_TQ = 512
ATT_TK = 512
WIN_SPAN = WINDOW + ATT_TQ // 2


def _nsa_attn_kernel(q_ref, mask_ref, ks_ref, vs_ref, kw_ref, vw_ref, oc_ref, gate_ref, o_ref,
                     qa_scr, acc_scr, m_scr):
    i = pl.program_id(1)
    qbase = i * ATT_TQ
    assert mask_ref.shape[2] == LANES, "the block-selection bias rides in one extra 128-wide contraction slab"
    bias = mask_ref[0]
    for r in range(NSA_REP):
        qa_scr[r] = jnp.concatenate([q_ref[:, r * DH:(r + 1) * DH], bias], axis=1)
    m_scr[...] = jnp.full(m_scr.shape, NEG_INF, F32)
    acc_scr[...] = jnp.zeros(acc_scr.shape, F32)
    krow = lax.broadcasted_iota(jnp.int32, (ATT_TK, LANES), 0)
    kblk = lax.broadcasted_iota(jnp.int32, (ATT_TK, LANES), 1)

    half = ATT_TQ // 2

    def key_tile(j):
        k0 = pl.multiple_of(j * ATT_TK, ATT_TK)
        onehot = jnp.where(((krow + k0) >> 6) == kblk, 1.0, 0.0).astype(BF16)
        k_aug = jnp.concatenate([ks_ref[pl.ds(k0, ATT_TK), :], onehot], axis=1)
        v_aug = jnp.concatenate([vs_ref[pl.ds(k0, ATT_TK), :], jnp.ones((ATT_TK, LANES), BF16)], axis=1)
        return k_aug, v_aug

    def online_update(r, rows, s, v_aug):
        m_prev = m_scr[r, rows, :]
        m_new = jnp.maximum(m_prev, jnp.max(s, axis=-1, keepdims=True))
        alpha = jnp.exp2(m_prev - m_new)
        p = jnp.exp2(s - jnp.concatenate([m_new] * (s.shape[1] // LANES), axis=1))
        acc_scr[r, rows, :] = (acc_scr[r, rows, :] * jnp.concatenate([alpha, alpha], axis=1)
                               + _dot(p.astype(BF16), v_aug))
        m_scr[r, rows, :] = m_new

    def step(j, diagonal):
        k_aug, v_aug = key_tile(j)
        if not diagonal:
            for r in range(NSA_REP):
                online_update(r, slice(0, ATT_TQ), _dot_nt(qa_scr[r], k_aug), v_aug)
            return
        r_ = lax.broadcasted_iota(jnp.int32, (half, half), 0)
        c_ = lax.broadcasted_iota(jnp.int32, (half, half), 1)
        tri_bias = jnp.where(c_ <= r_, 0.0, MASK_NEG)
        for r in range(NSA_REP):
            s0 = _dot_nt(qa_scr[r, 0:half, :], k_aug[:half]) + tri_bias
            online_update(r, slice(0, half), s0, v_aug[:half])
            s1 = _dot_nt(qa_scr[r, half:ATT_TQ, :], k_aug)
            s1 = jnp.concatenate([s1[:, :half], s1[:, half:] + tri_bias], axis=1)
            online_update(r, slice(half, ATT_TQ), s1, v_aug)

    def body(jj, carry):
        for u in range(4):
            step(4 * jj + u, False)
        return carry

    lax.fori_loop(0, i // 4, body, 0)
    done = (i // 4) * 4

    @pl.when((i & 2) != 0)
    def _():
        step(done, False)
        step(done + 1, False)

    @pl.when((i & 1) != 0)
    def _():
        step(i - 1, False)

    step(i, True)

    first_gate = SSM_HEADS + 3 * NSA_REP * pl.program_id(0)
    gates = jax.nn.sigmoid(pltpu.roll(gate_ref[...], SMALL_W - first_gate, 1))
    r_ = lax.broadcasted_iota(jnp.int32, (half, WIN_SPAN), 0)
    c_ = lax.broadcasted_iota(jnp.int32, (half, WIN_SPAN), 1)
    for hh in range(2):
        rows = slice(hh * half, (hh + 1) * half)
        q0 = qbase + hh * half
        start = pl.multiple_of(jnp.maximum(q0 - WINDOW, 0), half)
        kw = kw_ref[pl.ds(start, WIN_SPAN), :]
        vw_aug = jnp.concatenate([vw_ref[pl.ds(start, WIN_SPAN), :], jnp.ones((WIN_SPAN, LANES), BF16)], axis=1)
        diff = (q0 + r_) - (start + c_)
        win_bias = jnp.where((diff >= 0) & (diff < WINDOW), 0.0, MASK_NEG)
        for r in range(NSA_REP):
            s = _dot_nt(q_ref[rows, r * DH:(r + 1) * DH], kw) + win_bias
            p = jnp.exp2(s - jnp.max(s, axis=-1, keepdims=True))
            ow = _dot(p.astype(BF16), vw_aug)
            o_win = ow[:, :DH] / jnp.maximum(ow[:, DH:], TINY)
            acc = acc_scr[r, rows, :]
            o_sel = acc[:, :DH] / jnp.maximum(acc[:, DH:], TINY)
            g_r = gates[rows]
            o = (g_r[:, 3 * r:3 * r + 1] * oc_ref[rows, r * DH:(r + 1) * DH].astype(F32)
                 + g_r[:, 3 * r + 1:3 * r + 2] * o_sel
                 + g_r[:, 3 * r + 2:3 * r + 3] * o_win)
            o_ref[rows, r * DH:(r + 1) * DH] = o.astype(o_ref.dtype)


def _nsa_attn(main, small, mask, o_cmp, seq):
    gw = NSA_REP * DH
    nsel = mask.shape[2]
    kv_spec = lambda col: pl.BlockSpec((seq, DH), lambda g, i: (0, col // DH + g))
    return pl.pallas_call(
        _nsa_attn_kernel,
        grid=(NSA_GROUPS, seq // ATT_TQ),
        in_specs=[pl.BlockSpec((ATT_TQ, gw), lambda g, i: (i, COL_Q // gw + g)),
                  pl.BlockSpec((1, ATT_TQ, nsel), lambda g, i: (g, i, 0)),
                  kv_spec(COL_KS), kv_spec(COL_VS), kv_spec(COL_KW), kv_spec(COL_VW),
                  pl.BlockSpec((ATT_TQ, gw), lambda g, i: (i, g)),
                  pl.BlockSpec((ATT_TQ, SMALL_W), lambda g, i: (i, 0))],
        out_specs=pl.BlockSpec((ATT_TQ, gw), lambda g, i: (i, g)),
        out_shape=jax.ShapeDtypeStruct((seq, NSA_HEADS * DH), BF16),
        scratch_shapes=[pltpu.VMEM((NSA_REP, ATT_TQ, 2 * DH), BF16),
                        pltpu.VMEM((NSA_REP, ATT_TQ, 2 * DH), F32),
                        pltpu.VMEM((NSA_REP, ATT_TQ, LANES), F32)],
        compiler_params=_params(("arbitrary", "arbitrary")),
        name="nsa_sel_win_attn",
    )(main, mask, main, main, main, main, o_cmp, small)


HALO = 8
GROUP_W = SSM_D_INNER // SSM_GROUPS
HEADS_PER_GROUP = SSM_HEADS // SSM_GROUPS
MAMBA_ROWS = 2 * SSD_CHUNK


def _silu(x):
    h = 0.5 * x
    return h + h * jnp.tanh(h)


def _mamba_kernel(z_ref, xs_ref, b_ref, c_ref, dt_ref,
                  cwx_ref, cwb_ref, cwc_ref, cbx_ref, cbb_ref, cbc_ref,
                  dtb_ref, alog_ref, dfull_ref, nw_ref, expand2_ref, tril3_ref,
                  o_ref, state_scr, xs_ext, b_ext, c_ext):
    L = SSD_CHUNK

    @pl.when(pl.program_id(0) == 0)
    def _():
        state_scr[...] = jnp.zeros(state_scr.shape, F32)
        for carry in (xs_ext, b_ext, c_ext):
            carry[...] = jnp.zeros(carry.shape, F32)

    for sub in range(z_ref.shape[0] // L):
        _mamba_chunk(slice(sub * L, (sub + 1) * L), z_ref, xs_ref, b_ref, c_ref, dt_ref,
                     cwx_ref, cwb_ref, cwc_ref, cbx_ref, cbb_ref, cbc_ref,
                     dtb_ref, alog_ref, dfull_ref, nw_ref, expand2_ref, tril3_ref,
                     o_ref, state_scr, xs_ext, b_ext, c_ext)


def _mamba_chunk(rows, z_ref, xs_ref, b_ref, c_ref, dt_ref,
                 cwx_ref, cwb_ref, cwc_ref, cbx_ref, cbb_ref, cbc_ref,
                 dtb_ref, alog_ref, dfull_ref, nw_ref, expand2_ref, tril3_ref,
                 o_ref, state_scr, xs_ext, b_ext, c_ext):
    L = SSD_CHUNK

    def conv_silu(x_ref, carry, w_ref, bias_ref):
        x = x_ref[rows, :].astype(F32)
        first_row = lax.broadcasted_iota(jnp.int32, (HALO, x.shape[1]), 0) == 0
        r = None
        for k in range(CONV_WIDTH - 1):
            t = w_ref[k:k + 1, :] * x
            if r is not None:
                t = t + r
            rolled = pltpu.roll(t, 1, 0)
            top = jnp.where(first_row, carry[k:k + 1, :], rolled[0:HALO])
            carry[k:k + 1, :] = t[L - 1:L, :]
            r = jnp.concatenate([top, rolled[HALO:]], axis=0)
        return _silu(r + w_ref[CONV_WIDTH - 1:CONV_WIDTH, :] * x + bias_ref[...])

    xs = conv_silu(xs_ref, xs_ext, cwx_ref, cbx_ref)
    bm = conv_silu(b_ref, b_ext, cwb_ref, cbb_ref).astype(BF16)
    cm = conv_silu(c_ref, c_ext, cwc_ref, cbc_ref).astype(BF16)

    xdt_raw = dt_ref[rows, :] + dtb_ref[...]
    dt = jnp.maximum(xdt_raw, 0.0) + jnp.log1p(jnp.exp(-jnp.abs(xdt_raw)))
    adt = dt * (-jnp.exp(alog_ref[...]))
    acs = _dot(tril3_ref[...], jnp.concatenate(_split3(adt), axis=0)) * LOG2E
    acs_t = acs.T
    a_last = acs[L - 1:L, :]
    stacked = jnp.concatenate([dt, jnp.exp2(acs), jnp.exp2(a_last - acs)], axis=0)
    wide = _dot(jnp.concatenate(_split2(stacked), axis=1), expand2_ref[...])
    dt_w = wide[0:L]
    in_decay_w = wide[L:2 * L]
    out_decay_w = wide[2 * L:3 * L]

    xdt = xs * dt_w
    xst = xdt * out_decay_w
    causal = lax.broadcasted_iota(jnp.int32, (L, L), 0) >= lax.broadcasted_iota(jnp.int32, (L, L), 1)
    causal_bias = jnp.where(causal, 0.0, MASK_NEG)
    lane = lax.broadcasted_iota(jnp.int32, (L, LANES), 1)
    low_half = lane < SSM_HEAD_DIM
    chunk_decay = jnp.exp2(acs_t[:, L - 1:L])

    for g in range(SSM_GROUPS):
        gs = slice(g * GROUP_W, (g + 1) * GROUP_W)
        bg = bm[:, g * SSM_STATE:(g + 1) * SSM_STATE]
        cg = cm[:, g * SSM_STATE:(g + 1) * SSM_STATE]
        cb = _dot_nt(cg, bg)
        prev = state_scr[gs, :]
        y_off = _dot_nt(cg, prev.astype(BF16)) * in_decay_w[:, gs]
        pieces = []
        for hp in range(HEADS_PER_GROUP // 2):
            lhs = []
            for e in range(2):
                h = g * HEADS_PER_GROUP + 2 * hp + e
                decay = jnp.exp2(acs[:, h:h + 1] - acs_t[h:h + 1, :] + causal_bias)
                lhs.append((cb * decay).astype(BF16))
            slab = xdt[:, g * GROUP_W + hp * LANES:g * GROUP_W + (hp + 1) * LANES]
            rhs = jnp.concatenate([jnp.where(low_half, slab, 0.0), jnp.where(low_half, 0.0, slab)], axis=0)
            pieces.append(_dot(jnp.concatenate(lhs, axis=1), rhs.astype(BF16)))
        y_g = jnp.concatenate(pieces, axis=1) + y_off
        y_g = y_g + dfull_ref[:, gs] * xs[:, gs]
        y_g = y_g * _silu(z_ref[rows, gs].astype(F32))
        ms = jnp.mean(y_g * y_g, axis=-1, keepdims=True)
        o_ref[rows, gs] = (y_g * lax.rsqrt(ms + EPS) * nw_ref[:, gs]).astype(o_ref.dtype)
        new = _dot(xst[:, gs].T.astype(BF16), bg)
        cd = jnp.broadcast_to(chunk_decay[g * HEADS_PER_GROUP:(g + 1) * HEADS_PER_GROUP], (HEADS_PER_GROUP, LANES))
        cd = jnp.broadcast_to(cd[:, None, :], (HEADS_PER_GROUP, SSM_HEAD_DIM, LANES)).reshape(GROUP_W, LANES)
        state_scr[gs, :] = prev * cd + new


def _mamba(main, small, cwx, cwb, cwc, cbx, cbb, cbc, dtb, alog, dfull, nw, expand2, tril3, seq):
    L = MAMBA_ROWS
    bw = SSM_GROUPS * SSM_STATE
    full = lambda a: pl.BlockSpec(a.shape, lambda c: (0,) * a.ndim)
    consts = [cwx, cwb, cwc, cbx, cbb, cbc, dtb, alog, dfull, nw, expand2, tril3]
    return pl.pallas_call(
        _mamba_kernel,
        grid=(seq // L,),
        in_specs=[pl.BlockSpec((L, SSM_D_INNER), lambda c: (c, COL_Z // SSM_D_INNER)),
                  pl.BlockSpec((L, SSM_D_INNER), lambda c: (c, COL_XS // SSM_D_INNER)),
                  pl.BlockSpec((L, bw), lambda c: (c, COL_B // bw)),
                  pl.BlockSpec((L, bw), lambda c: (c, COL_C // bw)),
                  pl.BlockSpec((L, LANES), lambda c: (c, 0))] + [full(a) for a in consts],
        out_specs=pl.BlockSpec((L, SSM_D_INNER), lambda c: (c, 0)),
        out_shape=jax.ShapeDtypeStruct((seq, SSM_D_INNER), BF16),
        scratch_shapes=[pltpu.VMEM((SSM_D_INNER, SSM_STATE), F32),
                        pltpu.VMEM((HALO, SSM_D_INNER), F32),
                        pltpu.VMEM((HALO, bw), F32),
                        pltpu.VMEM((HALO, bw), F32)],
        compiler_params=_params(("arbitrary",)),
        name="mamba2_ssd",
    )(main, main, main, main, small, *consts)


def _merge_kernel(a_ref, b_ref, pa_ref, pb_ref, ga_ref, gb_ref, o_ref):
    ya = _dot(a_ref[...], pa_ref[...])
    yb = _dot(b_ref[...], pb_ref[...])
    o = jax.nn.sigmoid(ga_ref[...].astype(F32)) * ya + jax.nn.sigmoid(gb_ref[...].astype(F32)) * yb
    o_ref[...] = o.astype(o_ref.dtype)


def _merge(o_nsa, o_ssm, pa, pb, main, seq):
    tm = min(seq, 512)
    tn = 1024
    return pl.pallas_call(
        _merge_kernel,
        grid=(seq // tm, D_MODEL // tn),
        in_specs=[pl.BlockSpec((tm, NSA_HEADS * DH), lambda i, j: (i, 0)),
                  pl.BlockSpec((tm, SSM_D_INNER), lambda i, j: (i, 0)),
                  pl.BlockSpec((NSA_HEADS * DH, tn), lambda i, j: (0, j)),
                  pl.BlockSpec((SSM_D_INNER, tn), lambda i, j: (0, j)),
                  pl.BlockSpec((tm, tn), lambda i, j: (i, COL_GA // tn + j)),
                  pl.BlockSpec((tm, tn), lambda i, j: (i, COL_GB // tn + j))],
        out_specs=pl.BlockSpec((tm, tn), lambda i, j: (i, j)),
        out_shape=jax.ShapeDtypeStruct((seq, D_MODEL), BF16),
        compiler_params=_params(("arbitrary", "arbitrary")),
        name="gated_merge",
    )(o_nsa, o_ssm, pa, pb, main, main)


def _outproj_kernel(m_ref, w_ref, x_ref, nw_ref, h_ref, hn_ref):
    h = x_ref[...] + _dot(m_ref[...], w_ref[...])
    h_ref[...] = h
    ms = jnp.mean(h * h, axis=-1, keepdims=True)
    hn_ref[...] = (h * lax.rsqrt(ms + EPS) * nw_ref[...]).astype(hn_ref.dtype)


def _outproj(merged, w_out, x2, norm_w, seq):
    tm = min(seq, 512)
    row = pl.BlockSpec((tm, D_MODEL), lambda i: (i, 0))
    return pl.pallas_call(
        _outproj_kernel,
        grid=(seq // tm,),
        in_specs=[row, pl.BlockSpec((D_MODEL, D_MODEL), lambda i: (0, 0), pipeline_mode=pl.Buffered(1)), row,
                  pl.BlockSpec((1, D_MODEL), lambda i: (0, 0))],
        out_specs=[row, row],
        out_shape=[jax.ShapeDtypeStruct((seq, D_MODEL), F32), jax.ShapeDtypeStruct((seq, D_MODEL), BF16)],
        compiler_params=_params(("arbitrary",)),
        name="outproj_residual_norm",
    )(merged, w_out, x2, norm_w.reshape(1, D_MODEL))


def _mlp_kernel(hn_ref, wu_ref, wd_ref, h_ref, nw_ref, o_ref, acc_scr):
    f = pl.program_id(1)

    @pl.when(f == 0)
    def _():
        acc_scr[...] = jnp.zeros(acc_scr.shape, F32)

    up = jnp.maximum(_dot(hn_ref[...], wu_ref[...]), 0.0)
    acc_scr[...] += _dot((up * up).astype(BF16), wd_ref[...])

    @pl.when(f == pl.num_programs(1) - 1)
    def _():
        h = h_ref[...] + acc_scr[...]
        ms = jnp.mean(h * h, axis=-1, keepdims=True)
        o_ref[...] = h * lax.rsqrt(ms + EPS) * nw_ref[...]


def _mlp(hn, w_up, w_down, h1, norm_w, seq):
    tm = min(seq, 512)
    tf = 1024
    row = lambda i, f: (i, 0)
    return pl.pallas_call(
        _mlp_kernel,
        grid=(seq // tm, D_FF // tf),
        in_specs=[pl.BlockSpec((tm, D_MODEL), row),
                  pl.BlockSpec((D_MODEL, tf), lambda i, f: (0, f)),
                  pl.BlockSpec((tf, D_MODEL), lambda i, f: (f, 0)),
                  pl.BlockSpec((tm, D_MODEL), row),
                  pl.BlockSpec((1, D_MODEL), lambda i, f: (0, 0))],
        out_specs=pl.BlockSpec((tm, D_MODEL), row),
        out_shape=jax.ShapeDtypeStruct((seq, D_MODEL), F32),
        scratch_shapes=[pltpu.VMEM((tm, D_MODEL), F32)],
        compiler_params=_params(("arbitrary", "arbitrary")),
        name="mlp_final_norm",
    )(hn, w_up, w_down, h1, norm_w.reshape(1, D_MODEL))


def _cmp_to_sel(seq):
    n_cmp = (seq - CMP_BLOCK) // CMP_STRIDE + 1
    n_sel = seq // SEL_BLOCK
    c_start = np.arange(n_cmp) * CMP_STRIDE
    s_start = np.arange(n_sel) * SEL_BLOCK
    overlap = np.clip(np.minimum(c_start[:, None] + CMP_BLOCK, s_start[None, :] + SEL_BLOCK)
                      - np.maximum(c_start[:, None], s_start[None, :]), 0, None)
    m = np.zeros((seq // CMP_STRIDE, LANES), np.float32)
    m[:n_cmp, :n_sel] = overlap / CMP_STRIDE
    return jnp.asarray(m.T, BF16)


def kernel(x, positions, norm_mix_w, w_in, cmp_pos_k, cmp_pos_v, cmp_k_w1, cmp_k_w2, cmp_v_w1, cmp_v_w2, conv_w, conv_b, dt_bias, a_log, ssm_d, ssm_norm_w, w_proj_nsa, w_proj_ssm, w_out, norm_mlp_w, w_up, w_down, norm_final_w):
    bsz, seq, _ = x.shape
    assert bsz == 1 and seq % 1024 == 0 and SEL_TOPK <= seq // SEL_BLOCK <= LANES
    assert w_in.shape[0] == 1, "one layer"
    x2 = x.reshape(seq, D_MODEL)

    o_gate = NSA_HEADS * DH + 6 * NSA_GROUPS * DH
    o_z = o_gate + 3 * NSA_HEADS
    o_dt = o_z + 2 * SSM_D_INNER + 2 * SSM_GROUPS * SSM_STATE
    o_bg = o_dt + SSM_HEADS
    assert o_z == SRC_Z and o_bg == SRC_BG and o_gate == ATT_W
    w_t = jnp.swapaxes(w_in, 1, 2)[0]
    n_gate = 3 * NSA_HEADS
    w_small_t = jnp.concatenate([w_t[o_dt:o_dt + SSM_HEADS], w_t[o_gate:o_gate + n_gate],
                                 jnp.zeros((SMALL_W - SSM_HEADS - n_gate, D_MODEL), F32)], axis=0)

    pos = positions.reshape(seq)
    cosf, sinf = _rope_tables(pos)
    att, kc_chunks, vc_chunks, u = _inproj(x2, norm_mix_w[0], w_t, cosf, sinf)
    ssm = _proj_ssm(u, w_t)
    small = _proj_small(u, w_small_t)

    ncp = seq // CMP_STRIDE
    def pos2(p):
        p = p.reshape(2, CMP_STRIDE * DH)
        return jnp.concatenate([p, jnp.zeros((14, CMP_STRIDE * DH), F32)], axis=0).astype(BF16)
    pos_end = pos[CMP_BLOCK - 1::CMP_STRIDE]
    cos_c, sin_c = _rope_tables(jnp.concatenate([pos_end, jnp.zeros((ncp - pos_end.shape[0],), pos.dtype)]))
    kc, vc = _compress(kc_chunks, vc_chunks, pos2(cmp_pos_k[0]), pos2(cmp_pos_v[0]),
                       cmp_k_w1[0].astype(BF16), cmp_k_w2[0].astype(BF16),
                       cmp_v_w1[0].astype(BF16), cmp_v_w2[0].astype(BF16), cos_c, sin_c)

    o_cmp, sel_bias = _cmp_attn(att, kc, vc, _cmp_to_sel(seq), seq)
    o_nsa = _nsa_attn(att, small, sel_bias, o_cmp, seq)

    cw = conv_w[0]
    cb_ = conv_b[0].reshape(1, -1)
    lane_pad = lambda a: jnp.concatenate([a.reshape(1, -1), jnp.zeros((1, LANES - a.shape[-1]), F32)], axis=1)
    head_of = np.arange(SSM_D_INNER) // SSM_HEAD_DIM
    expand = np.arange(LANES)[:, None] == head_of[None, :]
    expand2 = jnp.asarray(np.concatenate([expand, expand], axis=0), BF16)
    tril = np.tril(np.ones((SSD_CHUNK, SSD_CHUNK), np.float32))
    tril3 = jnp.asarray(np.concatenate([tril, tril, tril], axis=1), BF16)
    o_ssm = _mamba(ssm, small, cw[:, :4096], cw[:, 4096:5120], cw[:, 5120:],
                   cb_[:, :4096], cb_[:, 4096:5120], cb_[:, 5120:],
                   lane_pad(dt_bias[0]), lane_pad(a_log[0]),
                   jnp.repeat(ssm_d[0], SSM_HEAD_DIM).reshape(1, -1), ssm_norm_w[0].reshape(1, -1),
                   expand2, tril3, seq)

    merged = _merge(o_nsa, o_ssm, w_proj_nsa[0].astype(BF16), w_proj_ssm[0].astype(BF16), ssm, seq)
    h1, hn = _outproj(merged, w_out[0].astype(BF16), x2, norm_mlp_w[0], seq)
    out = _mlp(hn, w_up[0].astype(BF16), w_down[0].astype(BF16), h1, norm_final_w, seq)
    return out.reshape(bsz, seq, D_MODEL)
```

```python
import math

import jax
import jax.numpy as jnp
import numpy as np
from jax import lax
from jax.experimental import pallas as pl
from jax.experimental.pallas import tpu as pltpu

F32 = jnp.float32
BF16 = jnp.bfloat16

D_MODEL = 2048
NSA_HEADS = 16
NSA_GROUPS = 4
NSA_REP = NSA_HEADS // NSA_GROUPS
DH = 128
CMP_BLOCK = 32
CMP_STRIDE = 16
SEL_BLOCK = 64
SEL_TOPK = 16
WINDOW = 512
ROPE_THETA = 10000.0
N_FORCED = 3
SSM_D_INNER = 4096
SSM_HEAD_DIM = 64
SSM_HEADS = 64
SSM_GROUPS = 8
SSM_STATE = 128
CONV_WIDTH = 4
SSD_CHUNK = 128
D_FF = 4 * D_MODEL
EPS = 1e-6
NEG_INF = -1e30
TINY = 1e-30
MASK_NEG = NEG_INF
TAKEN = -2.0
LOG2E = math.log2(math.e)
LANES = 128
VMEM_LIMIT = 56 * 1024 * 1024

COL_Q = 0
COL_KC = 2048
COL_VC = 2560
COL_KS = 3072
COL_VS = 3584
COL_KW = 4096
COL_VW = 4608
ATT_W = 5120
COL_Z = 0
COL_XS = 4096
COL_B = 8192
COL_C = 9216
COL_GA = 10240
COL_GB = 12288
SSM_W = 14336
SMALL_W = 128


def _dot(a, b):
    return jnp.dot(a, b, preferred_element_type=F32)


def _dot_nt(a, b):
    return lax.dot_general(a, b, (((1,), (1,)), ((), ())), preferred_element_type=F32)


def _params(sem):
    return pltpu.CompilerParams(dimension_semantics=sem, vmem_limit_bytes=VMEM_LIMIT)


def _rope(x, cos, sin_signed):
    return x * cos + pltpu.roll(x, DH // 2, 1) * sin_signed


def _split2(x):
    hi = x.astype(BF16)
    lo = (x - hi.astype(F32)).astype(BF16)
    return hi, lo


def _split3(x):
    hi = x.astype(BF16)
    r = x - hi.astype(F32)
    mid = r.astype(BF16)
    lo = (r - mid.astype(F32)).astype(BF16)
    return hi, mid, lo


def _rope_table_kernel(pos_ref, inv_ref, sign_ref, cos_ref, sin_ref):
    ang = pos_ref[...].astype(F32) * inv_ref[...]
    cos_ref[...] = jnp.cos(ang)
    sin_ref[...] = jnp.sin(ang) * sign_ref[...]


def _rope_tables(positions):
    seq = positions.shape[0]
    half = DH // 2
    inv = ROPE_THETA ** (-jnp.arange(0, DH, 2, dtype=F32) / DH)
    inv_full = jnp.concatenate([inv, inv]).reshape(1, DH)
    sign = jnp.concatenate([-jnp.ones((half,), F32), jnp.ones((half,), F32)]).reshape(1, DH)
    tm = math.gcd(seq, 1024)
    return pl.pallas_call(
        _rope_table_kernel,
        grid=(seq // tm,),
        in_specs=[pl.BlockSpec((tm, 1), lambda i: (i, 0)),
                  pl.BlockSpec((1, DH), lambda i: (0, 0)),
                  pl.BlockSpec((1, DH), lambda i: (0, 0))],
        out_specs=[pl.BlockSpec((tm, DH), lambda i: (i, 0))] * 2,
        out_shape=[jax.ShapeDtypeStruct((seq, DH), F32)] * 2,
        compiler_params=_params(("arbitrary",)),
        name="rope_tables",
    )(positions.reshape(seq, 1), inv_full, sign)


INPROJ_TN = 512
Q_TILE_LO = COL_Q // INPROJ_TN
Q_TILE_HI = COL_KC // INPROJ_TN
KC_TILE = COL_KC // INPROJ_TN
VC_TILE = COL_VC // INPROJ_TN
KS_TILE = COL_KS // INPROJ_TN
KW_TILE = COL_KW // INPROJ_TN


def _inproj_kernel(x_ref, nw_ref, w_ref, cos_ref, sin_ref, o_ref, kcx_ref, vcx_ref, u_ref, acc_scr):
    j = pl.program_id(1)

    @pl.when(j == 0)
    def _():
        xf = x_ref[...]
        ms = jnp.mean(xf * xf, axis=-1, keepdims=True)
        u_ref[...] = (xf * lax.rsqrt(ms + EPS) * nw_ref[...]).astype(BF16)

    acc = _dot_nt(u_ref[...], w_ref[...].astype(BF16))
    is_q = jnp.logical_and(j >= Q_TILE_LO, j < Q_TILE_HI)
    is_rope = is_q | (j == KS_TILE) | (j == KW_TILE)

    @pl.when(jnp.logical_not(is_rope))
    def _():
        o_ref[...] = acc.astype(o_ref.dtype)

    @pl.when(is_rope)
    def _():
        cos = cos_ref[...]
        sin = sin_ref[...]
        scale = jnp.where(is_q, DH ** -0.5 * LOG2E, 1.0).astype(F32)
        for h in range(INPROJ_TN // DH):
            xh = acc[:, h * DH:(h + 1) * DH]
            o_ref[:, h * DH:(h + 1) * DH] = (_rope(xh, cos, sin) * scale).astype(o_ref.dtype)

    def chunked(dst_ref):
        n_chunks = acc_scr.shape[1] // CMP_STRIDE
        for g in range(NSA_GROUPS):
            acc_scr[g] = acc[:, g * DH:(g + 1) * DH]
            for l in range(CMP_STRIDE):
                rows_l = acc_scr[g, pl.ds(l, n_chunks, stride=CMP_STRIDE), :]
                dst_ref[g, :, l * DH:(l + 1) * DH] = rows_l.astype(dst_ref.dtype)

    @pl.when(j == KC_TILE)
    def _():
        chunked(kcx_ref)

    @pl.when(j == VC_TILE)
    def _():
        chunked(vcx_ref)


def _inproj(x2, norm_w, w_t, cosf, sinf):
    seq = x2.shape[0]
    tm = min(seq, 1024)
    tn = INPROJ_TN
    chunk_w = CMP_STRIDE * DH
    chunk_spec = pl.BlockSpec((NSA_GROUPS, tm // CMP_STRIDE, chunk_w), lambda i, j: (0, i, 0))
    chunk_shape = jax.ShapeDtypeStruct((NSA_GROUPS, seq // CMP_STRIDE, chunk_w), BF16)
    return pl.pallas_call(
        _inproj_kernel,
        grid=(seq // tm, ATT_W // tn),
        in_specs=[pl.BlockSpec((tm, D_MODEL), lambda i, j: (i, 0)),
                  pl.BlockSpec((1, D_MODEL), lambda i, j: (0, 0)),
                  pl.BlockSpec((tn, D_MODEL), lambda i, j: (j, 0)),
                  pl.BlockSpec((tm, DH), lambda i, j: (i, 0)),
                  pl.BlockSpec((tm, DH), lambda i, j: (i, 0))],
        out_specs=[pl.BlockSpec((tm, tn), lambda i, j: (i, j)),
                   chunk_spec, chunk_spec,
                   pl.BlockSpec((tm, D_MODEL), lambda i, j: (i, 0))],
        out_shape=[jax.ShapeDtypeStruct((seq, ATT_W), BF16),
                   chunk_shape, chunk_shape,
                   jax.ShapeDtypeStruct((seq, D_MODEL), BF16)],
        scratch_shapes=[pltpu.VMEM((NSA_GROUPS, tm, DH), F32)],
        compiler_params=_params(("arbitrary", "arbitrary")),
        name="inproj_attn",
    )(x2, norm_w.reshape(1, D_MODEL), w_t, cosf, sinf)


SSM_TN = 1024
SRC_Z = NSA_HEADS * DH + 6 * NSA_GROUPS * DH + 3 * NSA_HEADS
SRC_BG = SRC_Z + 2 * SSM_D_INNER + 2 * SSM_GROUPS * SSM_STATE + SSM_HEADS
SSM_SPLIT = COL_GA // SSM_TN


def _nt_matmul_kernel(a_ref, w_ref, o_ref):
    o_ref[...] = _dot_nt(a_ref[...], w_ref[...].astype(BF16)).astype(o_ref.dtype)


def _proj_ssm(u, w_t):
    seq, k = u.shape
    tm = min(seq, 2048)

    def src_rows(i, j):
        jump = (SRC_BG - SRC_Z - SSM_TN * SSM_SPLIT) // 16
        return ((SRC_Z // 16 + (SSM_TN // 16) * j + jump * (j // SSM_SPLIT)) * 16, 0)

    return pl.pallas_call(
        _nt_matmul_kernel,
        grid=(seq // tm, SSM_W // SSM_TN),
        in_specs=[pl.BlockSpec((tm, k), lambda i, j: (i, 0)),
                  pl.BlockSpec((pl.Element(SSM_TN), pl.Element(k)), src_rows)],
        out_specs=pl.BlockSpec((tm, SSM_TN), lambda i, j: (i, j)),
        out_shape=jax.ShapeDtypeStruct((seq, SSM_W), BF16),
        compiler_params=_params(("arbitrary", "arbitrary")),
        name="inproj_ssm",
    )(u, w_t)


def _proj_small(u, w_small_t):
    seq, k = u.shape
    n = w_small_t.shape[0]
    tm = min(seq, 1024)
    return pl.pallas_call(
        _nt_matmul_kernel,
        grid=(seq // tm,),
        in_specs=[pl.BlockSpec((tm, k), lambda i: (i, 0)),
                  pl.BlockSpec((n, k), lambda i: (0, 0))],
        out_specs=pl.BlockSpec((tm, n), lambda i: (i, 0)),
        out_shape=jax.ShapeDtypeStruct((seq, n), F32),
        compiler_params=_params(("arbitrary",)),
        name="inproj_small",
    )(u, w_small_t)


def _compress_kernel(xk_ref, xv_ref, pk_ref, pv_ref, w1k_ref, w2k_ref, w1v_ref, w2v_ref,
                     cos_ref, sin_ref, kc_ref, vc_ref):
    half = CMP_STRIDE * DH
    ncp = xk_ref.shape[1]

    def comp(x_ref, p_ref, w1_ref, w2_ref):
        x = x_ref[0]
        wt = w1_ref[0:half, :]
        wb = w1_ref[half:2 * half, :]
        a = _dot(x, wt)
        b = _dot(x, wb)
        pb = _dot(p_ref[...], wt)[0:1] + _dot(p_ref[...], wb)[1:2]
        h = a + pltpu.roll(b, ncp - 1, 0) + pb
        h = h * jax.nn.sigmoid(h)
        return _dot(h.astype(BF16), w2_ref[...])

    kc = comp(xk_ref, pk_ref, w1k_ref, w2k_ref)
    kc_ref[0] = _rope(kc, cos_ref[...], sin_ref[...]).astype(BF16)
    vc_ref[0] = comp(xv_ref, pv_ref, w1v_ref, w2v_ref).astype(BF16)


def _compress(xk, xv, pk, pv, w1k, w2k, w1v, w2v, cos_c, sin_c):
    g, ncp, width = xk.shape
    full = lambda shape: pl.BlockSpec(shape, lambda i: (0,) * len(shape))
    per_g = pl.BlockSpec((1, ncp, width), lambda i: (i, 0, 0))
    out_g = pl.BlockSpec((1, ncp, DH), lambda i: (i, 0, 0))
    return pl.pallas_call(
        _compress_kernel,
        grid=(g,),
        in_specs=[per_g, per_g, full(pk.shape), full(pv.shape), full(w1k.shape), full(w2k.shape),
                  full(w1v.shape), full(w2v.shape), full(cos_c.shape), full(sin_c.shape)],
        out_specs=[out_g, out_g],
        out_shape=[jax.ShapeDtypeStruct((g, ncp, DH), BF16)] * 2,
        compiler_params=_params(("arbitrary",)),
        name="compress",
    )(xk, xv, pk, pv, w1k, w2k, w1v, w2v, cos_c, sin_c)


CMP_TQ = 256


def _cmp_attn_kernel(q_ref, kc_ref, vc_ref, mt_ref, oc_ref, mask_ref):
    i = pl.program_id(0)
    qbase = i * CMP_TQ
    ncp = kc_ref.shape[1]
    nsel = mt_ref.shape[0]
    row = lax.broadcasted_iota(jnp.int32, (CMP_TQ, ncp), 0)
    col = lax.broadcasted_iota(jnp.int32, (CMP_TQ, ncp), 1)
    bias1 = jnp.where((col * CMP_STRIDE + (CMP_BLOCK - 1)) <= qbase + row, 0.0, MASK_NEG)
    bias = jnp.concatenate([bias1] * NSA_REP, axis=0)
    has_key = jnp.concatenate([bias1[:, 0:1] == 0.0] * NSA_REP, axis=0)
    blk = lax.broadcasted_iota(jnp.int32, (nsel, CMP_TQ), 0)
    cur = (qbase + lax.broadcasted_iota(jnp.int32, (nsel, CMP_TQ), 1)) >> 6
    future = blk > cur
    forced = (blk == 0) | (blk == cur) | (blk == cur - 1)
    blk_f = blk.astype(F32)
    mt = mt_ref[...]
    importance = []
    for g in range(NSA_GROUPS):
        qg = jnp.concatenate(
            [q_ref[:, (g * NSA_REP + r) * DH:(g * NSA_REP + r + 1) * DH] for r in range(NSA_REP)], axis=0)
        s = _dot_nt(qg, kc_ref[g]) + bias
        p = jnp.exp2(s - jnp.max(s, axis=-1, keepdims=True)).astype(BF16)
        ov = _dot(p, jnp.concatenate([vc_ref[g], jnp.ones((ncp, LANES), BF16)], axis=1))
        inv = jnp.where(has_key, 1.0 / ov[:, DH:], 0.0)
        o = ov[:, :DH] * inv
        imp = None
        for r in range(NSA_REP):
            h = g * NSA_REP + r
            rows_r = slice(r * CMP_TQ, (r + 1) * CMP_TQ)
            oc_ref[:, h * DH:(h + 1) * DH] = o[rows_r].astype(oc_ref.dtype)
            part = _dot_nt(p[rows_r], mt) * inv[rows_r]
            imp = part if imp is None else imp + part
        importance.append(imp.T)
    v = jnp.where(future | forced, -1.0, jnp.stack(importance))
    for _ in range(SEL_TOPK - N_FORCED):
        mx = jnp.max(v, axis=1, keepdims=True)
        first = jnp.min(jnp.where(v == mx, blk_f, float(nsel)), axis=1, keepdims=True)
        v = jnp.where(blk_f == first, TAKEN, v)
    bias_t = jnp.where(forced | (v == TAKEN), 0.0, MASK_NEG)
    for g in range(NSA_GROUPS):
        mask_ref[g] = bias_t[g].T.astype(mask_ref.dtype)


def _cmp_attn(main, kc, vc, mt, seq):
    nsel = mt.shape[0]
    ncp = kc.shape[1]
    return pl.pallas_call(
        _cmp_attn_kernel,
        grid=(seq // CMP_TQ,),
        in_specs=[pl.BlockSpec((CMP_TQ, NSA_HEADS * DH), lambda i: (i, COL_Q // (NSA_HEADS * DH))),
                  pl.BlockSpec((NSA_GROUPS, ncp, DH), lambda i: (0, 0, 0)),
                  pl.BlockSpec((NSA_GROUPS, ncp, DH), lambda i: (0, 0, 0)),
                  pl.BlockSpec((nsel, ncp), lambda i: (0, 0))],
        out_specs=[pl.BlockSpec((CMP_TQ, NSA_HEADS * DH), lambda i: (i, 0)),
                   pl.BlockSpec((NSA_GROUPS, CMP_TQ, nsel), lambda i: (0, i, 0))],
        out_shape=[jax.ShapeDtypeStruct((seq, NSA_HEADS * DH), BF16),
                   jax.ShapeDtypeStruct((NSA_GROUPS, seq, nsel), BF16)],
        compiler_params=_params(("arbitrary",)),
        name="cmp_attn_topk",
    )(main, kc, vc, mt)


ATT_TQ = 512
ATT_TK = 512
WIN_SPAN = WINDOW + ATT_TQ // 2


def _nsa_attn_kernel(q_ref, mask_ref, ks_ref, vs_ref, kw_ref, vw_ref, oc_ref, gate_ref, o_ref,
                     qa_scr, acc_scr, m_scr):
    i = pl.program_id(1)
    qbase = i * ATT_TQ
    assert mask_ref.shape[2] == LANES, "the block-selection bias rides in one extra 128-wide contraction slab"
    bias = mask_ref[0]
    for r in range(NSA_REP):
        qa_scr[r] = jnp.concatenate([q_ref[:, r * DH:(r + 1) * DH], bias], axis=1)
    m_scr[...] = jnp.full(m_scr.shape, NEG_INF, F32)
    acc_scr[...] = jnp.zeros(acc_scr.shape, F32)
    krow = lax.broadcasted_iota(jnp.int32, (ATT_TK, LANES), 0)
    kblk = lax.broadcasted_iota(jnp.int32, (ATT_TK, LANES), 1)

    half = ATT_TQ // 2

    def key_tile(j):
        k0 = pl.multiple_of(j * ATT_TK, ATT_TK)
        onehot = jnp.where(((krow + k0) >> 6) == kblk, 1.0, 0.0).astype(BF16)
        k_aug = jnp.concatenate([ks_ref[pl.ds(k0, ATT_TK), :], onehot], axis=1)
        v_aug = jnp.concatenate([vs_ref[pl.ds(k0, ATT_TK), :], jnp.ones((ATT_TK, LANES), BF16)], axis=1)
        return k_aug, v_aug

    def online_update(r, rows, s, v_aug):
        m_prev = m_scr[r, rows, :]
        m_new = jnp.maximum(m_prev, jnp.max(s, axis=-1, keepdims=True))
        alpha = jnp.exp2(m_prev - m_new)
        p = jnp.exp2(s - jnp.concatenate([m_new] * (s.shape[1] // LANES), axis=1))
        acc_scr[r, rows, :] = (acc_scr[r, rows, :] * jnp.concatenate([alpha, alpha], axis=1)
                               + _dot(p.astype(BF16), v_aug))
        m_scr[r, rows, :] = m_new

    def step(j, diagonal):
        k_aug, v_aug = key_tile(j)
        if not diagonal:
            for r in range(NSA_REP):
                online_update(r, slice(0, ATT_TQ), _dot_nt(qa_scr[r], k_aug), v_aug)
            return
        r_ = lax.broadcasted_iota(jnp.int32, (half, half), 0)
        c_ = lax.broadcasted_iota(jnp.int32, (half, half), 1)
        tri_bias = jnp.where(c_ <= r_, 0.0, MASK_NEG)
        for r in range(NSA_REP):
            s0 = _dot_nt(qa_scr[r, 0:half, :], k_aug[:half]) + tri_bias
            online_update(r, slice(0, half), s0, v_aug[:half])
            s1 = _dot_nt(qa_scr[r, half:ATT_TQ, :], k_aug)
            s1 = jnp.concatenate([s1[:, :half], s1[:, half:] + tri_bias], axis=1)
            online_update(r, slice(half, ATT_TQ), s1, v_aug)

    def body(jj, carry):
        for u in range(4):
            step(4 * jj + u, False)
        return carry

    lax.fori_loop(0, i // 4, body, 0)
    done = (i // 4) * 4

    @pl.when((i & 2) != 0)
    def _():
        step(done, False)
        step(done + 1, False)

    @pl.when((i & 1) != 0)
    def _():
        step(i - 1, False)

    step(i, True)

    first_gate = SSM_HEADS + 3 * NSA_REP * pl.program_id(0)
    gates = jax.nn.sigmoid(pltpu.roll(gate_ref[...], SMALL_W - first_gate, 1))
    r_ = lax.broadcasted_iota(jnp.int32, (half, WIN_SPAN), 0)
    c_ = lax.broadcasted_iota(jnp.int32, (half, WIN_SPAN), 1)
    for hh in range(2):
        rows = slice(hh * half, (hh + 1) * half)
        q0 = qbase + hh * half
        start = pl.multiple_of(jnp.maximum(q0 - WINDOW, 0), half)
        kw = kw_ref[pl.ds(start, WIN_SPAN), :]
        vw_aug = jnp.concatenate([vw_ref[pl.ds(start, WIN_SPAN), :], jnp.ones((WIN_SPAN, LANES), BF16)], axis=1)
        diff = (q0 + r_) - (start + c_)
        win_bias = jnp.where((diff >= 0) & (diff < WINDOW), 0.0, MASK_NEG)
        for r in range(NSA_REP):
            s = _dot_nt(q_ref[rows, r * DH:(r + 1) * DH], kw) + win_bias
            p = jnp.exp2(s - jnp.max(s, axis=-1, keepdims=True))
            ow = _dot(p.astype(BF16), vw_aug)
            o_win = ow[:, :DH] / jnp.maximum(ow[:, DH:], TINY)
            acc = acc_scr[r, rows, :]
            o_sel = acc[:, :DH] / jnp.maximum(acc[:, DH:], TINY)
            g_r = gates[rows]
            o = (g_r[:, 3 * r:3 * r + 1] * oc_ref[rows, r * DH:(r + 1) * DH].astype(F32)
                 + g_r[:, 3 * r + 1:3 * r + 2] * o_sel
                 + g_r[:, 3 * r + 2:3 * r + 3] * o_win)
            o_ref[rows, r * DH:(r + 1) * DH] = o.astype(o_ref.dtype)


def _nsa_attn(main, small, mask, o_cmp, seq):
    gw = NSA_REP * DH
    nsel = mask.shape[2]
    kv_spec = lambda col: pl.BlockSpec((seq, DH), lambda g, i: (0, col // DH + g))
    return pl.pallas_call(
        _nsa_attn_kernel,
        grid=(NSA_GROUPS, seq // ATT_TQ),
        in_specs=[pl.BlockSpec((ATT_TQ, gw), lambda g, i: (i, COL_Q // gw + g)),
                  pl.BlockSpec((1, ATT_TQ, nsel), lambda g, i: (g, i, 0)),
                  kv_spec(COL_KS), kv_spec(COL_VS), kv_spec(COL_KW), kv_spec(COL_VW),
                  pl.BlockSpec((ATT_TQ, gw), lambda g, i: (i, g)),
                  pl.BlockSpec((ATT_TQ, SMALL_W), lambda g, i: (i, 0))],
        out_specs=pl.BlockSpec((ATT_TQ, gw), lambda g, i: (i, g)),
        out_shape=jax.ShapeDtypeStruct((seq, NSA_HEADS * DH), BF16),
        scratch_shapes=[pltpu.VMEM((NSA_REP, ATT_TQ, 2 * DH), BF16),
                        pltpu.VMEM((NSA_REP, ATT_TQ, 2 * DH), F32),
                        pltpu.VMEM((NSA_REP, ATT_TQ, LANES), F32)],
        compiler_params=_params(("arbitrary", "arbitrary")),
        name="nsa_sel_win_attn",
    )(main, mask, main, main, main, main, o_cmp, small)


HALO = 8
GROUP_W = SSM_D_INNER // SSM_GROUPS
HEADS_PER_GROUP = SSM_HEADS // SSM_GROUPS
MAMBA_ROWS = 2 * SSD_CHUNK


def _silu(x):
    h = 0.5 * x
    return h + h * jnp.tanh(h)


def _mamba_kernel(z_ref, xs_ref, b_ref, c_ref, dt_ref,
                  cwx_ref, cwb_ref, cwc_ref, cbx_ref, cbb_ref, cbc_ref,
                  dtb_ref, alog_ref, dfull_ref, nw_ref, expand2_ref, tril3_ref,
                  o_ref, state_scr, xs_ext, b_ext, c_ext):
    L = SSD_CHUNK

    @pl.when(pl.program_id(0) == 0)
    def _():
        state_scr[...] = jnp.zeros(state_scr.shape, F32)
        for carry in (xs_ext, b_ext, c_ext):
            carry[...] = jnp.zeros(carry.shape, F32)

    for sub in range(z_ref.shape[0] // L):
        _mamba_chunk(slice(sub * L, (sub + 1) * L), z_ref, xs_ref, b_ref, c_ref, dt_ref,
                     cwx_ref, cwb_ref, cwc_ref, cbx_ref, cbb_ref, cbc_ref,
                     dtb_ref, alog_ref, dfull_ref, nw_ref, expand2_ref, tril3_ref,
                     o_ref, state_scr, xs_ext, b_ext, c_ext)


def _mamba_chunk(rows, z_ref, xs_ref, b_ref, c_ref, dt_ref,
                 cwx_ref, cwb_ref, cwc_ref, cbx_ref, cbb_ref, cbc_ref,
                 dtb_ref, alog_ref, dfull_ref, nw_ref, expand2_ref, tril3_ref,
                 o_ref, state_scr, xs_ext, b_ext, c_ext):
    L = SSD_CHUNK

    def conv_silu(x_ref, carry, w_ref, bias_ref):
        x = x_ref[rows, :].astype(F32)
        first_row = lax.broadcasted_iota(jnp.int32, (HALO, x.shape[1]), 0) == 0
        r = None
        for k in range(CONV_WIDTH - 1):
            t = w_ref[k:k + 1, :] * x
            if r is not None:
                t = t + r
            rolled = pltpu.roll(t, 1, 0)
            top = jnp.where(first_row, carry[k:k + 1, :], rolled[0:HALO])
            carry[k:k + 1, :] = t[L - 1:L, :]
            r = jnp.concatenate([top, rolled[HALO:]], axis=0)
        return _silu(r + w_ref[CONV_WIDTH - 1:CONV_WIDTH, :] * x + bias_ref[...])

    xs = conv_silu(xs_ref, xs_ext, cwx_ref, cbx_ref)
    bm = conv_silu(b_ref, b_ext, cwb_ref, cbb_ref).astype(BF16)
    cm = conv_silu(c_ref, c_ext, cwc_ref, cbc_ref).astype(BF16)

    xdt_raw = dt_ref[rows, :] + dtb_ref[...]
    dt = jnp.maximum(xdt_raw, 0.0) + jnp.log1p(jnp.exp(-jnp.abs(xdt_raw)))
    adt = dt * (-jnp.exp(alog_ref[...]))
    acs = _dot(tril3_ref[...], jnp.concatenate(_split3(adt), axis=0)) * LOG2E
    acs_t = acs.T
    a_last = acs[L - 1:L, :]
    stacked = jnp.concatenate([dt, jnp.exp2(acs), jnp.exp2(a_last - acs)], axis=0)
    wide = _dot(jnp.concatenate(_split2(stacked), axis=1), expand2_ref[...])
    dt_w = wide[0:L]
    in_decay_w = wide[L:2 * L]
    out_decay_w = wide[2 * L:3 * L]

    xdt = xs * dt_w
    xst = xdt * out_decay_w
    causal = lax.broadcasted_iota(jnp.int32, (L, L), 0) >= lax.broadcasted_iota(jnp.int32, (L, L), 1)
    causal_bias = jnp.where(causal, 0.0, MASK_NEG)
    lane = lax.broadcasted_iota(jnp.int32, (L, LANES), 1)
    low_half = lane < SSM_HEAD_DIM
    chunk_decay = jnp.exp2(acs_t[:, L - 1:L])

    for g in range(SSM_GROUPS):
        gs = slice(g * GROUP_W, (g + 1) * GROUP_W)
        bg = bm[:, g * SSM_STATE:(g + 1) * SSM_STATE]
        cg = cm[:, g * SSM_STATE:(g + 1) * SSM_STATE]
        cb = _dot_nt(cg, bg)
        prev = state_scr[gs, :]
        y_off = _dot_nt(cg, prev.astype(BF16)) * in_decay_w[:, gs]
        pieces = []
        for hp in range(HEADS_PER_GROUP // 2):
            lhs = []
            for e in range(2):
                h = g * HEADS_PER_GROUP + 2 * hp + e
                decay = jnp.exp2(acs[:, h:h + 1] - acs_t[h:h + 1, :] + causal_bias)
                lhs.append((cb * decay).astype(BF16))
            slab = xdt[:, g * GROUP_W + hp * LANES:g * GROUP_W + (hp + 1) * LANES]
            rhs = jnp.concatenate([jnp.where(low_half, slab, 0.0), jnp.where(low_half, 0.0, slab)], axis=0)
            pieces.append(_dot(jnp.concatenate(lhs, axis=1), rhs.astype(BF16)))
        y_g = jnp.concatenate(pieces, axis=1) + y_off
        y_g = y_g + dfull_ref[:, gs] * xs[:, gs]
        y_g = y_g * _silu(z_ref[rows, gs].astype(F32))
        ms = jnp.mean(y_g * y_g, axis=-1, keepdims=True)
        o_ref[rows, gs] = (y_g * lax.rsqrt(ms + EPS) * nw_ref[:, gs]).astype(o_ref.dtype)
        new = _dot(xst[:, gs].T.astype(BF16), bg)
        cd = jnp.broadcast_to(chunk_decay[g * HEADS_PER_GROUP:(g + 1) * HEADS_PER_GROUP], (HEADS_PER_GROUP, LANES))
        cd = jnp.broadcast_to(cd[:, None, :], (HEADS_PER_GROUP, SSM_HEAD_DIM, LANES)).reshape(GROUP_W, LANES)
        state_scr[gs, :] = prev * cd + new


def _mamba(main, small, cwx, cwb, cwc, cbx, cbb, cbc, dtb, alog, dfull, nw, expand2, tril3, seq):
    L = MAMBA_ROWS
    bw = SSM_GROUPS * SSM_STATE
    full = lambda a: pl.BlockSpec(a.shape, lambda c: (0,) * a.ndim)
    consts = [cwx, cwb, cwc, cbx, cbb, cbc, dtb, alog, dfull, nw, expand2, tril3]
    return pl.pallas_call(
        _mamba_kernel,
        grid=(seq // L,),
        in_specs=[pl.BlockSpec((L, SSM_D_INNER), lambda c: (c, COL_Z // SSM_D_INNER)),
                  pl.BlockSpec((L, SSM_D_INNER), lambda c: (c, COL_XS // SSM_D_INNER)),
                  pl.BlockSpec((L, bw), lambda c: (c, COL_B // bw)),
                  pl.BlockSpec((L, bw), lambda c: (c, COL_C // bw)),
                  pl.BlockSpec((L, LANES), lambda c: (c, 0))] + [full(a) for a in consts],
        out_specs=pl.BlockSpec((L, SSM_D_INNER), lambda c: (c, 0)),
        out_shape=jax.ShapeDtypeStruct((seq, SSM_D_INNER), BF16),
        scratch_shapes=[pltpu.VMEM((SSM_D_INNER, SSM_STATE), F32),
                        pltpu.VMEM((HALO, SSM_D_INNER), F32),
                        pltpu.VMEM((HALO, bw), F32),
                        pltpu.VMEM((HALO, bw), F32)],
        compiler_params=_params(("arbitrary",)),
        name="mamba2_ssd",
    )(main, main, main, main, small, *consts)


def _merge_kernel(a_ref, b_ref, pa_ref, pb_ref, ga_ref, gb_ref, o_ref):
    ya = _dot(a_ref[...], pa_ref[...])
    yb = _dot(b_ref[...], pb_ref[...])
    o = jax.nn.sigmoid(ga_ref[...].astype(F32)) * ya + jax.nn.sigmoid(gb_ref[...].astype(F32)) * yb
    o_ref[...] = o.astype(o_ref.dtype)


def _merge(o_nsa, o_ssm, pa, pb, main, seq):
    tm = min(seq, 512)
    tn = 1024
    return pl.pallas_call(
        _merge_kernel,
        grid=(seq // tm, D_MODEL // tn),
        in_specs=[pl.BlockSpec((tm, NSA_HEADS * DH), lambda i, j: (i, 0)),
                  pl.BlockSpec((tm, SSM_D_INNER), lambda i, j: (i, 0)),
                  pl.BlockSpec((NSA_HEADS * DH, tn), lambda i, j: (0, j)),
                  pl.BlockSpec((SSM_D_INNER, tn), lambda i, j: (0, j)),
                  pl.BlockSpec((tm, tn), lambda i, j: (i, COL_GA // tn + j)),
                  pl.BlockSpec((tm, tn), lambda i, j: (i, COL_GB // tn + j))],
        out_specs=pl.BlockSpec((tm, tn), lambda i, j: (i, j)),
        out_shape=jax.ShapeDtypeStruct((seq, D_MODEL), BF16),
        compiler_params=_params(("arbitrary", "arbitrary")),
        name="gated_merge",
    )(o_nsa, o_ssm, pa, pb, main, main)


def _outproj_kernel(m_ref, w_ref, x_ref, nw_ref, h_ref, hn_ref):
    h = x_ref[...] + _dot(m_ref[...], w_ref[...])
    h_ref[...] = h
    ms = jnp.mean(h * h, axis=-1, keepdims=True)
    hn_ref[...] = (h * lax.rsqrt(ms + EPS) * nw_ref[...]).astype(hn_ref.dtype)


def _outproj(merged, w_out, x2, norm_w, seq):
    tm = min(seq, 512)
    row = pl.BlockSpec((tm, D_MODEL), lambda i: (i, 0))
    return pl.pallas_call(
        _outproj_kernel,
        grid=(seq // tm,),
        in_specs=[row, pl.BlockSpec((D_MODEL, D_MODEL), lambda i: (0, 0), pipeline_mode=pl.Buffered(1)), row,
                  pl.BlockSpec((1, D_MODEL), lambda i: (0, 0))],
        out_specs=[row, row],
        out_shape=[jax.ShapeDtypeStruct((seq, D_MODEL), F32), jax.ShapeDtypeStruct((seq, D_MODEL), BF16)],
        compiler_params=_params(("arbitrary",)),
        name="outproj_residual_norm",
    )(merged, w_out, x2, norm_w.reshape(1, D_MODEL))


def _mlp_kernel(hn_ref, wu_ref, wd_ref, h_ref, nw_ref, o_ref, acc_scr):
    f = pl.program_id(1)

    @pl.when(f == 0)
    def _():
        acc_scr[...] = jnp.zeros(acc_scr.shape, F32)

    up = jnp.maximum(_dot(hn_ref[...], wu_ref[...]), 0.0)
    acc_scr[...] += _dot((up * up).astype(BF16), wd_ref[...])

    @pl.when(f == pl.num_programs(1) - 1)
    def _():
        h = h_ref[...] + acc_scr[...]
        ms = jnp.mean(h * h, axis=-1, keepdims=True)
        o_ref[...] = h * lax.rsqrt(ms + EPS) * nw_ref[...]


def _mlp(hn, w_up, w_down, h1, norm_w, seq):
    tm = min(seq, 512)
    tf = 1024
    row = lambda i, f: (i, 0)
    return pl.pallas_call(
        _mlp_kernel,
        grid=(seq // tm, D_FF // tf),
        in_specs=[pl.BlockSpec((tm, D_MODEL), row),
                  pl.BlockSpec((D_MODEL, tf), lambda i, f: (0, f)),
                  pl.BlockSpec((tf, D_MODEL), lambda i, f: (f, 0)),
                  pl.BlockSpec((tm, D_MODEL), row),
                  pl.BlockSpec((1, D_MODEL), lambda i, f: (0, 0))],
        out_specs=pl.BlockSpec((tm, D_MODEL), row),
        out_shape=jax.ShapeDtypeStruct((seq, D_MODEL), F32),
        scratch_shapes=[pltpu.VMEM((tm, D_MODEL), F32)],
        compiler_params=_params(("arbitrary", "arbitrary")),
        name="mlp_final_norm",
    )(hn, w_up, w_down, h1, norm_w.reshape(1, D_MODEL))


def _cmp_to_sel(seq):
    n_cmp = (seq - CMP_BLOCK) // CMP_STRIDE + 1
    n_sel = seq // SEL_BLOCK
    c_start = np.arange(n_cmp) * CMP_STRIDE
    s_start = np.arange(n_sel) * SEL_BLOCK
    overlap = np.clip(np.minimum(c_start[:, None] + CMP_BLOCK, s_start[None, :] + SEL_BLOCK)
                      - np.maximum(c_start[:, None], s_start[None, :]), 0, None)
    m = np.zeros((seq // CMP_STRIDE, LANES), np.float32)
    m[:n_cmp, :n_sel] = overlap / CMP_STRIDE
    return jnp.asarray(m.T, BF16)


def kernel(x, positions, norm_mix_w, w_in, cmp_pos_k, cmp_pos_v, cmp_k_w1, cmp_k_w2, cmp_v_w1, cmp_v_w2, conv_w, conv_b, dt_bias, a_log, ssm_d, ssm_norm_w, w_proj_nsa, w_proj_ssm, w_out, norm_mlp_w, w_up, w_down, norm_final_w):
    bsz, seq, _ = x.shape
    assert bsz == 1 and seq % 1024 == 0 and SEL_TOPK <= seq // SEL_BLOCK <= LANES
    assert w_in.shape[0] == 1, "one layer"
    x2 = x.reshape(seq, D_MODEL)

    o_gate = NSA_HEADS * DH + 6 * NSA_GROUPS * DH
    o_z = o_gate + 3 * NSA_HEADS
    o_dt = o_z + 2 * SSM_D_INNER + 2 * SSM_GROUPS * SSM_STATE
    o_bg = o_dt + SSM_HEADS
    assert o_z == SRC_Z and o_bg == SRC_BG and o_gate == ATT_W
    w_t = jnp.swapaxes(w_in, 1, 2)[0]
    n_gate = 3 * NSA_HEADS
    w_small_t = jnp.concatenate([w_t[o_dt:o_dt + SSM_HEADS], w_t[o_gate:o_gate + n_gate],
                                 jnp.zeros((SMALL_W - SSM_HEADS - n_gate, D_MODEL), F32)], axis=0)

    pos = positions.reshape(seq)
    cosf, sinf = _rope_tables(pos)
    att, kc_chunks, vc_chunks, u = _inproj(x2, norm_mix_w[0], w_t, cosf, sinf)
    ssm = _proj_ssm(u, w_t)
    small = _proj_small(u, w_small_t)

    ncp = seq // CMP_STRIDE
    def pos2(p):
        p = p.reshape(2, CMP_STRIDE * DH)
        return jnp.concatenate([p, jnp.zeros((14, CMP_STRIDE * DH), F32)], axis=0).astype(BF16)
    pos_end = pos[CMP_BLOCK - 1::CMP_STRIDE]
    cos_c, sin_c = _rope_tables(jnp.concatenate([pos_end, jnp.zeros((ncp - pos_end.shape[0],), pos.dtype)]))
    kc, vc = _compress(kc_chunks, vc_chunks, pos2(cmp_pos_k[0]), pos2(cmp_pos_v[0]),
                       cmp_k_w1[0].astype(BF16), cmp_k_w2[0].astype(BF16),
                       cmp_v_w1[0].astype(BF16), cmp_v_w2[0].astype(BF16), cos_c, sin_c)

    o_cmp, sel_bias = _cmp_attn(att, kc, vc, _cmp_to_sel(seq), seq)
    o_nsa = _nsa_attn(att, small, sel_bias, o_cmp, seq)

    cw = conv_w[0]
    cb_ = conv_b[0].reshape(1, -1)
    lane_pad = lambda a: jnp.concatenate([a.reshape(1, -1), jnp.zeros((1, LANES - a.shape[-1]), F32)], axis=1)
    head_of = np.arange(SSM_D_INNER) // SSM_HEAD_DIM
    expand = np.arange(LANES)[:, None] == head_of[None, :]
    expand2 = jnp.asarray(np.concatenate([expand, expand], axis=0), BF16)
    tril = np.tril(np.ones((SSD_CHUNK, SSD_CHUNK), np.float32))
    tril3 = jnp.asarray(np.concatenate([tril, tril, tril], axis=1), BF16)
    o_ssm = _mamba(ssm, small, cw[:, :4096], cw[:, 4096:5120], cw[:, 5120:],
                   cb_[:, :4096], cb_[:, 4096:5120], cb_[:, 5120:],
                   lane_pad(dt_bias[0]), lane_pad(a_log[0]),
                   jnp.repeat(ssm_d[0], SSM_HEAD_DIM).reshape(1, -1), ssm_norm_w[0].reshape(1, -1),
                   expand2, tril3, seq)

    merged = _merge(o_nsa, o_ssm, w_proj_nsa[0].astype(BF16), w_proj_ssm[0].astype(BF16), ssm, seq)
    h1, hn = _outproj(merged, w_out[0].astype(BF16), x2, norm_mlp_w[0], seq)
    out = _mlp(hn, w_up[0].astype(BF16), w_down[0].astype(BF16), h1, norm_final_w, seq)
    return out.reshape(bsz, seq, D_MODEL)
```

```python
import math

import jax
import jax.numpy as jnp
import numpy as np
from jax import lax
from jax.experimental import pallas as pl
from jax.experimental.pallas import tpu as pltpu

F32 = jnp.float32
BF16 = jnp.bfloat16

D_MODEL = 2048
NSA_HEADS = 16
NSA_GROUPS = 4
NSA_REP = NSA_HEADS // NSA_GROUPS
DH = 128
CMP_BLOCK = 32
CMP_STRIDE = 16
SEL_BLOCK = 64
SEL_TOPK = 16
WINDOW = 512
ROPE_THETA = 10000.0
N_FORCED = 3
SSM_D_INNER = 4096
SSM_HEAD_DIM = 64
SSM_HEADS = 64
SSM_GROUPS = 8
SSM_STATE = 128
CONV_WIDTH = 4
SSD_CHUNK = 128
D_FF = 4 * D_MODEL
EPS = 1e-6
NEG_INF = -1e30
TINY = 1e-30
MASK_NEG = NEG_INF
TAKEN = -2.0
LOG2E = math.log2(math.e)
LANES = 128
VMEM_LIMIT = 56 * 1024 * 1024

COL_Q = 0
COL_KC = 2048
COL_VC = 2560
COL_KS = 3072
COL_VS = 3584
COL_KW = 4096
COL_VW = 4608
ATT_W = 5120
COL_Z = 0
COL_XS = 4096
COL_B = 8192
COL_C = 9216
COL_GA = 10240
COL_GB = 12288
SSM_W = 14336
SMALL_W = 128


def _dot(a, b):
    return jnp.dot(a, b, preferred_element_type=F32)


def _dot_nt(a, b):
    return lax.dot_general(a, b, (((1,), (1,)), ((), ())), preferred_element_type=F32)


def _params(sem):
    return pltpu.CompilerParams(dimension_semantics=sem, vmem_limit_bytes=VMEM_LIMIT)


def _rope(x, cos, sin_signed):
    return x * cos + pltpu.roll(x, DH // 2, 1) * sin_signed


def _split2(x):
    hi = x.astype(BF16)
    lo = (x - hi.astype(F32)).astype(BF16)
    return hi, lo


def _split3(x):
    hi = x.astype(BF16)
    r = x - hi.astype(F32)
    mid = r.astype(BF16)
    lo = (r - mid.astype(F32)).astype(BF16)
    return hi, mid, lo


def _rope_table_kernel(pos_ref, inv_ref, sign_ref, cos_ref, sin_ref):
    ang = pos_ref[...].astype(F32) * inv_ref[...]
    cos_ref[...] = jnp.cos(ang)
    sin_ref[...] = jnp.sin(ang) * sign_ref[...]


def _rope_tables(positions):
    seq = positions.shape[0]
    half = DH // 2
    inv = ROPE_THETA ** (-jnp.arange(0, DH, 2, dtype=F32) / DH)
    inv_full = jnp.concatenate([inv, inv]).reshape(1, DH)
    sign = jnp.concatenate([-jnp.ones((half,), F32), jnp.ones((half,), F32)]).reshape(1, DH)
    tm = math.gcd(seq, 1024)
    return pl.pallas_call(
        _rope_table_kernel,
        grid=(seq // tm,),
        in_specs=[pl.BlockSpec((tm, 1), lambda i: (i, 0)),
                  pl.BlockSpec((1, DH), lambda i: (0, 0)),
                  pl.BlockSpec((1, DH), lambda i: (0, 0))],
        out_specs=[pl.BlockSpec((tm, DH), lambda i: (i, 0))] * 2,
        out_shape=[jax.ShapeDtypeStruct((seq, DH), F32)] * 2,
        compiler_params=_params(("arbitrary",)),
        name="rope_tables",
    )(positions.reshape(seq, 1), inv_full, sign)


INPROJ_TN = 512
Q_TILE_LO = COL_Q // INPROJ_TN
Q_TILE_HI = COL_KC // INPROJ_TN
KC_TILE = COL_KC // INPROJ_TN
VC_TILE = COL_VC // INPROJ_TN
KS_TILE = COL_KS // INPROJ_TN
KW_TILE = COL_KW // INPROJ_TN


def _inproj_kernel(x_ref, nw_ref, w_ref, cos_ref, sin_ref, o_ref, kcx_ref, vcx_ref, u_ref, acc_scr):
    j = pl.program_id(1)

    @pl.when(j == 0)
    def _():
        xf = x_ref[...]
        ms = jnp.mean(xf * xf, axis=-1, keepdims=True)
        u_ref[...] = (xf * lax.rsqrt(ms + EPS) * nw_ref[...]).astype(BF16)

    acc = _dot_nt(u_ref[...], w_ref[...].astype(BF16))
    is_q = jnp.logical_and(j >= Q_TILE_LO, j < Q_TILE_HI)
    is_rope = is_q | (j == KS_TILE) | (j == KW_TILE)

    @pl.when(jnp.logical_not(is_rope))
    def _():
        o_ref[...] = acc.astype(o_ref.dtype)

    @pl.when(is_rope)
    def _():
        cos = cos_ref[...]
        sin = sin_ref[...]
        scale = jnp.where(is_q, DH ** -0.5 * LOG2E, 1.0).astype(F32)
        for h in range(INPROJ_TN // DH):
            xh = acc[:, h * DH:(h + 1) * DH]
            o_ref[:, h * DH:(h + 1) * DH] = (_rope(xh, cos, sin) * scale).astype(o_ref.dtype)

    def chunked(dst_ref):
        n_chunks = acc_scr.shape[1] // CMP_STRIDE
        for g in range(NSA_GROUPS):
            acc_scr[g] = acc[:, g * DH:(g + 1) * DH]
            for l in range(CMP_STRIDE):
                rows_l = acc_scr[g, pl.ds(l, n_chunks, stride=CMP_STRIDE), :]
                dst_ref[g, :, l * DH:(l + 1) * DH] = rows_l.astype(dst_ref.dtype)

    @pl.when(j == KC_TILE)
    def _():
        chunked(kcx_ref)

    @pl.when(j == VC_TILE)
    def _():
        chunked(vcx_ref)


def _inproj(x2, norm_w, w_t, cosf, sinf):
    seq = x2.shape[0]
    tm = min(seq, 1024)
    tn = INPROJ_TN
    chunk_w = CMP_STRIDE * DH
    chunk_spec = pl.BlockSpec((NSA_GROUPS, tm // CMP_STRIDE, chunk_w), lambda i, j: (0, i, 0))
    chunk_shape = jax.ShapeDtypeStruct((NSA_GROUPS, seq // CMP_STRIDE, chunk_w), BF16)
    return pl.pallas_call(
        _inproj_kernel,
        grid=(seq // tm, ATT_W // tn),
        in_specs=[pl.BlockSpec((tm, D_MODEL), lambda i, j: (i, 0)),
                  pl.BlockSpec((1, D_MODEL), lambda i, j: (0, 0)),
                  pl.BlockSpec((tn, D_MODEL), lambda i, j: (j, 0)),
                  pl.BlockSpec((tm, DH), lambda i, j: (i, 0)),
                  pl.BlockSpec((tm, DH), lambda i, j: (i, 0))],
        out_specs=[pl.BlockSpec((tm, tn), lambda i, j: (i, j)),
                   chunk_spec, chunk_spec,
                   pl.BlockSpec((tm, D_MODEL), lambda i, j: (i, 0))],
        out_shape=[jax.ShapeDtypeStruct((seq, ATT_W), BF16),
                   chunk_shape, chunk_shape,
                   jax.ShapeDtypeStruct((seq, D_MODEL), BF16)],
        scratch_shapes=[pltpu.VMEM((NSA_GROUPS, tm, DH), F32)],
        compiler_params=_params(("arbitrary", "arbitrary")),
        name="inproj_attn",
    )(x2, norm_w.reshape(1, D_MODEL), w_t, cosf, sinf)


SSM_TN = 1024
SRC_Z = NSA_HEADS * DH + 6 * NSA_GROUPS * DH + 3 * NSA_HEADS
SRC_BG = SRC_Z + 2 * SSM_D_INNER + 2 * SSM_GROUPS * SSM_STATE + SSM_HEADS
SSM_SPLIT = COL_GA // SSM_TN


def _nt_matmul_kernel(a_ref, w_ref, o_ref):
    o_ref[...] = _dot_nt(a_ref[...], w_ref[...].astype(BF16)).astype(o_ref.dtype)


def _proj_ssm(u, w_t):
    seq, k = u.shape
    tm = min(seq, 2048)

    def src_rows(i, j):
        jump = (SRC_BG - SRC_Z - SSM_TN * SSM_SPLIT) // 16
        return ((SRC_Z // 16 + (SSM_TN // 16) * j + jump * (j // SSM_SPLIT)) * 16, 0)

    return pl.pallas_call(
        _nt_matmul_kernel,
        grid=(seq // tm, SSM_W // SSM_TN),
        in_specs=[pl.BlockSpec((tm, k), lambda i, j: (i, 0)),
                  pl.BlockSpec((pl.Element(SSM_TN), pl.Element(k)), src_rows)],
        out_specs=pl.BlockSpec((tm, SSM_TN), lambda i, j: (i, j)),
        out_shape=jax.ShapeDtypeStruct((seq, SSM_W), BF16),
        compiler_params=_params(("arbitrary", "arbitrary")),
        name="inproj_ssm",
    )(u, w_t)


def _proj_small(u, w_small_t):
    seq, k = u.shape
    n = w_small_t.shape[0]
    tm = min(seq, 1024)
    return pl.pallas_call(
        _nt_matmul_kernel,
        grid=(seq // tm,),
        in_specs=[pl.BlockSpec((tm, k), lambda i: (i, 0)),
                  pl.BlockSpec((n, k), lambda i: (0, 0))],
        out_specs=pl.BlockSpec((tm, n), lambda i: (i, 0)),
        out_shape=jax.ShapeDtypeStruct((seq, n), F32),
        compiler_params=_params(("arbitrary",)),
        name="inproj_small",
    )(u, w_small_t)


def _compress_kernel(xk_ref, xv_ref, pk_ref, pv_ref, w1k_ref, w2k_ref, w1v_ref, w2v_ref,
                     cos_ref, sin_ref, kc_ref, vc_ref):
    half = CMP_STRIDE * DH
    ncp = xk_ref.shape[1]

    def comp(x_ref, p_ref, w1_ref, w2_ref):
        x = x_ref[0]
        wt = w1_ref[0:half, :]
        wb = w1_ref[half:2 * half, :]
        a = _dot(x, wt)
        b = _dot(x, wb)
        pb = _dot(p_ref[...], wt)[0:1] + _dot(p_ref[...], wb)[1:2]
        h = a + pltpu.roll(b, ncp - 1, 0) + pb
        h = h * jax.nn.sigmoid(h)
        return _dot(h.astype(BF16), w2_ref[...])

    kc = comp(xk_ref, pk_ref, w1k_ref, w2k_ref)
    kc_ref[0] = _rope(kc, cos_ref[...], sin_ref[...]).astype(BF16)
    vc_ref[0] = comp(xv_ref, pv_ref, w1v_ref, w2v_ref).astype(BF16)


def _compress(xk, xv, pk, pv, w1k, w2k, w1v, w2v, cos_c, sin_c):
    g, ncp, width = xk.shape
    full = lambda shape: pl.BlockSpec(shape, lambda i: (0,) * len(shape))
    per_g = pl.BlockSpec((1, ncp, width), lambda i: (i, 0, 0))
    out_g = pl.BlockSpec((1, ncp, DH), lambda i: (i, 0, 0))
    return pl.pallas_call(
        _compress_kernel,
        grid=(g,),
        in_specs=[per_g, per_g, full(pk.shape), full(pv.shape), full(w1k.shape), full(w2k.shape),
                  full(w1v.shape), full(w2v.shape), full(cos_c.shape), full(sin_c.shape)],
        out_specs=[out_g, out_g],
        out_shape=[jax.ShapeDtypeStruct((g, ncp, DH), BF16)] * 2,
        compiler_params=_params(("arbitrary",)),
        name="compress",
    )(xk, xv, pk, pv, w1k, w2k, w1v, w2v, cos_c, sin_c)


CMP_TQ = 256


def _cmp_attn_kernel(q_ref, kc_ref, vc_ref, mt_ref, oc_ref, mask_ref):
    i = pl.program_id(0)
    qbase = i * CMP_TQ
    ncp = kc_ref.shape[1]
    nsel = mt_ref.shape[0]
    row = lax.broadcasted_iota(jnp.int32, (CMP_TQ, ncp), 0)
    col = lax.broadcasted_iota(jnp.int32, (CMP_TQ, ncp), 1)
    bias1 = jnp.where((col * CMP_STRIDE + (CMP_BLOCK - 1)) <= qbase + row, 0.0, MASK_NEG)
    bias = jnp.concatenate([bias1] * NSA_REP, axis=0)
    has_key = jnp.concatenate([bias1[:, 0:1] == 0.0] * NSA_REP, axis=0)
    blk = lax.broadcasted_iota(jnp.int32, (nsel, CMP_TQ), 0)
    cur = (qbase + lax.broadcasted_iota(jnp.int32, (nsel, CMP_TQ), 1)) >> 6
    future = blk > cur
    forced = (blk == 0) | (blk == cur) | (blk == cur - 1)
    blk_f = blk.astype(F32)
    mt = mt_ref[...]
    importance = []
    for g in range(NSA_GROUPS):
        qg = jnp.concatenate(
            [q_ref[:, (g * NSA_REP + r) * DH:(g * NSA_REP + r + 1) * DH] for r in range(NSA_REP)], axis=0)
        s = _dot_nt(qg, kc_ref[g]) + bias
        p = jnp.exp2(s - jnp.max(s, axis=-1, keepdims=True)).astype(BF16)
        ov = _dot(p, jnp.concatenate([vc_ref[g], jnp.ones((ncp, LANES), BF16)], axis=1))
        inv = jnp.where(has_key, 1.0 / ov[:, DH:], 0.0)
        o = ov[:, :DH] * inv
        imp = None
        for r in range(NSA_REP):
            h = g * NSA_REP + r
            rows_r = slice(r * CMP_TQ, (r + 1) * CMP_TQ)
            oc_ref[:, h * DH:(h + 1) * DH] = o[rows_r].astype(oc_ref.dtype)
            part = _dot_nt(p[rows_r], mt) * inv[rows_r]
            imp = part if imp is None else imp + part
        importance.append(imp.T)
    v = jnp.where(future | forced, -1.0, jnp.stack(importance))
    for _ in range(SEL_TOPK - N_FORCED):
        mx = jnp.max(v, axis=1, keepdims=True)
        first = jnp.min(jnp.where(v == mx, blk_f, float(nsel)), axis=1, keepdims=True)
        v = jnp.where(blk_f == first, TAKEN, v)
    bias_t = jnp.where(forced | (v == TAKEN), 0.0, MASK_NEG)
    for g in range(NSA_GROUPS):
        mask_ref[g] = bias_t[g].T.astype(mask_ref.dtype)


def _cmp_attn(main, kc, vc, mt, seq):
    nsel = mt.shape[0]
    ncp = kc.shape[1]
    return pl.pallas_call(
        _cmp_attn_kernel,
        grid=(seq // CMP_TQ,),
        in_specs=[pl.BlockSpec((CMP_TQ, NSA_HEADS * DH), lambda i: (i, COL_Q // (NSA_HEADS * DH))),
                  pl.BlockSpec((NSA_GROUPS, ncp, DH), lambda i: (0, 0, 0)),
                  pl.BlockSpec((NSA_GROUPS, ncp, DH), lambda i: (0, 0, 0)),
                  pl.BlockSpec((nsel, ncp), lambda i: (0, 0))],
        out_specs=[pl.BlockSpec((CMP_TQ, NSA_HEADS * DH), lambda i: (i, 0)),
                   pl.BlockSpec((NSA_GROUPS, CMP_TQ, nsel), lambda i: (0, i, 0))],
        out_shape=[jax.ShapeDtypeStruct((seq, NSA_HEADS * DH), BF16),
                   jax.ShapeDtypeStruct((NSA_GROUPS, seq, nsel), BF16)],
        compiler_params=_params(("arbitrary",)),
        name="cmp_attn_topk",
    )(main, kc, vc, mt)


ATT_TQ = 512
ATT_TK = 512
WIN_SPAN = WINDOW + ATT_TQ // 2


def _nsa_attn_kernel(q_ref, mask_ref, ks_ref, vs_ref, kw_ref, vw_ref, oc_ref, gate_ref, *rest):
    n_w = (len(rest) - 4) // 2
    w_refs, o_ref, wb_refs = rest[:n_w], rest[n_w], rest[n_w + 1:2 * n_w + 1]
    qa_scr, acc_scr, m_scr = rest[2 * n_w + 1:]
    for src, dst in zip(w_refs, wb_refs):
        dst[...] = src[...].astype(dst.dtype)
    i = pl.program_id(1)
    qbase = i * ATT_TQ
    assert mask_ref.shape[2] == LANES, "the block-selection bias rides in one extra 128-wide contraction slab"
    bias = mask_ref[0]
    for r in range(NSA_REP):
        qa_scr[r] = jnp.concatenate([q_ref[:, r * DH:(r + 1) * DH], bias], axis=1)
    m_scr[...] = jnp.full(m_scr.shape, NEG_INF, F32)
    acc_scr[...] = jnp.zeros(acc_scr.shape, F32)
    krow = lax.broadcasted_iota(jnp.int32, (ATT_TK, LANES), 0)
    kblk = lax.broadcasted_iota(jnp.int32, (ATT_TK, LANES), 1)

    half = ATT_TQ // 2

    def key_tile(j):
        k0 = pl.multiple_of(j * ATT_TK, ATT_TK)
        onehot = jnp.where(((krow + k0) >> 6) == kblk, 1.0, 0.0).astype(BF16)
        k_aug = jnp.concatenate([ks_ref[pl.ds(k0, ATT_TK), :], onehot], axis=1)
        v_aug = jnp.concatenate([vs_ref[pl.ds(k0, ATT_TK), :], jnp.ones((ATT_TK, LANES), BF16)], axis=1)
        return k_aug, v_aug

    def online_update(r, rows, s, v_aug):
        m_prev = m_scr[r, rows, :]
        m_new = jnp.maximum(m_prev, jnp.max(s, axis=-1, keepdims=True))
        alpha = jnp.exp2(m_prev - m_new)
        p = jnp.exp2(s - jnp.concatenate([m_new] * (s.shape[1] // LANES), axis=1))
        acc_scr[r, rows, :] = (acc_scr[r, rows, :] * jnp.concatenate([alpha, alpha], axis=1)
                               + _dot(p.astype(BF16), v_aug))
        m_scr[r, rows, :] = m_new

    def step(j, diagonal):
        k_aug, v_aug = key_tile(j)
        if not diagonal:
            for r in range(NSA_REP):
                online_update(r, slice(0, ATT_TQ), _dot_nt(qa_scr[r], k_aug), v_aug)
            return
        r_ = lax.broadcasted_iota(jnp.int32, (half, half), 0)
        c_ = lax.broadcasted_iota(jnp.int32, (half, half), 1)
        tri_bias = jnp.where(c_ <= r_, 0.0, MASK_NEG)
        for r in range(NSA_REP):
            s0 = _dot_nt(qa_scr[r, 0:half, :], k_aug[:half]) + tri_bias
            online_update(r, slice(0, half), s0, v_aug[:half])
            s1 = _dot_nt(qa_scr[r, half:ATT_TQ, :], k_aug)
            s1 = jnp.concatenate([s1[:, :half], s1[:, half:] + tri_bias], axis=1)
            online_update(r, slice(half, ATT_TQ), s1, v_aug)

    def body(jj, carry):
        for u in range(4):
            step(4 * jj + u, False)
        return carry

    lax.fori_loop(0, i // 4, body, 0)
    done = (i // 4) * 4

    @pl.when((i & 2) != 0)
    def _():
        step(done, False)
        step(done + 1, False)

    @pl.when((i & 1) != 0)
    def _():
        step(i - 1, False)

    step(i, True)

    first_gate = SSM_HEADS + 3 * NSA_REP * pl.program_id(0)
    gates = jax.nn.sigmoid(pltpu.roll(gate_ref[...], SMALL_W - first_gate, 1))
    r_ = lax.broadcasted_iota(jnp.int32, (half, WIN_SPAN), 0)
    c_ = lax.broadcasted_iota(jnp.int32, (half, WIN_SPAN), 1)
    for hh in range(2):
        rows = slice(hh * half, (hh + 1) * half)
        q0 = qbase + hh * half
        start = pl.multiple_of(jnp.maximum(q0 - WINDOW, 0), half)
        kw = kw_ref[pl.ds(start, WIN_SPAN), :]
        vw_aug = jnp.concatenate([vw_ref[pl.ds(start, WIN_SPAN), :], jnp.ones((WIN_SPAN, LANES), BF16)], axis=1)
        diff = (q0 + r_) - (start + c_)
        win_bias = jnp.where((diff >= 0) & (diff < WINDOW), 0.0, MASK_NEG)
        for r in range(NSA_REP):
            s = _dot_nt(q_ref[rows, r * DH:(r + 1) * DH], kw) + win_bias
            p = jnp.exp2(s - jnp.max(s, axis=-1, keepdims=True))
            ow = _dot(p.astype(BF16), vw_aug)
            o_win = ow[:, :DH] / jnp.maximum(ow[:, DH:], TINY)
            acc = acc_scr[r, rows, :]
            o_sel = acc[:, :DH] / jnp.maximum(acc[:, DH:], TINY)
            g_r = gates[rows]
            o = (g_r[:, 3 * r:3 * r + 1] * oc_ref[rows, r * DH:(r + 1) * DH].astype(F32)
                 + g_r[:, 3 * r + 1:3 * r + 2] * o_sel
                 + g_r[:, 3 * r + 2:3 * r + 3] * o_win)
            o_ref[rows, r * DH:(r + 1) * DH] = o.astype(o_ref.dtype)


def _nsa_attn(main, small, mask, o_cmp, weights, seq):
    gw = NSA_REP * DH
    nsel = mask.shape[2]
    n_i = seq // ATT_TQ
    steps = NSA_GROUPS * n_i
    kv_spec = lambda col: pl.BlockSpec((seq, DH), lambda g, i: (0, col // DH + g))
    w_in_specs, w_out_specs, w_out_shapes = [], [], []
    for w in weights:
        _, rows, cols = w.shape
        slab = rows // steps
        assert slab * steps == rows and slab % 16 == 0
        w_in_specs.append(pl.BlockSpec((None, slab, cols), lambda g, i: (0, g * n_i + i, 0)))
        w_out_specs.append(pl.BlockSpec((slab, cols), lambda g, i: (g * n_i + i, 0)))
        w_out_shapes.append(jax.ShapeDtypeStruct((rows, cols), BF16))
    return pl.pallas_call(
        _nsa_attn_kernel,
        grid=(NSA_GROUPS, n_i),
        in_specs=[pl.BlockSpec((ATT_TQ, gw), lambda g, i: (i, COL_Q // gw + g)),
                  pl.BlockSpec((1, ATT_TQ, nsel), lambda g, i: (g, i, 0)),
                  kv_spec(COL_KS), kv_spec(COL_VS), kv_spec(COL_KW), kv_spec(COL_VW),
                  pl.BlockSpec((ATT_TQ, gw), lambda g, i: (i, g)),
                  pl.BlockSpec((ATT_TQ, SMALL_W), lambda g, i: (i, 0))] + w_in_specs,
        out_specs=[pl.BlockSpec((ATT_TQ, gw), lambda g, i: (i, g))] + w_out_specs,
        out_shape=[jax.ShapeDtypeStruct((seq, NSA_HEADS * DH), BF16)] + w_out_shapes,
        scratch_shapes=[pltpu.VMEM((NSA_REP, ATT_TQ, 2 * DH), BF16),
                        pltpu.VMEM((NSA_REP, ATT_TQ, 2 * DH), F32),
                        pltpu.VMEM((NSA_REP, ATT_TQ, LANES), F32)],
        compiler_params=_params(("arbitrary", "arbitrary")),
        name="nsa_sel_win_attn",
    )(main, mask, main, main, main, main, o_cmp, small, *weights)


HALO = 8
GROUP_W = SSM_D_INNER // SSM_GROUPS
HEADS_PER_GROUP = SSM_HEADS // SSM_GROUPS
MAMBA_ROWS = 2 * SSD_CHUNK


def _silu(x):
    h = 0.5 * x
    return h + h * jnp.tanh(h)


def _mamba_kernel(z_ref, xs_ref, b_ref, c_ref, dt_ref,
                  cwx_ref, cwb_ref, cwc_ref, cbx_ref, cbb_ref, cbc_ref,
                  dtb_ref, alog_ref, dfull_ref, nw_ref, expand2_ref, tril3_ref,
                  o_ref, state_scr, xs_ext, b_ext, c_ext):
    L = SSD_CHUNK

    @pl.when(pl.program_id(0) == 0)
    def _():
        state_scr[...] = jnp.zeros(state_scr.shape, F32)
        for carry in (xs_ext, b_ext, c_ext):
            carry[...] = jnp.zeros(carry.shape, F32)

    for sub in range(z_ref.shape[0] // L):
        _mamba_chunk(slice(sub * L, (sub + 1) * L), z_ref, xs_ref, b_ref, c_ref, dt_ref,
                     cwx_ref, cwb_ref, cwc_ref, cbx_ref, cbb_ref, cbc_ref,
                     dtb_ref, alog_ref, dfull_ref, nw_ref, expand2_ref, tril3_ref,
                     o_ref, state_scr, xs_ext, b_ext, c_ext)


def _mamba_chunk(rows, z_ref, xs_ref, b_ref, c_ref, dt_ref,
                 cwx_ref, cwb_ref, cwc_ref, cbx_ref, cbb_ref, cbc_ref,
                 dtb_ref, alog_ref, dfull_ref, nw_ref, expand2_ref, tril3_ref,
                 o_ref, state_scr, xs_ext, b_ext, c_ext):
    L = SSD_CHUNK

    def conv_silu(x_ref, carry, w_ref, bias_ref):
        x = x_ref[rows, :].astype(F32)
        first_row = lax.broadcasted_iota(jnp.int32, (HALO, x.shape[1]), 0) == 0
        r = None
        for k in range(CONV_WIDTH - 1):
            t = w_ref[k:k + 1, :] * x
            if r is not None:
                t = t + r
            rolled = pltpu.roll(t, 1, 0)
            top = jnp.where(first_row, carry[k:k + 1, :], rolled[0:HALO])
            carry[k:k + 1, :] = t[L - 1:L, :]
            r = jnp.concatenate([top, rolled[HALO:]], axis=0)
        return _silu(r + w_ref[CONV_WIDTH - 1:CONV_WIDTH, :] * x + bias_ref[...])

    xs = conv_silu(xs_ref, xs_ext, cwx_ref, cbx_ref)
    bm = conv_silu(b_ref, b_ext, cwb_ref, cbb_ref).astype(BF16)
    cm = conv_silu(c_ref, c_ext, cwc_ref, cbc_ref).astype(BF16)

    xdt_raw = dt_ref[rows, :] + dtb_ref[...]
    dt = jnp.maximum(xdt_raw, 0.0) + jnp.log1p(jnp.exp(-jnp.abs(xdt_raw)))
    adt = dt * (-jnp.exp(alog_ref[...]))
    acs = _dot(tril3_ref[...], jnp.concatenate(_split3(adt), axis=0)) * LOG2E
    acs_t = acs.T
    a_last = acs[L - 1:L, :]
    stacked = jnp.concatenate([dt, jnp.exp2(acs), jnp.exp2(a_last - acs)], axis=0)
    wide = _dot(jnp.concatenate(_split2(stacked), axis=1), expand2_ref[...])
    dt_w = wide[0:L]
    in_decay_w = wide[L:2 * L]
    out_decay_w = wide[2 * L:3 * L]

    xdt = xs * dt_w
    xst = xdt * out_decay_w
    causal = lax.broadcasted_iota(jnp.int32, (L, L), 0) >= lax.broadcasted_iota(jnp.int32, (L, L), 1)
    causal_bias = jnp.where(causal, 0.0, MASK_NEG)
    lane = lax.broadcasted_iota(jnp.int32, (L, LANES), 1)
    low_half = lane < SSM_HEAD_DIM
    chunk_decay = jnp.exp2(acs_t[:, L - 1:L])

    for g in range(SSM_GROUPS):
        gs = slice(g * GROUP_W, (g + 1) * GROUP_W)
        bg = bm[:, g * SSM_STATE:(g + 1) * SSM_STATE]
        cg = cm[:, g * SSM_STATE:(g + 1) * SSM_STATE]
        cb = _dot_nt(cg, bg)
        prev = state_scr[gs, :]
        y_off = _dot_nt(cg, prev.astype(BF16)) * in_decay_w[:, gs]
        pieces = []
        for hp in range(HEADS_PER_GROUP // 2):
            lhs = []
            for e in range(2):
                h = g * HEADS_PER_GROUP + 2 * hp + e
                decay = jnp.exp2(acs[:, h:h + 1] - acs_t[h:h + 1, :] + causal_bias)
                lhs.append((cb * decay).astype(BF16))
            slab = xdt[:, g * GROUP_W + hp * LANES:g * GROUP_W + (hp + 1) * LANES]
            rhs = jnp.concatenate([jnp.where(low_half, slab, 0.0), jnp.where(low_half, 0.0, slab)], axis=0)
            pieces.append(_dot(jnp.concatenate(lhs, axis=1), rhs.astype(BF16)))
        y_g = jnp.concatenate(pieces, axis=1) + y_off
        y_g = y_g + dfull_ref[:, gs] * xs[:, gs]
        y_g = y_g * _silu(z_ref[rows, gs].astype(F32))
        ms = jnp.mean(y_g * y_g, axis=-1, keepdims=True)
        o_ref[rows, gs] = (y_g * lax.rsqrt(ms + EPS) * nw_ref[:, gs]).astype(o_ref.dtype)
        new = _dot(xst[:, gs].T.astype(BF16), bg)
        cd = jnp.broadcast_to(chunk_decay[g * HEADS_PER_GROUP:(g + 1) * HEADS_PER_GROUP], (HEADS_PER_GROUP, LANES))
        cd = jnp.broadcast_to(cd[:, None, :], (HEADS_PER_GROUP, SSM_HEAD_DIM, LANES)).reshape(GROUP_W, LANES)
        state_scr[gs, :] = prev * cd + new


def _mamba(main, small, cwx, cwb, cwc, cbx, cbb, cbc, dtb, alog, dfull, nw, expand2, tril3, seq):
    L = MAMBA_ROWS
    bw = SSM_GROUPS * SSM_STATE
    full = lambda a: pl.BlockSpec(a.shape, lambda c: (0,) * a.ndim)
    consts = [cwx, cwb, cwc, cbx, cbb, cbc, dtb, alog, dfull, nw, expand2, tril3]
    return pl.pallas_call(
        _mamba_kernel,
        grid=(seq // L,),
        in_specs=[pl.BlockSpec((L, SSM_D_INNER), lambda c: (c, COL_Z // SSM_D_INNER)),
                  pl.BlockSpec((L, SSM_D_INNER), lambda c: (c, COL_XS // SSM_D_INNER)),
                  pl.BlockSpec((L, bw), lambda c: (c, COL_B // bw)),
                  pl.BlockSpec((L, bw), lambda c: (c, COL_C // bw)),
                  pl.BlockSpec((L, LANES), lambda c: (c, 0))] + [full(a) for a in consts],
        out_specs=pl.BlockSpec((L, SSM_D_INNER), lambda c: (c, 0)),
        out_shape=jax.ShapeDtypeStruct((seq, SSM_D_INNER), BF16),
        scratch_shapes=[pltpu.VMEM((SSM_D_INNER, SSM_STATE), F32),
                        pltpu.VMEM((HALO, SSM_D_INNER), F32),
                        pltpu.VMEM((HALO, bw), F32),
                        pltpu.VMEM((HALO, bw), F32)],
        compiler_params=_params(("arbitrary",)),
        name="mamba2_ssd",
    )(main, main, main, main, small, *consts)


def _merge_kernel(a_ref, b_ref, pa_ref, pb_ref, ga_ref, gb_ref, o_ref):
    ya = _dot(a_ref[...], pa_ref[...])
    yb = _dot(b_ref[...], pb_ref[...])
    o = jax.nn.sigmoid(ga_ref[...].astype(F32)) * ya + jax.nn.sigmoid(gb_ref[...].astype(F32)) * yb
    o_ref[...] = o.astype(o_ref.dtype)


def _merge(o_nsa, o_ssm, pa, pb, main, seq):
    tm = min(seq, 512)
    tn = 1024
    return pl.pallas_call(
        _merge_kernel,
        grid=(seq // tm, D_MODEL // tn),
        in_specs=[pl.BlockSpec((tm, NSA_HEADS * DH), lambda i, j: (i, 0)),
                  pl.BlockSpec((tm, SSM_D_INNER), lambda i, j: (i, 0)),
                  pl.BlockSpec((NSA_HEADS * DH, tn), lambda i, j: (0, j)),
                  pl.BlockSpec((SSM_D_INNER, tn), lambda i, j: (0, j)),
                  pl.BlockSpec((tm, tn), lambda i, j: (i, COL_GA // tn + j)),
                  pl.BlockSpec((tm, tn), lambda i, j: (i, COL_GB // tn + j))],
        out_specs=pl.BlockSpec((tm, tn), lambda i, j: (i, j)),
        out_shape=jax.ShapeDtypeStruct((seq, D_MODEL), BF16),
        compiler_params=_params(("arbitrary", "arbitrary")),
        name="gated_merge",
    )(o_nsa, o_ssm, pa, pb, main, main)


def _outproj_kernel(m_ref, w_ref, x_ref, nw_ref, h_ref, hn_ref):
    h = x_ref[...] + _dot(m_ref[...], w_ref[...])
    h_ref[...] = h
    ms = jnp.mean(h * h, axis=-1, keepdims=True)
    hn_ref[...] = (h * lax.rsqrt(ms + EPS) * nw_ref[...]).astype(hn_ref.dtype)


def _outproj(merged, w_out, x2, norm_w, seq):
    tm = min(seq, 512)
    row = pl.BlockSpec((tm, D_MODEL), lambda i: (i, 0))
    return pl.pallas_call(
        _outproj_kernel,
        grid=(seq // tm,),
        in_specs=[row, pl.BlockSpec((D_MODEL, D_MODEL), lambda i: (0, 0), pipeline_mode=pl.Buffered(1)), row,
                  pl.BlockSpec((1, D_MODEL), lambda i: (0, 0))],
        out_specs=[row, row],
        out_shape=[jax.ShapeDtypeStruct((seq, D_MODEL), F32), jax.ShapeDtypeStruct((seq, D_MODEL), BF16)],
        compiler_params=_params(("arbitrary",)),
        name="outproj_residual_norm",
    )(merged, w_out, x2, norm_w.reshape(1, D_MODEL))


def _mlp_kernel(hn_ref, wu_ref, wd_ref, h_ref, nw_ref, o_ref, acc_scr):
    f = pl.program_id(1)

    @pl.when(f == 0)
    def _():
        acc_scr[...] = jnp.zeros(acc_scr.shape, F32)

    up = jnp.maximum(_dot(hn_ref[...], wu_ref[...]), 0.0)
    acc_scr[...] += _dot((up * up).astype(BF16), wd_ref[...])

    @pl.when(f == pl.num_programs(1) - 1)
    def _():
        h = h_ref[...] + acc_scr[...]
        ms = jnp.mean(h * h, axis=-1, keepdims=True)
        o_ref[...] = h * lax.rsqrt(ms + EPS) * nw_ref[...]


def _mlp(hn, w_up, w_down, h1, norm_w, seq):
    tm = min(seq, 512)
    tf = 1024
    row = lambda i, f: (i, 0)
    return pl.pallas_call(
        _mlp_kernel,
        grid=(seq // tm, D_FF // tf),
        in_specs=[pl.BlockSpec((tm, D_MODEL), row),
                  pl.BlockSpec((D_MODEL, tf), lambda i, f: (0, f)),
                  pl.BlockSpec((tf, D_MODEL), lambda i, f: (f, 0)),
                  pl.BlockSpec((tm, D_MODEL), row),
                  pl.BlockSpec((1, D_MODEL), lambda i, f: (0, 0))],
        out_specs=pl.BlockSpec((tm, D_MODEL), row),
        out_shape=jax.ShapeDtypeStruct((seq, D_MODEL), F32),
        scratch_shapes=[pltpu.VMEM((tm, D_MODEL), F32)],
        compiler_params=_params(("arbitrary", "arbitrary")),
        name="mlp_final_norm",
    )(hn, w_up, w_down, h1, norm_w.reshape(1, D_MODEL))


def _cmp_to_sel(seq):
    n_cmp = (seq - CMP_BLOCK) // CMP_STRIDE + 1
    n_sel = seq // SEL_BLOCK
    c_start = np.arange(n_cmp) * CMP_STRIDE
    s_start = np.arange(n_sel) * SEL_BLOCK
    overlap = np.clip(np.minimum(c_start[:, None] + CMP_BLOCK, s_start[None, :] + SEL_BLOCK)
                      - np.maximum(c_start[:, None], s_start[None, :]), 0, None)
    m = np.zeros((seq // CMP_STRIDE, LANES), np.float32)
    m[:n_cmp, :n_sel] = overlap / CMP_STRIDE
    return jnp.asarray(m.T, BF16)


def kernel(x, positions, norm_mix_w, w_in, cmp_pos_k, cmp_pos_v, cmp_k_w1, cmp_k_w2, cmp_v_w1, cmp_v_w2, conv_w, conv_b, dt_bias, a_log, ssm_d, ssm_norm_w, w_proj_nsa, w_proj_ssm, w_out, norm_mlp_w, w_up, w_down, norm_final_w):
    bsz, seq, _ = x.shape
    assert bsz == 1 and seq % 1024 == 0 and SEL_TOPK <= seq // SEL_BLOCK <= LANES
    assert w_in.shape[0] == 1, "one layer"
    x2 = x.reshape(seq, D_MODEL)

    o_gate = NSA_HEADS * DH + 6 * NSA_GROUPS * DH
    o_z = o_gate + 3 * NSA_HEADS
    o_dt = o_z + 2 * SSM_D_INNER + 2 * SSM_GROUPS * SSM_STATE
    o_bg = o_dt + SSM_HEADS
    assert o_z == SRC_Z and o_bg == SRC_BG and o_gate == ATT_W
    w_t = jnp.swapaxes(w_in, 1, 2)[0]
    n_gate = 3 * NSA_HEADS
    w_small_t = jnp.concatenate([w_t[o_dt:o_dt + SSM_HEADS], w_t[o_gate:o_gate + n_gate],
                                 jnp.zeros((SMALL_W - SSM_HEADS - n_gate, D_MODEL), F32)], axis=0)

    pos = positions.reshape(seq)
    cosf, sinf = _rope_tables(pos)
    att, kc_chunks, vc_chunks, u = _inproj(x2, norm_mix_w[0], w_t, cosf, sinf)
    ssm = _proj_ssm(u, w_t)
    small = _proj_small(u, w_small_t)

    ncp = seq // CMP_STRIDE
    def pos2(p):
        p = p.reshape(2, CMP_STRIDE * DH)
        return jnp.concatenate([p, jnp.zeros((14, CMP_STRIDE * DH), F32)], axis=0).astype(BF16)
    pos_end = pos[CMP_BLOCK - 1::CMP_STRIDE]
    cos_c, sin_c = _rope_tables(jnp.concatenate([pos_end, jnp.zeros((ncp - pos_end.shape[0],), pos.dtype)]))
    kc, vc = _compress(kc_chunks, vc_chunks, pos2(cmp_pos_k[0]), pos2(cmp_pos_v[0]),
                       cmp_k_w1[0].astype(BF16), cmp_k_w2[0].astype(BF16),
                       cmp_v_w1[0].astype(BF16), cmp_v_w2[0].astype(BF16), cos_c, sin_c)

    o_cmp, sel_bias = _cmp_attn(att, kc, vc, _cmp_to_sel(seq), seq)
    o_nsa, pa, pb, wo, wu, wd = _nsa_attn(att, small, sel_bias, o_cmp,
                                          (w_proj_nsa, w_proj_ssm, w_out, w_up, w_down), seq)

    cw = conv_w[0]
    cb_ = conv_b[0].reshape(1, -1)
    lane_pad = lambda a: jnp.concatenate([a.reshape(1, -1), jnp.zeros((1, LANES - a.shape[-1]), F32)], axis=1)
    head_of = np.arange(SSM_D_INNER) // SSM_HEAD_DIM
    expand = np.arange(LANES)[:, None] == head_of[None, :]
    expand2 = jnp.asarray(np.concatenate([expand, expand], axis=0), BF16)
    tril = np.tril(np.ones((SSD_CHUNK, SSD_CHUNK), np.float32))
    tril3 = jnp.asarray(np.concatenate([tril, tril, tril], axis=1), BF16)
    o_ssm = _mamba(ssm, small, cw[:, :4096], cw[:, 4096:5120], cw[:, 5120:],
                   cb_[:, :4096], cb_[:, 4096:5120], cb_[:, 5120:],
                   lane_pad(dt_bias[0]), lane_pad(a_log[0]),
                   jnp.repeat(ssm_d[0], SSM_HEAD_DIM).reshape(1, -1), ssm_norm_w[0].reshape(1, -1),
                   expand2, tril3, seq)

    merged = _merge(o_nsa, o_ssm, pa, pb, ssm, seq)
    h1, hn = _outproj(merged, wo, x2, norm_mlp_w[0], seq)
    out = _mlp(hn, wu, wd, h1, norm_final_w, seq)
    return out.reshape(bsz, seq, D_MODEL)
```

```python
import math

import jax
import jax.numpy as jnp
import numpy as np
from jax import lax
from jax.experimental import pallas as pl
from jax.experimental.pallas import tpu as pltpu

F32 = jnp.float32
BF16 = jnp.bfloat16

D_MODEL = 2048
NSA_HEADS = 16
NSA_GROUPS = 4
NSA_REP = NSA_HEADS // NSA_GROUPS
DH = 128
CMP_BLOCK = 32
CMP_STRIDE = 16
SEL_BLOCK = 64
SEL_TOPK = 16
WINDOW = 512
ROPE_THETA = 10000.0
N_FORCED = 3
SSM_D_INNER = 4096
SSM_HEAD_DIM = 64
SSM_HEADS = 64
SSM_GROUPS = 8
SSM_STATE = 128
CONV_WIDTH = 4
SSD_CHUNK = 128
D_FF = 4 * D_MODEL
EPS = 1e-6
NEG_INF = -1e30
TINY = 1e-30
MASK_NEG = NEG_INF
TAKEN = -2.0
LOG2E = math.log2(math.e)
LANES = 128
VMEM_LIMIT = 56 * 1024 * 1024

COL_Q = 0
COL_KC = 2048
COL_VC = 2560
COL_KS = 3072
COL_VS = 3584
COL_KW = 4096
COL_VW = 4608
ATT_W = 5120
COL_Z = 0
COL_XS = 4096
COL_B = 8192
COL_C = 9216
COL_GA = 10240
COL_GB = 12288
SSM_W = 14336
SMALL_W = 128


def _dot(a, b):
    return jnp.dot(a, b, preferred_element_type=F32)


def _dot_nt(a, b):
    return lax.dot_general(a, b, (((1,), (1,)), ((), ())), preferred_element_type=F32)


def _params(sem):
    return pltpu.CompilerParams(dimension_semantics=sem, vmem_limit_bytes=VMEM_LIMIT)


def _rope(x, cos, sin_signed):
    return x * cos + pltpu.roll(x, DH // 2, 1) * sin_signed


def _split2(x):
    hi = x.astype(BF16)
    lo = (x - hi.astype(F32)).astype(BF16)
    return hi, lo


def _split3(x):
    hi = x.astype(BF16)
    r = x - hi.astype(F32)
    mid = r.astype(BF16)
    lo = (r - mid.astype(F32)).astype(BF16)
    return hi, mid, lo


def _rope_table_kernel(pos_ref, inv_ref, sign_ref, cos_ref, sin_ref):
    ang = pos_ref[...].astype(F32) * inv_ref[...]
    cos_ref[...] = jnp.cos(ang)
    sin_ref[...] = jnp.sin(ang) * sign_ref[...]


def _rope_tables(positions):
    seq = positions.shape[0]
    half = DH // 2
    inv = ROPE_THETA ** (-jnp.arange(0, DH, 2, dtype=F32) / DH)
    inv_full = jnp.concatenate([inv, inv]).reshape(1, DH)
    sign = jnp.concatenate([-jnp.ones((half,), F32), jnp.ones((half,), F32)]).reshape(1, DH)
    tm = math.gcd(seq, 1024)
    return pl.pallas_call(
        _rope_table_kernel,
        grid=(seq // tm,),
        in_specs=[pl.BlockSpec((tm, 1), lambda i: (i, 0)),
                  pl.BlockSpec((1, DH), lambda i: (0, 0)),
                  pl.BlockSpec((1, DH), lambda i: (0, 0))],
        out_specs=[pl.BlockSpec((tm, DH), lambda i: (i, 0))] * 2,
        out_shape=[jax.ShapeDtypeStruct((seq, DH), F32)] * 2,
        compiler_params=_params(("arbitrary",)),
        name="rope_tables",
    )(positions.reshape(seq, 1), inv_full, sign)


INPROJ_TN = 512
Q_TILE_LO = COL_Q // INPROJ_TN
Q_TILE_HI = COL_KC // INPROJ_TN
KC_TILE = COL_KC // INPROJ_TN
VC_TILE = COL_VC // INPROJ_TN
KS_TILE = COL_KS // INPROJ_TN
KW_TILE = COL_KW // INPROJ_TN


def _inproj_kernel(x_ref, nw_ref, w_ref, cos_ref, sin_ref, o_ref, kcx_ref, vcx_ref, u_ref, acc_scr):
    j = pl.program_id(1)

    @pl.when(j == 0)
    def _():
        xf = x_ref[...]
        ms = jnp.mean(xf * xf, axis=-1, keepdims=True)
        u_ref[...] = (xf * lax.rsqrt(ms + EPS) * nw_ref[...]).astype(BF16)

    acc = _dot_nt(u_ref[...], w_ref[...].astype(BF16))
    is_q = jnp.logical_and(j >= Q_TILE_LO, j < Q_TILE_HI)
    is_rope = is_q | (j == KS_TILE) | (j == KW_TILE)

    @pl.when(jnp.logical_not(is_rope))
    def _():
        o_ref[...] = acc.astype(o_ref.dtype)

    @pl.when(is_rope)
    def _():
        cos = cos_ref[...]
        sin = sin_ref[...]
        scale = jnp.where(is_q, DH ** -0.5 * LOG2E, 1.0).astype(F32)
        for h in range(INPROJ_TN // DH):
            xh = acc[:, h * DH:(h + 1) * DH]
            o_ref[:, h * DH:(h + 1) * DH] = (_rope(xh, cos, sin) * scale).astype(o_ref.dtype)

    def chunked(dst_ref):
        n_chunks = acc_scr.shape[1] // CMP_STRIDE
        for g in range(NSA_GROUPS):
            acc_scr[g] = acc[:, g * DH:(g + 1) * DH]
            for l in range(CMP_STRIDE):
                rows_l = acc_scr[g, pl.ds(l, n_chunks, stride=CMP_STRIDE), :]
                dst_ref[g, :, l * DH:(l + 1) * DH] = rows_l.astype(dst_ref.dtype)

    @pl.when(j == KC_TILE)
    def _():
        chunked(kcx_ref)

    @pl.when(j == VC_TILE)
    def _():
        chunked(vcx_ref)


def _inproj(x2, norm_w, w_t, cosf, sinf):
    seq = x2.shape[0]
    tm = min(seq, 1024)
    tn = INPROJ_TN
    chunk_w = CMP_STRIDE * DH
    chunk_spec = pl.BlockSpec((NSA_GROUPS, tm // CMP_STRIDE, chunk_w), lambda i, j: (0, i, 0))
    chunk_shape = jax.ShapeDtypeStruct((NSA_GROUPS, seq // CMP_STRIDE, chunk_w), BF16)
    return pl.pallas_call(
        _inproj_kernel,
        grid=(seq // tm, ATT_W // tn),
        in_specs=[pl.BlockSpec((tm, D_MODEL), lambda i, j: (i, 0)),
                  pl.BlockSpec((1, D_MODEL), lambda i, j: (0, 0)),
                  pl.BlockSpec((tn, D_MODEL), lambda i, j: (j, 0)),
                  pl.BlockSpec((tm, DH), lambda i, j: (i, 0)),
                  pl.BlockSpec((tm, DH), lambda i, j: (i, 0))],
        out_specs=[pl.BlockSpec((tm, tn), lambda i, j: (i, j)),
                   chunk_spec, chunk_spec,
                   pl.BlockSpec((tm, D_MODEL), lambda i, j: (i, 0))],
        out_shape=[jax.ShapeDtypeStruct((seq, ATT_W), BF16),
                   chunk_shape, chunk_shape,
                   jax.ShapeDtypeStruct((seq, D_MODEL), BF16)],
        scratch_shapes=[pltpu.VMEM((NSA_GROUPS, tm, DH), F32)],
        compiler_params=_params(("arbitrary", "arbitrary")),
        name="inproj_attn",
    )(x2, norm_w.reshape(1, D_MODEL), w_t, cosf, sinf)


SSM_TN = 1024
SRC_Z = NSA_HEADS * DH + 6 * NSA_GROUPS * DH + 3 * NSA_HEADS
SRC_BG = SRC_Z + 2 * SSM_D_INNER + 2 * SSM_GROUPS * SSM_STATE + SSM_HEADS
SSM_SPLIT = COL_GA // SSM_TN


def _ssm_small_kernel(a_ref, w_ref, ws_ref, o_ref, small_ref):
    j = pl.program_id(1)
    last = pl.num_programs(1) - 1

    @pl.when(j < last)
    def _():
        o_ref[...] = _dot_nt(a_ref[...], w_ref[...].astype(BF16)).astype(o_ref.dtype)

    @pl.when(j == last)
    def _():
        small_ref[...] = _dot_nt(a_ref[...], ws_ref[...].astype(BF16))


def _proj_ssm(u, w_t, w_small_t):
    seq, k = u.shape
    tm = min(seq, 2048)
    n_main = SSM_W // SSM_TN
    n_small = w_small_t.shape[0]

    def main_tile(j):
        return j - j // n_main

    def src_rows(i, j):
        jm = main_tile(j)
        jump = (SRC_BG - SRC_Z - SSM_TN * SSM_SPLIT) // 16
        return ((SRC_Z // 16 + (SSM_TN // 16) * jm + jump * (jm // SSM_SPLIT)) * 16, 0)

    return pl.pallas_call(
        _ssm_small_kernel,
        grid=(seq // tm, n_main + 1),
        in_specs=[pl.BlockSpec((tm, k), lambda i, j: (i, 0)),
                  pl.BlockSpec((pl.Element(SSM_TN), pl.Element(k)), src_rows),
                  pl.BlockSpec((n_small, k), lambda i, j: (0, 0))],
        out_specs=[pl.BlockSpec((tm, SSM_TN), lambda i, j: (i, main_tile(j))),
                   pl.BlockSpec((tm, n_small), lambda i, j: (i, 0))],
        out_shape=[jax.ShapeDtypeStruct((seq, SSM_W), BF16),
                   jax.ShapeDtypeStruct((seq, n_small), F32)],
        compiler_params=_params(("arbitrary", "arbitrary")),
        name="inproj_ssm",
    )(u, w_t, w_small_t)


def _compress_kernel(xk_ref, xv_ref, pk_ref, pv_ref, w1k_ref, w2k_ref, w1v_ref, w2v_ref,
                     cos_ref, sin_ref, kc_ref, vc_ref):
    half = CMP_STRIDE * DH
    ncp = xk_ref.shape[1]

    def comp(x_ref, p_ref, w1_ref, w2_ref):
        x = x_ref[0]
        wt = w1_ref[0:half, :]
        wb = w1_ref[half:2 * half, :]
        a = _dot(x, wt)
        b = _dot(x, wb)
        pb = _dot(p_ref[...], wt)[0:1] + _dot(p_ref[...], wb)[1:2]
        h = a + pltpu.roll(b, ncp - 1, 0) + pb
        h = h * jax.nn.sigmoid(h)
        return _dot(h.astype(BF16), w2_ref[...])

    kc = comp(xk_ref, pk_ref, w1k_ref, w2k_ref)
    kc_ref[0] = _rope(kc, cos_ref[...], sin_ref[...]).astype(BF16)
    vc_ref[0] = comp(xv_ref, pv_ref, w1v_ref, w2v_ref).astype(BF16)


def _compress(xk, xv, pk, pv, w1k, w2k, w1v, w2v, cos_c, sin_c):
    g, ncp, width = xk.shape
    full = lambda shape: pl.BlockSpec(shape, lambda i: (0,) * len(shape))
    per_g = pl.BlockSpec((1, ncp, width), lambda i: (i, 0, 0))
    out_g = pl.BlockSpec((1, ncp, DH), lambda i: (i, 0, 0))
    return pl.pallas_call(
        _compress_kernel,
        grid=(g,),
        in_specs=[per_g, per_g, full(pk.shape), full(pv.shape), full(w1k.shape), full(w2k.shape),
                  full(w1v.shape), full(w2v.shape), full(cos_c.shape), full(sin_c.shape)],
        out_specs=[out_g, out_g],
        out_shape=[jax.ShapeDtypeStruct((g, ncp, DH), BF16)] * 2,
        compiler_params=_params(("arbitrary",)),
        name="compress",
    )(xk, xv, pk, pv, w1k, w2k, w1v, w2v, cos_c, sin_c)


CMP_TQ = 256


def _cmp_attn_kernel(q_ref, kc_ref, vc_ref, mt_ref, oc_ref, mask_ref):
    i = pl.program_id(0)
    qbase = i * CMP_TQ
    ncp = kc_ref.shape[1]
    nsel = mt_ref.shape[0]
    row = lax.broadcasted_iota(jnp.int32, (CMP_TQ, ncp), 0)
    col = lax.broadcasted_iota(jnp.int32, (CMP_TQ, ncp), 1)
    bias1 = jnp.where((col * CMP_STRIDE + (CMP_BLOCK - 1)) <= qbase + row, 0.0, MASK_NEG)
    bias = jnp.concatenate([bias1] * NSA_REP, axis=0)
    has_key = jnp.concatenate([bias1[:, 0:1] == 0.0] * NSA_REP, axis=0)
    blk = lax.broadcasted_iota(jnp.int32, (nsel, CMP_TQ), 0)
    cur = (qbase + lax.broadcasted_iota(jnp.int32, (nsel, CMP_TQ), 1)) >> 6
    future = blk > cur
    forced = (blk == 0) | (blk == cur) | (blk == cur - 1)
    blk_f = blk.astype(F32)
    mt = mt_ref[...]
    importance = []
    for g in range(NSA_GROUPS):
        qg = jnp.concatenate(
            [q_ref[:, (g * NSA_REP + r) * DH:(g * NSA_REP + r + 1) * DH] for r in range(NSA_REP)], axis=0)
        s = _dot_nt(qg, kc_ref[g]) + bias
        p = jnp.exp2(s - jnp.max(s, axis=-1, keepdims=True)).astype(BF16)
        ov = _dot(p, jnp.concatenate([vc_ref[g], jnp.ones((ncp, LANES), BF16)], axis=1))
        inv = jnp.where(has_key, 1.0 / ov[:, DH:], 0.0)
        o = ov[:, :DH] * inv
        imp = None
        for r in range(NSA_REP):
            h = g * NSA_REP + r
            rows_r = slice(r * CMP_TQ, (r + 1) * CMP_TQ)
            oc_ref[:, h * DH:(h + 1) * DH] = o[rows_r].astype(oc_ref.dtype)
            part = _dot_nt(p[rows_r], mt) * inv[rows_r]
            imp = part if imp is None else imp + part
        importance.append(imp.T)
    v = jnp.where(future | forced, -1.0, jnp.stack(importance))
    for _ in range(SEL_TOPK - N_FORCED):
        mx = jnp.max(v, axis=1, keepdims=True)
        first = jnp.min(jnp.where(v == mx, blk_f, float(nsel)), axis=1, keepdims=True)
        v = jnp.where(blk_f == first, TAKEN, v)
    bias_t = jnp.where(forced | (v == TAKEN), 0.0, MASK_NEG)
    for g in range(NSA_GROUPS):
        mask_ref[g] = bias_t[g].T.astype(mask_ref.dtype)


def _cmp_attn(main, kc, vc, mt, seq):
    nsel = mt.shape[0]
    ncp = kc.shape[1]
    return pl.pallas_call(
        _cmp_attn_kernel,
        grid=(seq // CMP_TQ,),
        in_specs=[pl.BlockSpec((CMP_TQ, NSA_HEADS * DH), lambda i: (i, COL_Q // (NSA_HEADS * DH))),
                  pl.BlockSpec((NSA_GROUPS, ncp, DH), lambda i: (0, 0, 0)),
                  pl.BlockSpec((NSA_GROUPS, ncp, DH), lambda i: (0, 0, 0)),
                  pl.BlockSpec((nsel, ncp), lambda i: (0, 0))],
        out_specs=[pl.BlockSpec((CMP_TQ, NSA_HEADS * DH), lambda i: (i, 0)),
                   pl.BlockSpec((NSA_GROUPS, CMP_TQ, nsel), lambda i: (0, i, 0))],
        out_shape=[jax.ShapeDtypeStruct((seq, NSA_HEADS * DH), BF16),
                   jax.ShapeDtypeStruct((NSA_GROUPS, seq, nsel), BF16)],
        compiler_params=_params(("arbitrary",)),
        name="cmp_attn_topk",
    )(main, kc, vc, mt)


ATT_TQ = 512
ATT_TK = 512
WIN_SPAN = WINDOW + ATT_TQ // 2


def _nsa_attn_kernel(q_ref, mask_ref, ks_ref, vs_ref, kw_ref, vw_ref, oc_ref, gate_ref, *rest):
    n_w = (len(rest) - 4) // 2
    w_refs, o_ref, wb_refs = rest[:n_w], rest[n_w], rest[n_w + 1:2 * n_w + 1]
    qa_scr, acc_scr, m_scr = rest[2 * n_w + 1:]
    for src, dst in zip(w_refs, wb_refs):
        dst[...] = src[...].astype(dst.dtype)
    i = pl.program_id(1)
    qbase = i * ATT_TQ
    assert mask_ref.shape[2] == LANES, "the block-selection bias rides in one extra 128-wide contraction slab"
    bias = mask_ref[0]
    for r in range(NSA_REP):
        qa_scr[r] = jnp.concatenate([q_ref[:, r * DH:(r + 1) * DH], bias], axis=1)
    m_scr[...] = jnp.full(m_scr.shape, NEG_INF, F32)
    acc_scr[...] = jnp.zeros(acc_scr.shape, F32)
    krow = lax.broadcasted_iota(jnp.int32, (ATT_TK, LANES), 0)
    kblk = lax.broadcasted_iota(jnp.int32, (ATT_TK, LANES), 1)

    half = ATT_TQ // 2

    def key_tile(j):
        k0 = pl.multiple_of(j * ATT_TK, ATT_TK)
        onehot = jnp.where(((krow + k0) >> 6) == kblk, 1.0, 0.0).astype(BF16)
        k_aug = jnp.concatenate([ks_ref[pl.ds(k0, ATT_TK), :], onehot], axis=1)
        v_aug = jnp.concatenate([vs_ref[pl.ds(k0, ATT_TK), :], jnp.ones((ATT_TK, LANES), BF16)], axis=1)
        return k_aug, v_aug

    def online_update(r, rows, s, v_aug):
        m_prev = m_scr[r, rows, :]
        m_new = jnp.maximum(m_prev, jnp.max(s, axis=-1, keepdims=True))
        alpha = jnp.exp2(m_prev - m_new)
        p = jnp.exp2(s - jnp.concatenate([m_new] * (s.shape[1] // LANES), axis=1))
        acc_scr[r, rows, :] = (acc_scr[r, rows, :] * jnp.concatenate([alpha, alpha], axis=1)
                               + _dot(p.astype(BF16), v_aug))
        m_scr[r, rows, :] = m_new

    def step(j, diagonal):
        k_aug, v_aug = key_tile(j)
        if not diagonal:
            for r in range(NSA_REP):
                online_update(r, slice(0, ATT_TQ), _dot_nt(qa_scr[r], k_aug), v_aug)
            return
        r_ = lax.broadcasted_iota(jnp.int32, (half, half), 0)
        c_ = lax.broadcasted_iota(jnp.int32, (half, half), 1)
        tri_bias = jnp.where(c_ <= r_, 0.0, MASK_NEG)
        for r in range(NSA_REP):
            s0 = _dot_nt(qa_scr[r, 0:half, :], k_aug[:half]) + tri_bias
            online_update(r, slice(0, half), s0, v_aug[:half])
            s1 = _dot_nt(qa_scr[r, half:ATT_TQ, :], k_aug)
            s1 = jnp.concatenate([s1[:, :half], s1[:, half:] + tri_bias], axis=1)
            online_update(r, slice(half, ATT_TQ), s1, v_aug)

    def body(jj, carry):
        for u in range(4):
            step(4 * jj + u, False)
        return carry

    lax.fori_loop(0, i // 4, body, 0)
    done = (i // 4) * 4

    @pl.when((i & 2) != 0)
    def _():
        step(done, False)
        step(done + 1, False)

    @pl.when((i & 1) != 0)
    def _():
        step(i - 1, False)

    step(i, True)

    first_gate = SSM_HEADS + 3 * NSA_REP * pl.program_id(0)
    gates = jax.nn.sigmoid(pltpu.roll(gate_ref[...], SMALL_W - first_gate, 1))
    r_ = lax.broadcasted_iota(jnp.int32, (half, WIN_SPAN), 0)
    c_ = lax.broadcasted_iota(jnp.int32, (half, WIN_SPAN), 1)
    for hh in range(2):
        rows = slice(hh * half, (hh + 1) * half)
        q0 = qbase + hh * half
        start = pl.multiple_of(jnp.maximum(q0 - WINDOW, 0), half)
        kw = kw_ref[pl.ds(start, WIN_SPAN), :]
        vw_aug = jnp.concatenate([vw_ref[pl.ds(start, WIN_SPAN), :], jnp.ones((WIN_SPAN, LANES), BF16)], axis=1)
        diff = (q0 + r_) - (start + c_)
        win_bias = jnp.where((diff >= 0) & (diff < WINDOW), 0.0, MASK_NEG)
        for r in range(NSA_REP):
            s = _dot_nt(q_ref[rows, r * DH:(r + 1) * DH], kw) + win_bias
            p = jnp.exp2(s - jnp.max(s, axis=-1, keepdims=True))
            ow = _dot(p.astype(BF16), vw_aug)
            o_win = ow[:, :DH] / jnp.maximum(ow[:, DH:], TINY)
            acc = acc_scr[r, rows, :]
            o_sel = acc[:, :DH] / jnp.maximum(acc[:, DH:], TINY)
            g_r = gates[rows]
            o = (g_r[:, 3 * r:3 * r + 1] * oc_ref[rows, r * DH:(r + 1) * DH].astype(F32)
                 + g_r[:, 3 * r + 1:3 * r + 2] * o_sel
                 + g_r[:, 3 * r + 2:3 * r + 3] * o_win)
            o_ref[rows, r * DH:(r + 1) * DH] = o.astype(o_ref.dtype)


def _nsa_attn(main, small, mask, o_cmp, weights, seq):
    gw = NSA_REP * DH
    nsel = mask.shape[2]
    n_i = seq // ATT_TQ
    steps = NSA_GROUPS * n_i
    kv_spec = lambda col: pl.BlockSpec((seq, DH), lambda g, i: (0, col // DH + g))
    w_in_specs, w_out_specs, w_out_shapes = [], [], []
    for w in weights:
        _, rows, cols = w.shape
        slab = rows // steps
        assert slab * steps == rows and slab % 16 == 0
        w_in_specs.append(pl.BlockSpec((None, slab, cols), lambda g, i: (0, g * n_i + i, 0)))
        w_out_specs.append(pl.BlockSpec((slab, cols), lambda g, i: (g * n_i + i, 0)))
        w_out_shapes.append(jax.ShapeDtypeStruct((rows, cols), BF16))
    return pl.pallas_call(
        _nsa_attn_kernel,
        grid=(NSA_GROUPS, n_i),
        in_specs=[pl.BlockSpec((ATT_TQ, gw), lambda g, i: (i, COL_Q // gw + g)),
                  pl.BlockSpec((1, ATT_TQ, nsel), lambda g, i: (g, i, 0)),
                  kv_spec(COL_KS), kv_spec(COL_VS), kv_spec(COL_KW), kv_spec(COL_VW),
                  pl.BlockSpec((ATT_TQ, gw), lambda g, i: (i, g)),
                  pl.BlockSpec((ATT_TQ, SMALL_W), lambda g, i: (i, 0))] + w_in_specs,
        out_specs=[pl.BlockSpec((ATT_TQ, gw), lambda g, i: (i, g))] + w_out_specs,
        out_shape=[jax.ShapeDtypeStruct((seq, NSA_HEADS * DH), BF16)] + w_out_shapes,
        scratch_shapes=[pltpu.VMEM((NSA_REP, ATT_TQ, 2 * DH), BF16),
                        pltpu.VMEM((NSA_REP, ATT_TQ, 2 * DH), F32),
                        pltpu.VMEM((NSA_REP, ATT_TQ, LANES), F32)],
        compiler_params=_params(("arbitrary", "arbitrary")),
        name="nsa_sel_win_attn",
    )(main, mask, main, main, main, main, o_cmp, small, *weights)


HALO = 8
GROUP_W = SSM_D_INNER // SSM_GROUPS
HEADS_PER_GROUP = SSM_HEADS // SSM_GROUPS
MAMBA_ROWS = 2 * SSD_CHUNK


def _silu(x):
    h = 0.5 * x
    return h + h * jnp.tanh(h)


def _mamba_kernel(z_ref, xs_ref, b_ref, c_ref, dt_ref,
                  cwx_ref, cwb_ref, cwc_ref, cbx_ref, cbb_ref, cbc_ref,
                  dtb_ref, alog_ref, dfull_ref, nw_ref, expand2_ref, tril3_ref,
                  o_ref, state_scr, xs_ext, b_ext, c_ext):
    L = SSD_CHUNK

    @pl.when(pl.program_id(0) == 0)
    def _():
        state_scr[...] = jnp.zeros(state_scr.shape, F32)
        for carry in (xs_ext, b_ext, c_ext):
            carry[...] = jnp.zeros(carry.shape, F32)

    for sub in range(z_ref.shape[0] // L):
        _mamba_chunk(slice(sub * L, (sub + 1) * L), z_ref, xs_ref, b_ref, c_ref, dt_ref,
                     cwx_ref, cwb_ref, cwc_ref, cbx_ref, cbb_ref, cbc_ref,
                     dtb_ref, alog_ref, dfull_ref, nw_ref, expand2_ref, tril3_ref,
                     o_ref, state_scr, xs_ext, b_ext, c_ext)


def _mamba_chunk(rows, z_ref, xs_ref, b_ref, c_ref, dt_ref,
                 cwx_ref, cwb_ref, cwc_ref, cbx_ref, cbb_ref, cbc_ref,
                 dtb_ref, alog_ref, dfull_ref, nw_ref, expand2_ref, tril3_ref,
                 o_ref, state_scr, xs_ext, b_ext, c_ext):
    L = SSD_CHUNK

    def conv_silu(x_ref, carry, w_ref, bias_ref):
        x = x_ref[rows, :].astype(F32)
        first_row = lax.broadcasted_iota(jnp.int32, (HALO, x.shape[1]), 0) == 0
        r = None
        for k in range(CONV_WIDTH - 1):
            t = w_ref[k:k + 1, :] * x
            if r is not None:
                t = t + r
            rolled = pltpu.roll(t, 1, 0)
            top = jnp.where(first_row, carry[k:k + 1, :], rolled[0:HALO])
            carry[k:k + 1, :] = t[L - 1:L, :]
            r = jnp.concatenate([top, rolled[HALO:]], axis=0)
        return _silu(r + w_ref[CONV_WIDTH - 1:CONV_WIDTH, :] * x + bias_ref[...])

    xs = conv_silu(xs_ref, xs_ext, cwx_ref, cbx_ref)
    bm = conv_silu(b_ref, b_ext, cwb_ref, cbb_ref).astype(BF16)
    cm = conv_silu(c_ref, c_ext, cwc_ref, cbc_ref).astype(BF16)

    xdt_raw = dt_ref[rows, :] + dtb_ref[...]
    dt = jnp.maximum(xdt_raw, 0.0) + jnp.log1p(jnp.exp(-jnp.abs(xdt_raw)))
    adt = dt * (-jnp.exp(alog_ref[...]))
    acs = _dot(tril3_ref[...], jnp.concatenate(_split3(adt), axis=0)) * LOG2E
    acs_t = acs.T
    a_last = acs[L - 1:L, :]
    stacked = jnp.concatenate([dt, jnp.exp2(acs), jnp.exp2(a_last - acs)], axis=0)
    wide = _dot(jnp.concatenate(_split2(stacked), axis=1), expand2_ref[...])
    dt_w = wide[0:L]
    in_decay_w = wide[L:2 * L]
    out_decay_w = wide[2 * L:3 * L]

    xdt = xs * dt_w
    xst = xdt * out_decay_w
    causal = lax.broadcasted_iota(jnp.int32, (L, L), 0) >= lax.broadcasted_iota(jnp.int32, (L, L), 1)
    causal_bias = jnp.where(causal, 0.0, MASK_NEG)
    lane = lax.broadcasted_iota(jnp.int32, (L, LANES), 1)
    low_half = lane < SSM_HEAD_DIM
    chunk_decay = jnp.exp2(acs_t[:, L - 1:L])

    for g in range(SSM_GROUPS):
        gs = slice(g * GROUP_W, (g + 1) * GROUP_W)
        bg = bm[:, g * SSM_STATE:(g + 1) * SSM_STATE]
        cg = cm[:, g * SSM_STATE:(g + 1) * SSM_STATE]
        cb = _dot_nt(cg, bg)
        prev = state_scr[gs, :]
        y_off = _dot_nt(cg, prev.astype(BF16)) * in_decay_w[:, gs]
        pieces = []
        for hp in range(HEADS_PER_GROUP // 2):
            lhs = []
            for e in range(2):
                h = g * HEADS_PER_GROUP + 2 * hp + e
                decay = jnp.exp2(acs[:, h:h + 1] - acs_t[h:h + 1, :] + causal_bias)
                lhs.append((cb * decay).astype(BF16))
            slab = xdt[:, g * GROUP_W + hp * LANES:g * GROUP_W + (hp + 1) * LANES]
            rhs = jnp.concatenate([jnp.where(low_half, slab, 0.0), jnp.where(low_half, 0.0, slab)], axis=0)
            pieces.append(_dot(jnp.concatenate(lhs, axis=1), rhs.astype(BF16)))
        y_g = jnp.concatenate(pieces, axis=1) + y_off
        y_g = y_g + dfull_ref[:, gs] * xs[:, gs]
        y_g = y_g * _silu(z_ref[rows, gs].astype(F32))
        ms = jnp.mean(y_g * y_g, axis=-1, keepdims=True)
        o_ref[rows, gs] = (y_g * lax.rsqrt(ms + EPS) * nw_ref[:, gs]).astype(o_ref.dtype)
        new = _dot(xst[:, gs].T.astype(BF16), bg)
        cd = jnp.broadcast_to(chunk_decay[g * HEADS_PER_GROUP:(g + 1) * HEADS_PER_GROUP], (HEADS_PER_GROUP, LANES))
        cd = jnp.broadcast_to(cd[:, None, :], (HEADS_PER_GROUP, SSM_HEAD_DIM, LANES)).reshape(GROUP_W, LANES)
        state_scr[gs, :] = prev * cd + new


def _mamba(main, small, cwx, cwb, cwc, cbx, cbb, cbc, dtb, alog, dfull, nw, expand2, tril3, seq):
    L = MAMBA_ROWS
    bw = SSM_GROUPS * SSM_STATE
    full = lambda a: pl.BlockSpec(a.shape, lambda c: (0,) * a.ndim)
    consts = [cwx, cwb, cwc, cbx, cbb, cbc, dtb, alog, dfull, nw, expand2, tril3]
    return pl.pallas_call(
        _mamba_kernel,
        grid=(seq // L,),
        in_specs=[pl.BlockSpec((L, SSM_D_INNER), lambda c: (c, COL_Z // SSM_D_INNER)),
                  pl.BlockSpec((L, SSM_D_INNER), lambda c: (c, COL_XS // SSM_D_INNER)),
                  pl.BlockSpec((L, bw), lambda c: (c, COL_B // bw)),
                  pl.BlockSpec((L, bw), lambda c: (c, COL_C // bw)),
                  pl.BlockSpec((L, LANES), lambda c: (c, 0))] + [full(a) for a in consts],
        out_specs=pl.BlockSpec((L, SSM_D_INNER), lambda c: (c, 0)),
        out_shape=jax.ShapeDtypeStruct((seq, SSM_D_INNER), BF16),
        scratch_shapes=[pltpu.VMEM((SSM_D_INNER, SSM_STATE), F32),
                        pltpu.VMEM((HALO, SSM_D_INNER), F32),
                        pltpu.VMEM((HALO, bw), F32),
                        pltpu.VMEM((HALO, bw), F32)],
        compiler_params=_params(("arbitrary",)),
        name="mamba2_ssd",
    )(main, main, main, main, small, *consts)


def _merge_kernel(a_ref, b_ref, pa_ref, pb_ref, ga_ref, gb_ref, o_ref):
    ya = _dot(a_ref[...], pa_ref[...])
    yb = _dot(b_ref[...], pb_ref[...])
    o = jax.nn.sigmoid(ga_ref[...].astype(F32)) * ya + jax.nn.sigmoid(gb_ref[...].astype(F32)) * yb
    o_ref[...] = o.astype(o_ref.dtype)


def _merge(o_nsa, o_ssm, pa, pb, main, seq):
    tm = min(seq, 512)
    tn = 1024
    return pl.pallas_call(
        _merge_kernel,
        grid=(seq // tm, D_MODEL // tn),
        in_specs=[pl.BlockSpec((tm, NSA_HEADS * DH), lambda i, j: (i, 0)),
                  pl.BlockSpec((tm, SSM_D_INNER), lambda i, j: (i, 0)),
                  pl.BlockSpec((NSA_HEADS * DH, tn), lambda i, j: (0, j)),
                  pl.BlockSpec((SSM_D_INNER, tn), lambda i, j: (0, j)),
                  pl.BlockSpec((tm, tn), lambda i, j: (i, COL_GA // tn + j)),
                  pl.BlockSpec((tm, tn), lambda i, j: (i, COL_GB // tn + j))],
        out_specs=pl.BlockSpec((tm, tn), lambda i, j: (i, j)),
        out_shape=jax.ShapeDtypeStruct((seq, D_MODEL), BF16),
        compiler_params=_params(("arbitrary", "arbitrary")),
        name="gated_merge",
    )(o_nsa, o_ssm, pa, pb, main, main)


def _outproj_kernel(m_ref, w_ref, x_ref, nw_ref, h_ref, hn_ref):
    h = x_ref[...] + _dot(m_ref[...], w_ref[...])
    h_ref[...] = h
    ms = jnp.mean(h * h, axis=-1, keepdims=True)
    hn_ref[...] = (h * lax.rsqrt(ms + EPS) * nw_ref[...]).astype(hn_ref.dtype)


def _outproj(merged, w_out, x2, norm_w, seq):
    tm = min(seq, 512)
    row = pl.BlockSpec((tm, D_MODEL), lambda i: (i, 0))
    return pl.pallas_call(
        _outproj_kernel,
        grid=(seq // tm,),
        in_specs=[row, pl.BlockSpec((D_MODEL, D_MODEL), lambda i: (0, 0), pipeline_mode=pl.Buffered(1)), row,
                  pl.BlockSpec((1, D_MODEL), lambda i: (0, 0))],
        out_specs=[row, row],
        out_shape=[jax.ShapeDtypeStruct((seq, D_MODEL), F32), jax.ShapeDtypeStruct((seq, D_MODEL), BF16)],
        compiler_params=_params(("arbitrary",)),
        name="outproj_residual_norm",
    )(merged, w_out, x2, norm_w.reshape(1, D_MODEL))


def _mlp_kernel(hn_ref, wu_ref, wd_ref, h_ref, nw_ref, o_ref, acc_scr):
    f = pl.program_id(1)

    @pl.when(f == 0)
    def _():
        acc_scr[...] = jnp.zeros(acc_scr.shape, F32)

    up = jnp.maximum(_dot(hn_ref[...], wu_ref[...]), 0.0)
    acc_scr[...] += _dot((up * up).astype(BF16), wd_ref[...])

    @pl.when(f == pl.num_programs(1) - 1)
    def _():
        h = h_ref[...] + acc_scr[...]
        ms = jnp.mean(h * h, axis=-1, keepdims=True)
        o_ref[...] = h * lax.rsqrt(ms + EPS) * nw_ref[...]


def _mlp(hn, w_up, w_down, h1, norm_w, seq):
    tm = min(seq, 512)
    tf = 1024
    row = lambda i, f: (i, 0)
    return pl.pallas_call(
        _mlp_kernel,
        grid=(seq // tm, D_FF // tf),
        in_specs=[pl.BlockSpec((tm, D_MODEL), row),
                  pl.BlockSpec((D_MODEL, tf), lambda i, f: (0, f)),
                  pl.BlockSpec((tf, D_MODEL), lambda i, f: (f, 0)),
                  pl.BlockSpec((tm, D_MODEL), row),
                  pl.BlockSpec((1, D_MODEL), lambda i, f: (0, 0))],
        out_specs=pl.BlockSpec((tm, D_MODEL), row),
        out_shape=jax.ShapeDtypeStruct((seq, D_MODEL), F32),
        scratch_shapes=[pltpu.VMEM((tm, D_MODEL), F32)],
        compiler_params=_params(("arbitrary", "arbitrary")),
        name="mlp_final_norm",
    )(hn, w_up, w_down, h1, norm_w.reshape(1, D_MODEL))


def _cmp_to_sel(seq):
    n_cmp = (seq - CMP_BLOCK) // CMP_STRIDE + 1
    n_sel = seq // SEL_BLOCK
    c_start = np.arange(n_cmp) * CMP_STRIDE
    s_start = np.arange(n_sel) * SEL_BLOCK
    overlap = np.clip(np.minimum(c_start[:, None] + CMP_BLOCK, s_start[None, :] + SEL_BLOCK)
                      - np.maximum(c_start[:, None], s_start[None, :]), 0, None)
    m = np.zeros((seq // CMP_STRIDE, LANES), np.float32)
    m[:n_cmp, :n_sel] = overlap / CMP_STRIDE
    return jnp.asarray(m.T, BF16)


def kernel(x, positions, norm_mix_w, w_in, cmp_pos_k, cmp_pos_v, cmp_k_w1, cmp_k_w2, cmp_v_w1, cmp_v_w2, conv_w, conv_b, dt_bias, a_log, ssm_d, ssm_norm_w, w_proj_nsa, w_proj_ssm, w_out, norm_mlp_w, w_up, w_down, norm_final_w):
    bsz, seq, _ = x.shape
    assert bsz == 1 and seq % 1024 == 0 and SEL_TOPK <= seq // SEL_BLOCK <= LANES
    assert w_in.shape[0] == 1, "one layer"
    x2 = x.reshape(seq, D_MODEL)

    o_gate = NSA_HEADS * DH + 6 * NSA_GROUPS * DH
    o_z = o_gate + 3 * NSA_HEADS
    o_dt = o_z + 2 * SSM_D_INNER + 2 * SSM_GROUPS * SSM_STATE
    o_bg = o_dt + SSM_HEADS
    assert o_z == SRC_Z and o_bg == SRC_BG and o_gate == ATT_W
    w_t = jnp.swapaxes(w_in, 1, 2)[0]
    n_gate = 3 * NSA_HEADS
    w_small_t = jnp.concatenate([w_t[o_dt:o_dt + SSM_HEADS], w_t[o_gate:o_gate + n_gate],
                                 jnp.zeros((SMALL_W - SSM_HEADS - n_gate, D_MODEL), F32)], axis=0)

    pos = positions.reshape(seq)
    cosf, sinf = _rope_tables(pos)
    att, kc_chunks, vc_chunks, u = _inproj(x2, norm_mix_w[0], w_t, cosf, sinf)
    ssm, small = _proj_ssm(u, w_t, w_small_t)

    ncp = seq // CMP_STRIDE
    def pos2(p):
        p = p.reshape(2, CMP_STRIDE * DH)
        return jnp.concatenate([p, jnp.zeros((14, CMP_STRIDE * DH), F32)], axis=0).astype(BF16)
    pos_end = pos[CMP_BLOCK - 1::CMP_STRIDE]
    cos_c, sin_c = _rope_tables(jnp.concatenate([pos_end, jnp.zeros((ncp - pos_end.shape[0],), pos.dtype)]))
    kc, vc = _compress(kc_chunks, vc_chunks, pos2(cmp_pos_k[0]), pos2(cmp_pos_v[0]),
                       cmp_k_w1[0].astype(BF16), cmp_k_w2[0].astype(BF16),
                       cmp_v_w1[0].astype(BF16), cmp_v_w2[0].astype(BF16), cos_c, sin_c)

    o_cmp, sel_bias = _cmp_attn(att, kc, vc, _cmp_to_sel(seq), seq)
    o_nsa, pa, pb, wo, wu, wd = _nsa_attn(att, small, sel_bias, o_cmp,
                                          (w_proj_nsa, w_proj_ssm, w_out, w_up, w_down), seq)

    cw = conv_w[0]
    cb_ = conv_b[0].reshape(1, -1)
    lane_pad = lambda a: jnp.concatenate([a.reshape(1, -1), jnp.zeros((1, LANES - a.shape[-1]), F32)], axis=1)
    head_of = np.arange(SSM_D_INNER) // SSM_HEAD_DIM
    expand = np.arange(LANES)[:, None] == head_of[None, :]
    expand2 = jnp.asarray(np.concatenate([expand, expand], axis=0), BF16)
    tril = np.tril(np.ones((SSD_CHUNK, SSD_CHUNK), np.float32))
    tril3 = jnp.asarray(np.concatenate([tril, tril, tril], axis=1), BF16)
    o_ssm = _mamba(ssm, small, cw[:, :4096], cw[:, 4096:5120], cw[:, 5120:],
                   cb_[:, :4096], cb_[:, 4096:5120], cb_[:, 5120:],
                   lane_pad(dt_bias[0]), lane_pad(a_log[0]),
                   jnp.repeat(ssm_d[0], SSM_HEAD_DIM).reshape(1, -1), ssm_norm_w[0].reshape(1, -1),
                   expand2, tril3, seq)

    merged = _merge(o_nsa, o_ssm, pa, pb, ssm, seq)
    h1, hn = _outproj(merged, wo, x2, norm_mlp_w[0], seq)
    out = _mlp(hn, wu, wd, h1, norm_final_w, seq)
    return out.reshape(bsz, seq, D_MODEL)
```

```python
import math

import jax
import jax.numpy as jnp
import numpy as np
from jax import lax
from jax.experimental import pallas as pl
from jax.experimental.pallas import tpu as pltpu

F32 = jnp.float32
BF16 = jnp.bfloat16

D_MODEL = 2048
NSA_HEADS = 16
NSA_GROUPS = 4
NSA_REP = NSA_HEADS // NSA_GROUPS
DH = 128
CMP_BLOCK = 32
CMP_STRIDE = 16
SEL_BLOCK = 64
SEL_TOPK = 16
WINDOW = 512
ROPE_THETA = 10000.0
N_FORCED = 3
SSM_D_INNER = 4096
SSM_HEAD_DIM = 64
SSM_HEADS = 64
SSM_GROUPS = 8
SSM_STATE = 128
CONV_WIDTH = 4
SSD_CHUNK = 128
D_FF = 4 * D_MODEL
EPS = 1e-6
NEG_INF = -1e30
TINY = 1e-30
MASK_NEG = NEG_INF
TAKEN = -2.0
LOG2E = math.log2(math.e)
LANES = 128
VMEM_LIMIT = 56 * 1024 * 1024

COL_Q = 0
COL_KC = 2048
COL_VC = 2560
COL_KS = 3072
COL_VS = 3584
COL_KW = 4096
COL_VW = 4608
ATT_W = 5120
COL_Z = 0
COL_XS = 4096
COL_B = 8192
COL_C = 9216
COL_GA = 10240
COL_GB = 12288
SSM_W = 14336
SMALL_W = 128


def _dot(a, b):
    return jnp.dot(a, b, preferred_element_type=F32)


def _dot_nt(a, b):
    return lax.dot_general(a, b, (((1,), (1,)), ((), ())), preferred_element_type=F32)


def _params(sem):
    return pltpu.CompilerParams(dimension_semantics=sem, vmem_limit_bytes=VMEM_LIMIT)


def _rope(x, cos, sin_signed):
    return x * cos + pltpu.roll(x, DH // 2, 1) * sin_signed


def _split2(x):
    hi = x.astype(BF16)
    lo = (x - hi.astype(F32)).astype(BF16)
    return hi, lo


def _split3(x):
    hi = x.astype(BF16)
    r = x - hi.astype(F32)
    mid = r.astype(BF16)
    lo = (r - mid.astype(F32)).astype(BF16)
    return hi, mid, lo


def _rope_table_kernel(pos_ref, inv_ref, sign_ref, cos_ref, sin_ref):
    ang = pos_ref[...].astype(F32) * inv_ref[...]
    cos_ref[...] = jnp.cos(ang)
    sin_ref[...] = jnp.sin(ang) * sign_ref[...]


def _rope_tables(positions):
    seq = positions.shape[0]
    half = DH // 2
    inv = ROPE_THETA ** (-jnp.arange(0, DH, 2, dtype=F32) / DH)
    inv_full = jnp.concatenate([inv, inv]).reshape(1, DH)
    sign = jnp.concatenate([-jnp.ones((half,), F32), jnp.ones((half,), F32)]).reshape(1, DH)
    tm = math.gcd(seq, 1024)
    return pl.pallas_call(
        _rope_table_kernel,
        grid=(seq // tm,),
        in_specs=[pl.BlockSpec((tm, 1), lambda i: (i, 0)),
                  pl.BlockSpec((1, DH), lambda i: (0, 0)),
                  pl.BlockSpec((1, DH), lambda i: (0, 0))],
        out_specs=[pl.BlockSpec((tm, DH), lambda i: (i, 0))] * 2,
        out_shape=[jax.ShapeDtypeStruct((seq, DH), F32)] * 2,
        compiler_params=_params(("arbitrary",)),
        name="rope_tables",
    )(positions.reshape(seq, 1), inv_full, sign)


INPROJ_TN = 512
Q_TILE_LO = COL_Q // INPROJ_TN
Q_TILE_HI = COL_KC // INPROJ_TN
KC_TILE = COL_KC // INPROJ_TN
VC_TILE = COL_VC // INPROJ_TN
KS_TILE = COL_KS // INPROJ_TN
KW_TILE = COL_KW // INPROJ_TN


def _inproj_kernel(x_ref, nw_ref, w_ref, cos_ref, sin_ref, o_ref, kcx_ref, vcx_ref, u_ref, acc_scr):
    j = pl.program_id(1)

    @pl.when(j == 0)
    def _():
        xf = x_ref[...]
        ms = jnp.mean(xf * xf, axis=-1, keepdims=True)
        u_ref[...] = (xf * lax.rsqrt(ms + EPS) * nw_ref[...]).astype(BF16)

    acc = _dot_nt(u_ref[...], w_ref[...].astype(BF16))
    is_q = jnp.logical_and(j >= Q_TILE_LO, j < Q_TILE_HI)
    is_rope = is_q | (j == KS_TILE) | (j == KW_TILE)

    @pl.when(jnp.logical_not(is_rope))
    def _():
        o_ref[...] = acc.astype(o_ref.dtype)

    @pl.when(is_rope)
    def _():
        cos = cos_ref[...]
        sin = sin_ref[...]
        scale = jnp.where(is_q, DH ** -0.5 * LOG2E, 1.0).astype(F32)
        for h in range(INPROJ_TN // DH):
            xh = acc[:, h * DH:(h + 1) * DH]
            o_ref[:, h * DH:(h + 1) * DH] = (_rope(xh, cos, sin) * scale).astype(o_ref.dtype)

    def chunked(dst_ref):
        n_chunks = acc_scr.shape[1] // CMP_STRIDE
        for g in range(NSA_GROUPS):
            acc_scr[g] = acc[:, g * DH:(g + 1) * DH]
            for l in range(CMP_STRIDE):
                rows_l = acc_scr[g, pl.ds(l, n_chunks, stride=CMP_STRIDE), :]
                dst_ref[g, :, l * DH:(l + 1) * DH] = rows_l.astype(dst_ref.dtype)

    @pl.when(j == KC_TILE)
    def _():
        chunked(kcx_ref)

    @pl.when(j == VC_TILE)
    def _():
        chunked(vcx_ref)


def _inproj(x2, norm_w, w_t, cosf, sinf):
    seq = x2.shape[0]
    tm = min(seq, 1024)
    tn = INPROJ_TN
    chunk_w = CMP_STRIDE * DH
    chunk_spec = pl.BlockSpec((NSA_GROUPS, tm // CMP_STRIDE, chunk_w), lambda i, j: (0, i, 0))
    chunk_shape = jax.ShapeDtypeStruct((NSA_GROUPS, seq // CMP_STRIDE, chunk_w), BF16)
    return pl.pallas_call(
        _inproj_kernel,
        grid=(seq // tm, ATT_W // tn),
        in_specs=[pl.BlockSpec((tm, D_MODEL), lambda i, j: (i, 0)),
                  pl.BlockSpec((1, D_MODEL), lambda i, j: (0, 0)),
                  pl.BlockSpec((tn, D_MODEL), lambda i, j: (j, 0)),
                  pl.BlockSpec((tm, DH), lambda i, j: (i, 0)),
                  pl.BlockSpec((tm, DH), lambda i, j: (i, 0))],
        out_specs=[pl.BlockSpec((tm, tn), lambda i, j: (i, j)),
                   chunk_spec, chunk_spec,
                   pl.BlockSpec((tm, D_MODEL), lambda i, j: (i, 0))],
        out_shape=[jax.ShapeDtypeStruct((seq, ATT_W), BF16),
                   chunk_shape, chunk_shape,
                   jax.ShapeDtypeStruct((seq, D_MODEL), BF16)],
        scratch_shapes=[pltpu.VMEM((NSA_GROUPS, tm, DH), F32)],
        compiler_params=_params(("arbitrary", "arbitrary")),
        name="inproj_attn",
    )(x2, norm_w.reshape(1, D_MODEL), w_t, cosf, sinf)


SSM_TN = 1024
SRC_Z = NSA_HEADS * DH + 6 * NSA_GROUPS * DH + 3 * NSA_HEADS
SRC_BG = SRC_Z + 2 * SSM_D_INNER + 2 * SSM_GROUPS * SSM_STATE + SSM_HEADS
SSM_SPLIT = COL_GA // SSM_TN


def _nt_matmul_kernel(a_ref, w_ref, o_ref):
    o_ref[...] = _dot_nt(a_ref[...], w_ref[...].astype(BF16)).astype(o_ref.dtype)


def _proj_ssm(u, w_t):
    seq, k = u.shape
    tm = min(seq, 2048)

    def src_rows(i, j):
        jump = (SRC_BG - SRC_Z - SSM_TN * SSM_SPLIT) // 16
        return ((SRC_Z // 16 + (SSM_TN // 16) * j + jump * (j // SSM_SPLIT)) * 16, 0)

    return pl.pallas_call(
        _nt_matmul_kernel,
        grid=(seq // tm, SSM_W // SSM_TN),
        in_specs=[pl.BlockSpec((tm, k), lambda i, j: (i, 0)),
                  pl.BlockSpec((pl.Element(SSM_TN), pl.Element(k)), src_rows)],
        out_specs=pl.BlockSpec((tm, SSM_TN), lambda i, j: (i, j)),
        out_shape=jax.ShapeDtypeStruct((seq, SSM_W), BF16),
        compiler_params=_params(("arbitrary", "arbitrary")),
        name="inproj_ssm",
    )(u, w_t)


def _proj_small(u, w_small_t):
    seq, k = u.shape
    n = w_small_t.shape[0]
    tm = min(seq, 1024)
    return pl.pallas_call(
        _nt_matmul_kernel,
        grid=(seq // tm,),
        in_specs=[pl.BlockSpec((tm, k), lambda i: (i, 0)),
                  pl.BlockSpec((n, k), lambda i: (0, 0))],
        out_specs=pl.BlockSpec((tm, n), lambda i: (i, 0)),
        out_shape=jax.ShapeDtypeStruct((seq, n), F32),
        compiler_params=_params(("arbitrary",)),
        name="inproj_small",
    )(u, w_small_t)


def _compress_kernel(xk_ref, xv_ref, pk_ref, pv_ref, w1k_ref, w2k_ref, w1v_ref, w2v_ref,
                     cos_ref, sin_ref, kc_ref, vc_ref):
    half = CMP_STRIDE * DH
    ncp = xk_ref.shape[1]

    def comp(x_ref, p_ref, w1_ref, w2_ref):
        x = x_ref[0]
        wt = w1_ref[0:half, :]
        wb = w1_ref[half:2 * half, :]
        a = _dot(x, wt)
        b = _dot(x, wb)
        pb = _dot(p_ref[...], wt)[0:1] + _dot(p_ref[...], wb)[1:2]
        h = a + pltpu.roll(b, ncp - 1, 0) + pb
        h = h * jax.nn.sigmoid(h)
        return _dot(h.astype(BF16), w2_ref[...])

    kc = comp(xk_ref, pk_ref, w1k_ref, w2k_ref)
    kc_ref[0] = _rope(kc, cos_ref[...], sin_ref[...]).astype(BF16)
    vc_ref[0] = comp(xv_ref, pv_ref, w1v_ref, w2v_ref).astype(BF16)


def _compress(xk, xv, pk, pv, w1k, w2k, w1v, w2v, cos_c, sin_c):
    g, ncp, width = xk.shape
    full = lambda shape: pl.BlockSpec(shape, lambda i: (0,) * len(shape))
    per_g = pl.BlockSpec((1, ncp, width), lambda i: (i, 0, 0))
    out_g = pl.BlockSpec((1, ncp, DH), lambda i: (i, 0, 0))
    return pl.pallas_call(
        _compress_kernel,
        grid=(g,),
        in_specs=[per_g, per_g, full(pk.shape), full(pv.shape), full(w1k.shape), full(w2k.shape),
                  full(w1v.shape), full(w2v.shape), full(cos_c.shape), full(sin_c.shape)],
        out_specs=[out_g, out_g],
        out_shape=[jax.ShapeDtypeStruct((g, ncp, DH), BF16)] * 2,
        compiler_params=_params(("arbitrary",)),
        name="compress",
    )(xk, xv, pk, pv, w1k, w2k, w1v, w2v, cos_c, sin_c)


CMP_TQ = 256


def _cmp_attn_kernel(q_ref, kc_ref, vc_ref, mt_ref, oc_ref, mask_ref):
    i = pl.program_id(0)
    qbase = i * CMP_TQ
    ncp = kc_ref.shape[1]
    nsel = mt_ref.shape[0]
    row = lax.broadcasted_iota(jnp.int32, (CMP_TQ, ncp), 0)
    col = lax.broadcasted_iota(jnp.int32, (CMP_TQ, ncp), 1)
    bias1 = jnp.where((col * CMP_STRIDE + (CMP_BLOCK - 1)) <= qbase + row, 0.0, MASK_NEG)
    bias = jnp.concatenate([bias1] * NSA_REP, axis=0)
    has_key = jnp.concatenate([bias1[:, 0:1] == 0.0] * NSA_REP, axis=0)
    blk = lax.broadcasted_iota(jnp.int32, (nsel, CMP_TQ), 0)
    cur = (qbase + lax.broadcasted_iota(jnp.int32, (nsel, CMP_TQ), 1)) >> 6
    future = blk > cur
    forced = (blk == 0) | (blk == cur) | (blk == cur - 1)
    blk_f = blk.astype(F32)
    mt = mt_ref[...]
    importance = []
    for g in range(NSA_GROUPS):
        qg = jnp.concatenate(
            [q_ref[:, (g * NSA_REP + r) * DH:(g * NSA_REP + r + 1) * DH] for r in range(NSA_REP)], axis=0)
        s = _dot_nt(qg, kc_ref[g]) + bias
        p = jnp.exp2(s - jnp.max(s, axis=-1, keepdims=True)).astype(BF16)
        ov = _dot(p, jnp.concatenate([vc_ref[g], jnp.ones((ncp, LANES), BF16)], axis=1))
        inv = jnp.where(has_key, 1.0 / ov[:, DH:], 0.0)
        o = ov[:, :DH] * inv
        imp = None
        for r in range(NSA_REP):
            h = g * NSA_REP + r
            rows_r = slice(r * CMP_TQ, (r + 1) * CMP_TQ)
            oc_ref[:, h * DH:(h + 1) * DH] = o[rows_r].astype(oc_ref.dtype)
            part = _dot_nt(p[rows_r], mt) * inv[rows_r]
            imp = part if imp is None else imp + part
        importance.append(imp.T)
    v = jnp.where(future | forced, -1.0, jnp.stack(importance))
    for _ in range(SEL_TOPK - N_FORCED):
        mx = jnp.max(v, axis=1, keepdims=True)
        first = jnp.min(jnp.where(v == mx, blk_f, float(nsel)), axis=1, keepdims=True)
        v = jnp.where(blk_f == first, TAKEN, v)
    bias_t = jnp.where(forced | (v == TAKEN), 0.0, MASK_NEG)
    for g in range(NSA_GROUPS):
        mask_ref[g] = bias_t[g].T.astype(mask_ref.dtype)


def _cmp_attn(main, kc, vc, mt, seq):
    nsel = mt.shape[0]
    ncp = kc.shape[1]
    return pl.pallas_call(
        _cmp_attn_kernel,
        grid=(seq // CMP_TQ,),
        in_specs=[pl.BlockSpec((CMP_TQ, NSA_HEADS * DH), lambda i: (i, COL_Q // (NSA_HEADS * DH))),
                  pl.BlockSpec((NSA_GROUPS, ncp, DH), lambda i: (0, 0, 0)),
                  pl.BlockSpec((NSA_GROUPS, ncp, DH), lambda i: (0, 0, 0)),
                  pl.BlockSpec((nsel, ncp), lambda i: (0, 0))],
        out_specs=[pl.BlockSpec((CMP_TQ, NSA_HEADS * DH), lambda i: (i, 0)),
                   pl.BlockSpec((NSA_GROUPS, CMP_TQ, nsel), lambda i: (0, i, 0))],
        out_shape=[jax.ShapeDtypeStruct((seq, NSA_HEADS * DH), BF16),
                   jax.ShapeDtypeStruct((NSA_GROUPS, seq, nsel), BF16)],
        compiler_params=_params(("arbitrary",)),
        name="cmp_attn_topk",
    )(main, kc, vc, mt)


ATT_TQ = 512
ATT_TK = 512
WIN_SPAN = WINDOW + ATT_TQ // 2


def _nsa_attn_kernel(q_ref, mask_ref, ks_ref, vs_ref, kw_ref, vw_ref, oc_ref, gate_ref, *rest):
    n_w = (len(rest) - 4) // 2
    w_refs, o_ref, wb_refs = rest[:n_w], rest[n_w], rest[n_w + 1:2 * n_w + 1]
    qa_scr, acc_scr, m_scr = rest[2 * n_w + 1:]
    for src, dst in zip(w_refs, wb_refs):
        dst[...] = src[...].astype(dst.dtype)
    i = pl.program_id(1)
    qbase = i * ATT_TQ
    assert mask_ref.shape[2] == LANES, "the block-selection bias rides in one extra 128-wide contraction slab"
    bias = mask_ref[0]
    for r in range(NSA_REP):
        qa_scr[r] = jnp.concatenate([q_ref[:, r * DH:(r + 1) * DH], bias], axis=1)
    m_scr[...] = jnp.full(m_scr.shape, NEG_INF, F32)
    acc_scr[...] = jnp.zeros(acc_scr.shape, F32)
    krow = lax.broadcasted_iota(jnp.int32, (ATT_TK, LANES), 0)
    kblk = lax.broadcasted_iota(jnp.int32, (ATT_TK, LANES), 1)

    half = ATT_TQ // 2

    def key_tile(j):
        k0 = pl.multiple_of(j * ATT_TK, ATT_TK)
        onehot = jnp.where(((krow + k0) >> 6) == kblk, 1.0, 0.0).astype(BF16)
        k_aug = jnp.concatenate([ks_ref[pl.ds(k0, ATT_TK), :], onehot], axis=1)
        v_aug = jnp.concatenate([vs_ref[pl.ds(k0, ATT_TK), :], jnp.ones((ATT_TK, LANES), BF16)], axis=1)
        return k_aug, v_aug

    def online_update(r, rows, s, v_aug):
        m_prev = m_scr[r, rows, :]
        m_new = jnp.maximum(m_prev, jnp.max(s, axis=-1, keepdims=True))
        alpha = jnp.exp2(m_prev - m_new)
        p = jnp.exp2(s - jnp.concatenate([m_new] * (s.shape[1] // LANES), axis=1))
        acc_scr[r, rows, :] = (acc_scr[r, rows, :] * jnp.concatenate([alpha, alpha], axis=1)
                               + _dot(p.astype(BF16), v_aug))
        m_scr[r, rows, :] = m_new

    def step(j, diagonal):
        k_aug, v_aug = key_tile(j)
        if not diagonal:
            for r in range(NSA_REP):
                online_update(r, slice(0, ATT_TQ), _dot_nt(qa_scr[r], k_aug), v_aug)
            return
        r_ = lax.broadcasted_iota(jnp.int32, (half, half), 0)
        c_ = lax.broadcasted_iota(jnp.int32, (half, half), 1)
        tri_bias = jnp.where(c_ <= r_, 0.0, MASK_NEG)
        for r in range(NSA_REP):
            s0 = _dot_nt(qa_scr[r, 0:half, :], k_aug[:half]) + tri_bias
            online_update(r, slice(0, half), s0, v_aug[:half])
            s1 = _dot_nt(qa_scr[r, half:ATT_TQ, :], k_aug)
            s1 = jnp.concatenate([s1[:, :half], s1[:, half:] + tri_bias], axis=1)
            online_update(r, slice(half, ATT_TQ), s1, v_aug)

    def body(jj, carry):
        for u in range(4):
            step(4 * jj + u, False)
        return carry

    lax.fori_loop(0, i // 4, body, 0)
    done = (i // 4) * 4

    @pl.when((i & 2) != 0)
    def _():
        step(done, False)
        step(done + 1, False)

    @pl.when((i & 1) != 0)
    def _():
        step(i - 1, False)

    step(i, True)

    first_gate = SSM_HEADS + 3 * NSA_REP * pl.program_id(0)
    gates = jax.nn.sigmoid(pltpu.roll(gate_ref[...], SMALL_W - first_gate, 1))
    r_ = lax.broadcasted_iota(jnp.int32, (half, WIN_SPAN), 0)
    c_ = lax.broadcasted_iota(jnp.int32, (half, WIN_SPAN), 1)
    for hh in range(2):
        rows = slice(hh * half, (hh + 1) * half)
        q0 = qbase + hh * half
        start = pl.multiple_of(jnp.maximum(q0 - WINDOW, 0), half)
        kw = kw_ref[pl.ds(start, WIN_SPAN), :]
        vw_aug = jnp.concatenate([vw_ref[pl.ds(start, WIN_SPAN), :], jnp.ones((WIN_SPAN, LANES), BF16)], axis=1)
        diff = (q0 + r_) - (start + c_)
        win_bias = jnp.where((diff >= 0) & (diff < WINDOW), 0.0, MASK_NEG)
        for r in range(NSA_REP):
            s = _dot_nt(q_ref[rows, r * DH:(r + 1) * DH], kw) + win_bias
            p = jnp.exp2(s - jnp.max(s, axis=-1, keepdims=True))
            ow = _dot(p.astype(BF16), vw_aug)
            o_win = ow[:, :DH] / jnp.maximum(ow[:, DH:], TINY)
            acc = acc_scr[r, rows, :]
            o_sel = acc[:, :DH] / jnp.maximum(acc[:, DH:], TINY)
            g_r = gates[rows]
            o = (g_r[:, 3 * r:3 * r + 1] * oc_ref[rows, r * DH:(r + 1) * DH].astype(F32)
                 + g_r[:, 3 * r + 1:3 * r + 2] * o_sel
                 + g_r[:, 3 * r + 2:3 * r + 3] * o_win)
            o_ref[rows, r * DH:(r + 1) * DH] = o.astype(o_ref.dtype)


def _nsa_attn(main, small, mask, o_cmp, weights, seq):
    gw = NSA_REP * DH
    nsel = mask.shape[2]
    n_i = seq // ATT_TQ
    steps = NSA_GROUPS * n_i
    kv_spec = lambda col: pl.BlockSpec((seq, DH), lambda g, i: (0, col // DH + g))
    w_in_specs, w_out_specs, w_out_shapes = [], [], []
    for w in weights:
        _, rows, cols = w.shape
        slab = rows // steps
        assert slab * steps == rows and slab % 16 == 0
        w_in_specs.append(pl.BlockSpec((None, slab, cols), lambda g, i: (0, g * n_i + i, 0)))
        w_out_specs.append(pl.BlockSpec((slab, cols), lambda g, i: (g * n_i + i, 0)))
        w_out_shapes.append(jax.ShapeDtypeStruct((rows, cols), BF16))
    return pl.pallas_call(
        _nsa_attn_kernel,
        grid=(NSA_GROUPS, n_i),
        in_specs=[pl.BlockSpec((ATT_TQ, gw), lambda g, i: (i, COL_Q // gw + g)),
                  pl.BlockSpec((1, ATT_TQ, nsel), lambda g, i: (g, i, 0)),
                  kv_spec(COL_KS), kv_spec(COL_VS), kv_spec(COL_KW), kv_spec(COL_VW),
                  pl.BlockSpec((ATT_TQ, gw), lambda g, i: (i, g)),
                  pl.BlockSpec((ATT_TQ, SMALL_W), lambda g, i: (i, 0))] + w_in_specs,
        out_specs=[pl.BlockSpec((ATT_TQ, gw), lambda g, i: (i, g))] + w_out_specs,
        out_shape=[jax.ShapeDtypeStruct((seq, NSA_HEADS * DH), BF16)] + w_out_shapes,
        scratch_shapes=[pltpu.VMEM((NSA_REP, ATT_TQ, 2 * DH), BF16),
                        pltpu.VMEM((NSA_REP, ATT_TQ, 2 * DH), F32),
                        pltpu.VMEM((NSA_REP, ATT_TQ, LANES), F32)],
        compiler_params=_params(("arbitrary", "arbitrary")),
        name="nsa_sel_win_attn",
    )(main, mask, main, main, main, main, o_cmp, small, *weights)


HALO = 8
GROUP_W = SSM_D_INNER // SSM_GROUPS
HEADS_PER_GROUP = SSM_HEADS // SSM_GROUPS
MAMBA_ROWS = 2 * SSD_CHUNK


def _silu(x):
    h = 0.5 * x
    return h + h * jnp.tanh(h)


def _mamba_kernel(z_ref, xs_ref, b_ref, c_ref, dt_ref,
                  cwx_ref, cwb_ref, cwc_ref, cbx_ref, cbb_ref, cbc_ref,
                  dtb_ref, alog_ref, dfull_ref, nw_ref, expand2_ref, tril3_ref,
                  o_ref, state_scr, xs_ext, b_ext, c_ext):
    L = SSD_CHUNK

    @pl.when(pl.program_id(0) == 0)
    def _():
        state_scr[...] = jnp.zeros(state_scr.shape, F32)
        for carry in (xs_ext, b_ext, c_ext):
            carry[...] = jnp.zeros(carry.shape, F32)

    for sub in range(z_ref.shape[0] // L):
        _mamba_chunk(slice(sub * L, (sub + 1) * L), z_ref, xs_ref, b_ref, c_ref, dt_ref,
                     cwx_ref, cwb_ref, cwc_ref, cbx_ref, cbb_ref, cbc_ref,
                     dtb_ref, alog_ref, dfull_ref, nw_ref, expand2_ref, tril3_ref,
                     o_ref, state_scr, xs_ext, b_ext, c_ext)


def _mamba_chunk(rows, z_ref, xs_ref, b_ref, c_ref, dt_ref,
                 cwx_ref, cwb_ref, cwc_ref, cbx_ref, cbb_ref, cbc_ref,
                 dtb_ref, alog_ref, dfull_ref, nw_ref, expand2_ref, tril3_ref,
                 o_ref, state_scr, xs_ext, b_ext, c_ext):
    L = SSD_CHUNK

    def conv_silu(x_ref, carry, w_ref, bias_ref):
        x = x_ref[rows, :].astype(F32)
        first_row = lax.broadcasted_iota(jnp.int32, (HALO, x.shape[1]), 0) == 0
        r = None
        for k in range(CONV_WIDTH - 1):
            t = w_ref[k:k + 1, :] * x
            if r is not None:
                t = t + r
            rolled = pltpu.roll(t, 1, 0)
            top = jnp.where(first_row, carry[k:k + 1, :], rolled[0:HALO])
            carry[k:k + 1, :] = t[L - 1:L, :]
            r = jnp.concatenate([top, rolled[HALO:]], axis=0)
        h = r + w_ref[CONV_WIDTH - 1:CONV_WIDTH, :] * x + bias_ref[...]
        return h + h * jnp.tanh(h)

    xs = conv_silu(xs_ref, xs_ext, cwx_ref, cbx_ref)
    bm = conv_silu(b_ref, b_ext, cwb_ref, cbb_ref).astype(BF16)
    cm = conv_silu(c_ref, c_ext, cwc_ref, cbc_ref).astype(BF16)

    xdt_raw = dt_ref[rows, :] + dtb_ref[...]
    dt = jnp.maximum(xdt_raw, 0.0) + jnp.log1p(jnp.exp(-jnp.abs(xdt_raw)))
    adt = dt * (-jnp.exp(alog_ref[...]))
    acs = _dot(tril3_ref[...], jnp.concatenate(_split3(adt), axis=0)) * LOG2E
    acs_t = acs.T
    a_last = acs[L - 1:L, :]
    stacked = jnp.concatenate([dt, jnp.exp2(acs), jnp.exp2(a_last - acs)], axis=0)
    wide = _dot(jnp.concatenate(_split2(stacked), axis=1), expand2_ref[...])
    dt_w = wide[0:L]
    in_decay_w = wide[L:2 * L]
    out_decay_w = wide[2 * L:3 * L]

    xdt = xs * dt_w
    xst = xdt * out_decay_w
    causal = lax.broadcasted_iota(jnp.int32, (L, L), 0) >= lax.broadcasted_iota(jnp.int32, (L, L), 1)
    causal_bias = jnp.where(causal, 0.0, MASK_NEG)
    lane = lax.broadcasted_iota(jnp.int32, (L, LANES), 1)
    low_half = lane < SSM_HEAD_DIM
    chunk_decay = jnp.exp2(acs_t[:, L - 1:L])

    for g in range(SSM_GROUPS):
        gs = slice(g * GROUP_W, (g + 1) * GROUP_W)
        bg = bm[:, g * SSM_STATE:(g + 1) * SSM_STATE]
        cg = cm[:, g * SSM_STATE:(g + 1) * SSM_STATE]
        cb = _dot_nt(cg, bg)
        prev = state_scr[gs, :]
        y_off = _dot_nt(cg, prev.astype(BF16)) * in_decay_w[:, gs]
        pieces = []
        for hp in range(HEADS_PER_GROUP // 2):
            lhs = []
            for e in range(2):
                h = g * HEADS_PER_GROUP + 2 * hp + e
                decay = jnp.exp2(acs[:, h:h + 1] - acs_t[h:h + 1, :] + causal_bias)
                lhs.append((cb * decay).astype(BF16))
            slab = xdt[:, g * GROUP_W + hp * LANES:g * GROUP_W + (hp + 1) * LANES]
            rhs = jnp.concatenate([jnp.where(low_half, slab, 0.0), jnp.where(low_half, 0.0, slab)], axis=0)
            pieces.append(_dot(jnp.concatenate(lhs, axis=1), rhs.astype(BF16)))
        y_g = jnp.concatenate(pieces, axis=1) + y_off
        y_g = y_g + dfull_ref[:, gs] * xs[:, gs]
        y_g = y_g * _silu(z_ref[rows, gs].astype(F32))
        ms = jnp.mean(y_g * y_g, axis=-1, keepdims=True)
        o_ref[rows, gs] = (y_g * lax.rsqrt(ms + EPS) * nw_ref[:, gs]).astype(o_ref.dtype)
        new = _dot(xst[:, gs].T.astype(BF16), bg)
        cd = jnp.broadcast_to(chunk_decay[g * HEADS_PER_GROUP:(g + 1) * HEADS_PER_GROUP], (HEADS_PER_GROUP, LANES))
        cd = jnp.broadcast_to(cd[:, None, :], (HEADS_PER_GROUP, SSM_HEAD_DIM, LANES)).reshape(GROUP_W, LANES)
        state_scr[gs, :] = prev * cd + new


def _mamba(main, small, cwx, cwb, cwc, cbx, cbb, cbc, dtb, alog, dfull, nw, expand2, tril3, seq):
    L = MAMBA_ROWS
    bw = SSM_GROUPS * SSM_STATE
    full = lambda a: pl.BlockSpec(a.shape, lambda c: (0,) * a.ndim)
    consts = [cwx, cwb, cwc, cbx, cbb, cbc, dtb, alog, dfull, nw, expand2, tril3]
    return pl.pallas_call(
        _mamba_kernel,
        grid=(seq // L,),
        in_specs=[pl.BlockSpec((L, SSM_D_INNER), lambda c: (c, COL_Z // SSM_D_INNER)),
                  pl.BlockSpec((L, SSM_D_INNER), lambda c: (c, COL_XS // SSM_D_INNER)),
                  pl.BlockSpec((L, bw), lambda c: (c, COL_B // bw)),
                  pl.BlockSpec((L, bw), lambda c: (c, COL_C // bw)),
                  pl.BlockSpec((L, LANES), lambda c: (c, 0))] + [full(a) for a in consts],
        out_specs=pl.BlockSpec((L, SSM_D_INNER), lambda c: (c, 0)),
        out_shape=jax.ShapeDtypeStruct((seq, SSM_D_INNER), BF16),
        scratch_shapes=[pltpu.VMEM((SSM_D_INNER, SSM_STATE), F32),
                        pltpu.VMEM((HALO, SSM_D_INNER), F32),
                        pltpu.VMEM((HALO, bw), F32),
                        pltpu.VMEM((HALO, bw), F32)],
        compiler_params=_params(("arbitrary",)),
        name="mamba2_ssd",
    )(main, main, main, main, small, *consts)


def _merge_kernel(a_ref, b_ref, pa_ref, pb_ref, ga_ref, gb_ref, o_ref):
    ya = _dot(a_ref[...], pa_ref[...])
    yb = _dot(b_ref[...], pb_ref[...])
    o = jax.nn.sigmoid(ga_ref[...].astype(F32)) * ya + jax.nn.sigmoid(gb_ref[...].astype(F32)) * yb
    o_ref[...] = o.astype(o_ref.dtype)


def _merge(o_nsa, o_ssm, pa, pb, main, seq):
    tm = min(seq, 512)
    tn = 1024
    return pl.pallas_call(
        _merge_kernel,
        grid=(seq // tm, D_MODEL // tn),
        in_specs=[pl.BlockSpec((tm, NSA_HEADS * DH), lambda i, j: (i, 0)),
                  pl.BlockSpec((tm, SSM_D_INNER), lambda i, j: (i, 0)),
                  pl.BlockSpec((NSA_HEADS * DH, tn), lambda i, j: (0, j)),
                  pl.BlockSpec((SSM_D_INNER, tn), lambda i, j: (0, j)),
                  pl.BlockSpec((tm, tn), lambda i, j: (i, COL_GA // tn + j)),
                  pl.BlockSpec((tm, tn), lambda i, j: (i, COL_GB // tn + j))],
        out_specs=pl.BlockSpec((tm, tn), lambda i, j: (i, j)),
        out_shape=jax.ShapeDtypeStruct((seq, D_MODEL), BF16),
        compiler_params=_params(("arbitrary", "arbitrary")),
        name="gated_merge",
    )(o_nsa, o_ssm, pa, pb, main, main)


def _outproj_kernel(m_ref, w_ref, x_ref, nw_ref, h_ref, hn_ref):
    h = x_ref[...] + _dot(m_ref[...], w_ref[...])
    h_ref[...] = h
    ms = jnp.mean(h * h, axis=-1, keepdims=True)
    hn_ref[...] = (h * lax.rsqrt(ms + EPS) * nw_ref[...]).astype(hn_ref.dtype)


def _outproj(merged, w_out, x2, norm_w, seq):
    tm = min(seq, 512)
    row = pl.BlockSpec((tm, D_MODEL), lambda i: (i, 0))
    return pl.pallas_call(
        _outproj_kernel,
        grid=(seq // tm,),
        in_specs=[row, pl.BlockSpec((D_MODEL, D_MODEL), lambda i: (0, 0), pipeline_mode=pl.Buffered(1)), row,
                  pl.BlockSpec((1, D_MODEL), lambda i: (0, 0))],
        out_specs=[row, row],
        out_shape=[jax.ShapeDtypeStruct((seq, D_MODEL), F32), jax.ShapeDtypeStruct((seq, D_MODEL), BF16)],
        compiler_params=_params(("arbitrary",)),
        name="outproj_residual_norm",
    )(merged, w_out, x2, norm_w.reshape(1, D_MODEL))


def _mlp_kernel(hn_ref, wu_ref, wd_ref, h_ref, nw_ref, o_ref, acc_scr):
    f = pl.program_id(1)

    @pl.when(f == 0)
    def _():
        acc_scr[...] = jnp.zeros(acc_scr.shape, F32)

    up = jnp.maximum(_dot(hn_ref[...], wu_ref[...]), 0.0)
    acc_scr[...] += _dot((up * up).astype(BF16), wd_ref[...])

    @pl.when(f == pl.num_programs(1) - 1)
    def _():
        h = h_ref[...] + acc_scr[...]
        ms = jnp.mean(h * h, axis=-1, keepdims=True)
        o_ref[...] = h * lax.rsqrt(ms + EPS) * nw_ref[...]


def _mlp(hn, w_up, w_down, h1, norm_w, seq):
    tm = min(seq, 512)
    tf = 1024
    row = lambda i, f: (i, 0)
    return pl.pallas_call(
        _mlp_kernel,
        grid=(seq // tm, D_FF // tf),
        in_specs=[pl.BlockSpec((tm, D_MODEL), row),
                  pl.BlockSpec((D_MODEL, tf), lambda i, f: (0, f)),
                  pl.BlockSpec((tf, D_MODEL), lambda i, f: (f, 0)),
                  pl.BlockSpec((tm, D_MODEL), row),
                  pl.BlockSpec((1, D_MODEL), lambda i, f: (0, 0))],
        out_specs=pl.BlockSpec((tm, D_MODEL), row),
        out_shape=jax.ShapeDtypeStruct((seq, D_MODEL), F32),
        scratch_shapes=[pltpu.VMEM((tm, D_MODEL), F32)],
        compiler_params=_params(("arbitrary", "arbitrary")),
        name="mlp_final_norm",
    )(hn, w_up, w_down, h1, norm_w.reshape(1, D_MODEL))


def _cmp_to_sel(seq):
    n_cmp = (seq - CMP_BLOCK) // CMP_STRIDE + 1
    n_sel = seq // SEL_BLOCK
    c_start = np.arange(n_cmp) * CMP_STRIDE
    s_start = np.arange(n_sel) * SEL_BLOCK
    overlap = np.clip(np.minimum(c_start[:, None] + CMP_BLOCK, s_start[None, :] + SEL_BLOCK)
                      - np.maximum(c_start[:, None], s_start[None, :]), 0, None)
    m = np.zeros((seq // CMP_STRIDE, LANES), np.float32)
    m[:n_cmp, :n_sel] = overlap / CMP_STRIDE
    return jnp.asarray(m.T, BF16)


def kernel(x, positions, norm_mix_w, w_in, cmp_pos_k, cmp_pos_v, cmp_k_w1, cmp_k_w2, cmp_v_w1, cmp_v_w2, conv_w, conv_b, dt_bias, a_log, ssm_d, ssm_norm_w, w_proj_nsa, w_proj_ssm, w_out, norm_mlp_w, w_up, w_down, norm_final_w):
    bsz, seq, _ = x.shape
    assert bsz == 1 and seq % 1024 == 0 and SEL_TOPK <= seq // SEL_BLOCK <= LANES
    assert w_in.shape[0] == 1, "one layer"
    x2 = x.reshape(seq, D_MODEL)

    o_gate = NSA_HEADS * DH + 6 * NSA_GROUPS * DH
    o_z = o_gate + 3 * NSA_HEADS
    o_dt = o_z + 2 * SSM_D_INNER + 2 * SSM_GROUPS * SSM_STATE
    o_bg = o_dt + SSM_HEADS
    assert o_z == SRC_Z and o_bg == SRC_BG and o_gate == ATT_W
    w_t = jnp.swapaxes(w_in, 1, 2)[0]
    n_gate = 3 * NSA_HEADS
    w_small_t = jnp.concatenate([w_t[o_dt:o_dt + SSM_HEADS], w_t[o_gate:o_gate + n_gate],
                                 jnp.zeros((SMALL_W - SSM_HEADS - n_gate, D_MODEL), F32)], axis=0)

    pos = positions.reshape(seq)
    cosf, sinf = _rope_tables(pos)
    att, kc_chunks, vc_chunks, u = _inproj(x2, norm_mix_w[0], w_t, cosf, sinf)
    ssm = _proj_ssm(u, w_t)
    small = _proj_small(u, w_small_t)

    ncp = seq // CMP_STRIDE
    def pos2(p):
        p = p.reshape(2, CMP_STRIDE * DH)
        return jnp.concatenate([p, jnp.zeros((14, CMP_STRIDE * DH), F32)], axis=0).astype(BF16)
    pos_end = pos[CMP_BLOCK - 1::CMP_STRIDE]
    cos_c, sin_c = _rope_tables(jnp.concatenate([pos_end, jnp.zeros((ncp - pos_end.shape[0],), pos.dtype)]))
    kc, vc = _compress(kc_chunks, vc_chunks, pos2(cmp_pos_k[0]), pos2(cmp_pos_v[0]),
                       cmp_k_w1[0].astype(BF16), cmp_k_w2[0].astype(BF16),
                       cmp_v_w1[0].astype(BF16), cmp_v_w2[0].astype(BF16), cos_c, sin_c)

    o_cmp, sel_bias = _cmp_attn(att, kc, vc, _cmp_to_sel(seq), seq)
    o_nsa, pa, pb, wo, wu, wd = _nsa_attn(att, small, sel_bias, o_cmp,
                                          (w_proj_nsa, w_proj_ssm, w_out, w_up, w_down), seq)

    cw = conv_w[0] * 0.5
    cb_ = conv_b[0].reshape(1, -1) * 0.5
    lane_pad = lambda a: jnp.concatenate([a.reshape(1, -1), jnp.zeros((1, LANES - a.shape[-1]), F32)], axis=1)
    head_of = np.arange(SSM_D_INNER) // SSM_HEAD_DIM
    expand = np.arange(LANES)[:, None] == head_of[None, :]
    expand2 = jnp.asarray(np.concatenate([expand, expand], axis=0), BF16)
    tril = np.tril(np.ones((SSD_CHUNK, SSD_CHUNK), np.float32))
    tril3 = jnp.asarray(np.concatenate([tril, tril, tril], axis=1), BF16)
    o_ssm = _mamba(ssm, small, cw[:, :4096], cw[:, 4096:5120], cw[:, 5120:],
                   cb_[:, :4096], cb_[:, 4096:5120], cb_[:, 5120:],
                   lane_pad(dt_bias[0]), lane_pad(a_log[0]),
                   jnp.repeat(ssm_d[0], SSM_HEAD_DIM).reshape(1, -1), ssm_norm_w[0].reshape(1, -1),
                   expand2, tril3, seq)

    merged = _merge(o_nsa, o_ssm, pa, pb, ssm, seq)
    h1, hn = _outproj(merged, wo, x2, norm_mlp_w[0], seq)
    out = _mlp(hn, wu, wd, h1, norm_final_w, seq)
    return out.reshape(bsz, seq, D_MODEL)
```

```python
import math

import jax
import jax.numpy as jnp
import numpy as np
from jax import lax
from jax.experimental import pallas as pl
from jax.experimental.pallas import tpu as pltpu

F32 = jnp.float32
BF16 = jnp.bfloat16

D_MODEL = 2048
NSA_HEADS = 16
NSA_GROUPS = 4
NSA_REP = NSA_HEADS // NSA_GROUPS
DH = 128
CMP_BLOCK = 32
CMP_STRIDE = 16
SEL_BLOCK = 64
SEL_TOPK = 16
WINDOW = 512
ROPE_THETA = 10000.0
N_FORCED = 3
SSM_D_INNER = 4096
SSM_HEAD_DIM = 64
SSM_HEADS = 64
SSM_GROUPS = 8
SSM_STATE = 128
CONV_WIDTH = 4
SSD_CHUNK = 128
D_FF = 4 * D_MODEL
EPS = 1e-6
NEG_INF = -1e30
TINY = 1e-30
MASK_NEG = NEG_INF
TAKEN = -2.0
LOG2E = math.log2(math.e)
LANES = 128
VMEM_LIMIT = 56 * 1024 * 1024

COL_Q = 0
COL_KC = 2048
COL_VC = 2560
COL_KS = 3072
COL_VS = 3584
COL_KW = 4096
COL_VW = 4608
ATT_W = 5120
COL_Z = 0
COL_XS = 4096
COL_B = 8192
COL_C = 9216
COL_GA = 10240
COL_GB = 12288
SSM_W = 14336
SMALL_W = 128


def _dot(a, b):
    return jnp.dot(a, b, preferred_element_type=F32)


def _dot_nt(a, b):
    return lax.dot_general(a, b, (((1,), (1,)), ((), ())), preferred_element_type=F32)


def _params(sem):
    return pltpu.CompilerParams(dimension_semantics=sem, vmem_limit_bytes=VMEM_LIMIT)


def _rope(x, cos, sin_signed):
    return x * cos + pltpu.roll(x, DH // 2, 1) * sin_signed


def _split2(x):
    hi = x.astype(BF16)
    lo = (x - hi.astype(F32)).astype(BF16)
    return hi, lo


def _split3(x):
    hi = x.astype(BF16)
    r = x - hi.astype(F32)
    mid = r.astype(BF16)
    lo = (r - mid.astype(F32)).astype(BF16)
    return hi, mid, lo


def _rope_table_kernel(pos_ref, inv_ref, sign_ref, cos_ref, sin_ref):
    ang = pos_ref[...].astype(F32) * inv_ref[...]
    cos_ref[...] = jnp.cos(ang)
    sin_ref[...] = jnp.sin(ang) * sign_ref[...]


def _rope_tables(positions):
    seq = positions.shape[0]
    half = DH // 2
    inv = ROPE_THETA ** (-jnp.arange(0, DH, 2, dtype=F32) / DH)
    inv_full = jnp.concatenate([inv, inv]).reshape(1, DH)
    sign = jnp.concatenate([-jnp.ones((half,), F32), jnp.ones((half,), F32)]).reshape(1, DH)
    tm = math.gcd(seq, 1024)
    return pl.pallas_call(
        _rope_table_kernel,
        grid=(seq // tm,),
        in_specs=[pl.BlockSpec((tm, 1), lambda i: (i, 0)),
                  pl.BlockSpec((1, DH), lambda i: (0, 0)),
                  pl.BlockSpec((1, DH), lambda i: (0, 0))],
        out_specs=[pl.BlockSpec((tm, DH), lambda i: (i, 0))] * 2,
        out_shape=[jax.ShapeDtypeStruct((seq, DH), F32)] * 2,
        compiler_params=_params(("arbitrary",)),
        name="rope_tables",
    )(positions.reshape(seq, 1), inv_full, sign)


INPROJ_TN = 512
Q_TILE_LO = COL_Q // INPROJ_TN
Q_TILE_HI = COL_KC // INPROJ_TN
KC_TILE = COL_KC // INPROJ_TN
VC_TILE = COL_VC // INPROJ_TN
KS_TILE = COL_KS // INPROJ_TN
KW_TILE = COL_KW // INPROJ_TN


def _inproj_kernel(x_ref, nw_ref, w_ref, cos_ref, sin_ref, o_ref, kcx_ref, vcx_ref, u_ref, acc_scr):
    j = pl.program_id(1)

    @pl.when(j == 0)
    def _():
        xf = x_ref[...]
        ms = jnp.mean(xf * xf, axis=-1, keepdims=True)
        u_ref[...] = (xf * lax.rsqrt(ms + EPS) * nw_ref[...]).astype(BF16)

    acc = _dot_nt(u_ref[...], w_ref[...].astype(BF16))
    is_q = jnp.logical_and(j >= Q_TILE_LO, j < Q_TILE_HI)
    is_rope = is_q | (j == KS_TILE) | (j == KW_TILE)

    @pl.when(jnp.logical_not(is_rope))
    def _():
        o_ref[...] = acc.astype(o_ref.dtype)

    @pl.when(is_rope)
    def _():
        cos = cos_ref[...]
        sin = sin_ref[...]
        scale = jnp.where(is_q, DH ** -0.5 * LOG2E, 1.0).astype(F32)
        for h in range(INPROJ_TN // DH):
            xh = acc[:, h * DH:(h + 1) * DH]
            o_ref[:, h * DH:(h + 1) * DH] = (_rope(xh, cos, sin) * scale).astype(o_ref.dtype)

    def chunked(dst_ref):
        n_chunks = acc_scr.shape[1] // CMP_STRIDE
        for g in range(NSA_GROUPS):
            acc_scr[g] = acc[:, g * DH:(g + 1) * DH]
            for l in range(CMP_STRIDE):
                rows_l = acc_scr[g, pl.ds(l, n_chunks, stride=CMP_STRIDE), :]
                dst_ref[g, :, l * DH:(l + 1) * DH] = rows_l.astype(dst_ref.dtype)

    @pl.when(j == KC_TILE)
    def _():
        chunked(kcx_ref)

    @pl.when(j == VC_TILE)
    def _():
        chunked(vcx_ref)


def _inproj(x2, norm_w, w_t, cosf, sinf):
    seq = x2.shape[0]
    tm = min(seq, 1024)
    tn = INPROJ_TN
    chunk_w = CMP_STRIDE * DH
    chunk_spec = pl.BlockSpec((NSA_GROUPS, tm // CMP_STRIDE, chunk_w), lambda i, j: (0, i, 0))
    chunk_shape = jax.ShapeDtypeStruct((NSA_GROUPS, seq // CMP_STRIDE, chunk_w), BF16)
    return pl.pallas_call(
        _inproj_kernel,
        grid=(seq // tm, ATT_W // tn),
        in_specs=[pl.BlockSpec((tm, D_MODEL), lambda i, j: (i, 0)),
                  pl.BlockSpec((1, D_MODEL), lambda i, j: (0, 0)),
                  pl.BlockSpec((tn, D_MODEL), lambda i, j: (j, 0)),
                  pl.BlockSpec((tm, DH), lambda i, j: (i, 0)),
                  pl.BlockSpec((tm, DH), lambda i, j: (i, 0))],
        out_specs=[pl.BlockSpec((tm, tn), lambda i, j: (i, j)),
                   chunk_spec, chunk_spec,
                   pl.BlockSpec((tm, D_MODEL), lambda i, j: (i, 0))],
        out_shape=[jax.ShapeDtypeStruct((seq, ATT_W), BF16),
                   chunk_shape, chunk_shape,
                   jax.ShapeDtypeStruct((seq, D_MODEL), BF16)],
        scratch_shapes=[pltpu.VMEM((NSA_GROUPS, tm, DH), F32)],
        compiler_params=_params(("arbitrary", "arbitrary")),
        name="inproj_attn",
    )(x2, norm_w.reshape(1, D_MODEL), w_t, cosf, sinf)


SSM_TN = 1024
SRC_Z = NSA_HEADS * DH + 6 * NSA_GROUPS * DH + 3 * NSA_HEADS
SRC_BG = SRC_Z + 2 * SSM_D_INNER + 2 * SSM_GROUPS * SSM_STATE + SSM_HEADS
SSM_SPLIT = COL_GA // SSM_TN


def _nt_matmul_kernel(a_ref, w_ref, o_ref):
    o_ref[...] = _dot_nt(a_ref[...], w_ref[...].astype(BF16)).astype(o_ref.dtype)


def _proj_ssm(u, w_t):
    seq, k = u.shape
    tm = min(seq, 2048)

    def src_rows(i, j):
        jump = (SRC_BG - SRC_Z - SSM_TN * SSM_SPLIT) // 16
        return ((SRC_Z // 16 + (SSM_TN // 16) * j + jump * (j // SSM_SPLIT)) * 16, 0)

    return pl.pallas_call(
        _nt_matmul_kernel,
        grid=(seq // tm, SSM_W // SSM_TN),
        in_specs=[pl.BlockSpec((tm, k), lambda i, j: (i, 0)),
                  pl.BlockSpec((pl.Element(SSM_TN), pl.Element(k)), src_rows)],
        out_specs=pl.BlockSpec((tm, SSM_TN), lambda i, j: (i, j)),
        out_shape=jax.ShapeDtypeStruct((seq, SSM_W), BF16),
        compiler_params=_params(("arbitrary", "arbitrary")),
        name="inproj_ssm",
    )(u, w_t)


def _proj_small(u, w_small_t):
    seq, k = u.shape
    n = w_small_t.shape[0]
    tm = min(seq, 1024)
    return pl.pallas_call(
        _nt_matmul_kernel,
        grid=(seq // tm,),
        in_specs=[pl.BlockSpec((tm, k), lambda i: (i, 0)),
                  pl.BlockSpec((n, k), lambda i: (0, 0))],
        out_specs=pl.BlockSpec((tm, n), lambda i: (i, 0)),
        out_shape=jax.ShapeDtypeStruct((seq, n), F32),
        compiler_params=_params(("arbitrary",)),
        name="inproj_small",
    )(u, w_small_t)


def _compress_kernel(xk_ref, xv_ref, pk_ref, pv_ref, w1k_ref, w2k_ref, w1v_ref, w2v_ref,
                     cos_ref, sin_ref, kc_ref, vc_ref):
    half = CMP_STRIDE * DH
    ncp = xk_ref.shape[1]

    def comp(x_ref, p_ref, w1_ref, w2_ref):
        x = x_ref[0]
        wt = w1_ref[0:half, :]
        wb = w1_ref[half:2 * half, :]
        a = _dot(x, wt)
        b = _dot(x, wb)
        pb = _dot(p_ref[...], wt)[0:1] + _dot(p_ref[...], wb)[1:2]
        h = a + pltpu.roll(b, ncp - 1, 0) + pb
        h = h * jax.nn.sigmoid(h)
        return _dot(h.astype(BF16), w2_ref[...])

    kc = comp(xk_ref, pk_ref, w1k_ref, w2k_ref)
    kc_ref[0] = _rope(kc, cos_ref[...], sin_ref[...]).astype(BF16)
    vc_ref[0] = comp(xv_ref, pv_ref, w1v_ref, w2v_ref).astype(BF16)


def _compress(xk, xv, pk, pv, w1k, w2k, w1v, w2v, cos_c, sin_c):
    g, ncp, width = xk.shape
    full = lambda shape: pl.BlockSpec(shape, lambda i: (0,) * len(shape))
    per_g = pl.BlockSpec((1, ncp, width), lambda i: (i, 0, 0))
    out_g = pl.BlockSpec((1, ncp, DH), lambda i: (i, 0, 0))
    return pl.pallas_call(
        _compress_kernel,
        grid=(g,),
        in_specs=[per_g, per_g, full(pk.shape), full(pv.shape), full(w1k.shape), full(w2k.shape),
                  full(w1v.shape), full(w2v.shape), full(cos_c.shape), full(sin_c.shape)],
        out_specs=[out_g, out_g],
        out_shape=[jax.ShapeDtypeStruct((g, ncp, DH), BF16)] * 2,
        compiler_params=_params(("arbitrary",)),
        name="compress",
    )(xk, xv, pk, pv, w1k, w2k, w1v, w2v, cos_c, sin_c)


CMP_TQ = 256


def _cmp_attn_kernel(q_ref, kc_ref, vc_ref, mt_ref, oc_ref, mask_ref):
    i = pl.program_id(0)
    qbase = i * CMP_TQ
    ncp = kc_ref.shape[1]
    nsel = mt_ref.shape[0]
    row = lax.broadcasted_iota(jnp.int32, (CMP_TQ, ncp), 0)
    col = lax.broadcasted_iota(jnp.int32, (CMP_TQ, ncp), 1)
    bias1 = jnp.where((col * CMP_STRIDE + (CMP_BLOCK - 1)) <= qbase + row, 0.0, MASK_NEG)
    bias = jnp.concatenate([bias1] * NSA_REP, axis=0)
    has_key = jnp.concatenate([bias1[:, 0:1] == 0.0] * NSA_REP, axis=0)
    blk = lax.broadcasted_iota(jnp.int32, (nsel, CMP_TQ), 0)
    cur = (qbase + lax.broadcasted_iota(jnp.int32, (nsel, CMP_TQ), 1)) >> 6
    future = blk > cur
    forced = (blk == 0) | (blk == cur) | (blk == cur - 1)
    blk_f = blk.astype(F32)
    mt = mt_ref[...]
    importance = []
    for g in range(NSA_GROUPS):
        qg = jnp.concatenate(
            [q_ref[:, (g * NSA_REP + r) * DH:(g * NSA_REP + r + 1) * DH] for r in range(NSA_REP)], axis=0)
        s = _dot_nt(qg, kc_ref[g]) + bias
        p = jnp.exp2(s - jnp.max(s, axis=-1, keepdims=True)).astype(BF16)
        ov = _dot(p, jnp.concatenate([vc_ref[g], jnp.ones((ncp, LANES), BF16)], axis=1))
        inv = jnp.where(has_key, 1.0 / ov[:, DH:], 0.0)
        o = ov[:, :DH] * inv
        imp = None
        for r in range(NSA_REP):
            h = g * NSA_REP + r
            rows_r = slice(r * CMP_TQ, (r + 1) * CMP_TQ)
            oc_ref[:, h * DH:(h + 1) * DH] = o[rows_r].astype(oc_ref.dtype)
            part = _dot_nt(p[rows_r], mt) * inv[rows_r]
            imp = part if imp is None else imp + part
        importance.append(imp.T)
    v = jnp.where(future | forced, -1.0, jnp.stack(importance))
    for _ in range(SEL_TOPK - N_FORCED):
        mx = jnp.max(v, axis=1, keepdims=True)
        first = jnp.min(jnp.where(v == mx, blk_f, float(nsel)), axis=1, keepdims=True)
        v = jnp.where(blk_f == first, TAKEN, v)
    bias_t = jnp.where(forced | (v == TAKEN), 0.0, MASK_NEG)
    for g in range(NSA_GROUPS):
        mask_ref[g] = bias_t[g].T.astype(mask_ref.dtype)


def _cmp_attn(main, kc, vc, mt, seq):
    nsel = mt.shape[0]
    ncp = kc.shape[1]
    return pl.pallas_call(
        _cmp_attn_kernel,
        grid=(seq // CMP_TQ,),
        in_specs=[pl.BlockSpec((CMP_TQ, NSA_HEADS * DH), lambda i: (i, COL_Q // (NSA_HEADS * DH))),
                  pl.BlockSpec((NSA_GROUPS, ncp, DH), lambda i: (0, 0, 0)),
                  pl.BlockSpec((NSA_GROUPS, ncp, DH), lambda i: (0, 0, 0)),
                  pl.BlockSpec((nsel, ncp), lambda i: (0, 0))],
        out_specs=[pl.BlockSpec((CMP_TQ, NSA_HEADS * DH), lambda i: (i, 0)),
                   pl.BlockSpec((NSA_GROUPS, CMP_TQ, nsel), lambda i: (0, i, 0))],
        out_shape=[jax.ShapeDtypeStruct((seq, NSA_HEADS * DH), BF16),
                   jax.ShapeDtypeStruct((NSA_GROUPS, seq, nsel), BF16)],
        compiler_params=_params(("arbitrary",)),
        name="cmp_attn_topk",
    )(main, kc, vc, mt)


ATT_TQ = 512
ATT_TK = 512
WIN_SPAN = WINDOW + ATT_TQ // 2


def _nsa_attn_kernel(q_ref, mask_ref, ks_ref, vs_ref, kw_ref, vw_ref, oc_ref, gate_ref, *rest):
    n_w = (len(rest) - 4) // 2
    w_refs, o_ref, wb_refs = rest[:n_w], rest[n_w], rest[n_w + 1:2 * n_w + 1]
    qa_scr, acc_scr, m_scr = rest[2 * n_w + 1:]
    for src, dst in zip(w_refs, wb_refs):
        dst[...] = src[...].astype(dst.dtype)
    i = pl.program_id(1)
    qbase = i * ATT_TQ
    assert mask_ref.shape[2] == LANES, "the block-selection bias rides in one extra 128-wide contraction slab"
    bias = mask_ref[0]
    for r in range(NSA_REP):
        qa_scr[r] = jnp.concatenate([q_ref[:, r * DH:(r + 1) * DH], bias], axis=1)
    m_scr[...] = jnp.full(m_scr.shape, NEG_INF, F32)
    acc_scr[...] = jnp.zeros(acc_scr.shape, F32)
    krow = lax.broadcasted_iota(jnp.int32, (ATT_TK, LANES), 0)
    kblk = lax.broadcasted_iota(jnp.int32, (ATT_TK, LANES), 1)

    half = ATT_TQ // 2

    def key_tile(j):
        k0 = pl.multiple_of(j * ATT_TK, ATT_TK)
        onehot = jnp.where(((krow + k0) >> 6) == kblk, 1.0, 0.0).astype(BF16)
        k_aug = jnp.concatenate([ks_ref[pl.ds(k0, ATT_TK), :], onehot], axis=1)
        v_aug = jnp.concatenate([vs_ref[pl.ds(k0, ATT_TK), :], jnp.ones((ATT_TK, LANES), BF16)], axis=1)
        return k_aug, v_aug

    def online_update(r, rows, s, v_aug):
        m_prev = m_scr[r, rows, :]
        m_new = jnp.maximum(m_prev, jnp.max(s, axis=-1, keepdims=True))
        alpha = jnp.exp2(m_prev - m_new)
        p = jnp.exp2(s - jnp.concatenate([m_new] * (s.shape[1] // LANES), axis=1))
        acc_scr[r, rows, :] = (acc_scr[r, rows, :] * jnp.concatenate([alpha, alpha], axis=1)
                               + _dot(p.astype(BF16), v_aug))
        m_scr[r, rows, :] = m_new

    def step(j, diagonal):
        k_aug, v_aug = key_tile(j)
        if not diagonal:
            for r in range(NSA_REP):
                online_update(r, slice(0, ATT_TQ), _dot_nt(qa_scr[r], k_aug), v_aug)
            return
        r_ = lax.broadcasted_iota(jnp.int32, (half, half), 0)
        c_ = lax.broadcasted_iota(jnp.int32, (half, half), 1)
        tri_bias = jnp.where(c_ <= r_, 0.0, MASK_NEG)
        for r in range(NSA_REP):
            s0 = _dot_nt(qa_scr[r, 0:half, :], k_aug[:half]) + tri_bias
            online_update(r, slice(0, half), s0, v_aug[:half])
            s1 = _dot_nt(qa_scr[r, half:ATT_TQ, :], k_aug)
            s1 = jnp.concatenate([s1[:, :half], s1[:, half:] + tri_bias], axis=1)
            online_update(r, slice(half, ATT_TQ), s1, v_aug)

    def body(jj, carry):
        for u in range(4):
            step(4 * jj + u, False)
        return carry

    lax.fori_loop(0, i // 4, body, 0)
    done = (i // 4) * 4

    @pl.when((i & 2) != 0)
    def _():
        step(done, False)
        step(done + 1, False)

    @pl.when((i & 1) != 0)
    def _():
        step(i - 1, False)

    step(i, True)

    first_gate = SSM_HEADS + 3 * NSA_REP * pl.program_id(0)
    gates = jax.nn.sigmoid(pltpu.roll(gate_ref[...], SMALL_W - first_gate, 1))
    r_ = lax.broadcasted_iota(jnp.int32, (half, WIN_SPAN), 0)
    c_ = lax.broadcasted_iota(jnp.int32, (half, WIN_SPAN), 1)
    for hh in range(2):
        rows = slice(hh * half, (hh + 1) * half)
        q0 = qbase + hh * half
        start = pl.multiple_of(jnp.maximum(q0 - WINDOW, 0), half)
        kw = kw_ref[pl.ds(start, WIN_SPAN), :]
        vw_aug = jnp.concatenate([vw_ref[pl.ds(start, WIN_SPAN), :], jnp.ones((WIN_SPAN, LANES), BF16)], axis=1)
        diff = (q0 + r_) - (start + c_)
        win_bias = jnp.where((diff >= 0) & (diff < WINDOW), 0.0, MASK_NEG)
        for r in range(NSA_REP):
            s = _dot_nt(q_ref[rows, r * DH:(r + 1) * DH], kw) + win_bias
            p = jnp.exp2(s - jnp.max(s, axis=-1, keepdims=True))
            ow = _dot(p.astype(BF16), vw_aug)
            o_win = ow[:, :DH] / jnp.maximum(ow[:, DH:], TINY)
            acc = acc_scr[r, rows, :]
            o_sel = acc[:, :DH] / jnp.maximum(acc[:, DH:], TINY)
            g_r = gates[rows]
            o = (g_r[:, 3 * r:3 * r + 1] * oc_ref[rows, r * DH:(r + 1) * DH].astype(F32)
                 + g_r[:, 3 * r + 1:3 * r + 2] * o_sel
                 + g_r[:, 3 * r + 2:3 * r + 3] * o_win)
            o_ref[rows, r * DH:(r + 1) * DH] = o.astype(o_ref.dtype)


def _nsa_attn(main, small, mask, o_cmp, weights, seq):
    gw = NSA_REP * DH
    nsel = mask.shape[2]
    n_i = seq // ATT_TQ
    steps = NSA_GROUPS * n_i
    kv_spec = lambda col: pl.BlockSpec((seq, DH), lambda g, i: (0, col // DH + g), pipeline_mode=pl.Buffered(1))
    w_in_specs, w_out_specs, w_out_shapes = [], [], []
    for w in weights:
        _, rows, cols = w.shape
        slab = rows // steps
        assert slab * steps == rows and slab % 16 == 0
        w_in_specs.append(pl.BlockSpec((None, slab, cols), lambda g, i: (0, g * n_i + i, 0)))
        w_out_specs.append(pl.BlockSpec((slab, cols), lambda g, i: (g * n_i + i, 0)))
        w_out_shapes.append(jax.ShapeDtypeStruct((rows, cols), BF16))
    return pl.pallas_call(
        _nsa_attn_kernel,
        grid=(NSA_GROUPS, n_i),
        in_specs=[pl.BlockSpec((ATT_TQ, gw), lambda g, i: (i, COL_Q // gw + g)),
                  pl.BlockSpec((1, ATT_TQ, nsel), lambda g, i: (g, i, 0)),
                  kv_spec(COL_KS), kv_spec(COL_VS), kv_spec(COL_KW), kv_spec(COL_VW),
                  pl.BlockSpec((ATT_TQ, gw), lambda g, i: (i, g)),
                  pl.BlockSpec((ATT_TQ, SMALL_W), lambda g, i: (i, 0))] + w_in_specs,
        out_specs=[pl.BlockSpec((ATT_TQ, gw), lambda g, i: (i, g))] + w_out_specs,
        out_shape=[jax.ShapeDtypeStruct((seq, NSA_HEADS * DH), BF16)] + w_out_shapes,
        scratch_shapes=[pltpu.VMEM((NSA_REP, ATT_TQ, 2 * DH), BF16),
                        pltpu.VMEM((NSA_REP, ATT_TQ, 2 * DH), F32),
                        pltpu.VMEM((NSA_REP, ATT_TQ, LANES), F32)],
        compiler_params=_params(("arbitrary", "arbitrary")),
        name="nsa_sel_win_attn",
    )(main, mask, main, main, main, main, o_cmp, small, *weights)


HALO = 8
GROUP_W = SSM_D_INNER // SSM_GROUPS
HEADS_PER_GROUP = SSM_HEADS // SSM_GROUPS
MAMBA_ROWS = 2 * SSD_CHUNK


def _silu(x):
    h = 0.5 * x
    return h + h * jnp.tanh(h)


def _mamba_kernel(z_ref, xs_ref, b_ref, c_ref, dt_ref,
                  cwx_ref, cwb_ref, cwc_ref, cbx_ref, cbb_ref, cbc_ref,
                  dtb_ref, alog_ref, dfull_ref, nw_ref, expand2_ref, tril3_ref,
                  o_ref, state_scr, xs_ext, b_ext, c_ext):
    L = SSD_CHUNK

    @pl.when(pl.program_id(0) == 0)
    def _():
        state_scr[...] = jnp.zeros(state_scr.shape, F32)
        for carry in (xs_ext, b_ext, c_ext):
            carry[...] = jnp.zeros(carry.shape, F32)

    for sub in range(z_ref.shape[0] // L):
        _mamba_chunk(slice(sub * L, (sub + 1) * L), z_ref, xs_ref, b_ref, c_ref, dt_ref,
                     cwx_ref, cwb_ref, cwc_ref, cbx_ref, cbb_ref, cbc_ref,
                     dtb_ref, alog_ref, dfull_ref, nw_ref, expand2_ref, tril3_ref,
                     o_ref, state_scr, xs_ext, b_ext, c_ext)


def _mamba_chunk(rows, z_ref, xs_ref, b_ref, c_ref, dt_ref,
                 cwx_ref, cwb_ref, cwc_ref, cbx_ref, cbb_ref, cbc_ref,
                 dtb_ref, alog_ref, dfull_ref, nw_ref, expand2_ref, tril3_ref,
                 o_ref, state_scr, xs_ext, b_ext, c_ext):
    L = SSD_CHUNK

    def conv_silu(x_ref, carry, w_ref, bias_ref):
        x = x_ref[rows, :].astype(F32)
        first_row = lax.broadcasted_iota(jnp.int32, (HALO, x.shape[1]), 0) == 0
        r = None
        for k in range(CONV_WIDTH - 1):
            t = w_ref[k:k + 1, :] * x
            if r is not None:
                t = t + r
            rolled = pltpu.roll(t, 1, 0)
            top = jnp.where(first_row, carry[k:k + 1, :], rolled[0:HALO])
            carry[k:k + 1, :] = t[L - 1:L, :]
            r = jnp.concatenate([top, rolled[HALO:]], axis=0)
        return _silu(r + w_ref[CONV_WIDTH - 1:CONV_WIDTH, :] * x + bias_ref[...])

    xs = conv_silu(xs_ref, xs_ext, cwx_ref, cbx_ref)
    bm = conv_silu(b_ref, b_ext, cwb_ref, cbb_ref).astype(BF16)
    cm = conv_silu(c_ref, c_ext, cwc_ref, cbc_ref).astype(BF16)

    xdt_raw = dt_ref[rows, :] + dtb_ref[...]
    dt = jnp.maximum(xdt_raw, 0.0) + jnp.log1p(jnp.exp(-jnp.abs(xdt_raw)))
    adt = dt * (-jnp.exp(alog_ref[...]))
    acs = _dot(tril3_ref[...], jnp.concatenate(_split3(adt), axis=0)) * LOG2E
    acs_t = acs.T
    a_last = acs[L - 1:L, :]
    stacked = jnp.concatenate([dt, jnp.exp2(acs), jnp.exp2(a_last - acs)], axis=0)
    wide = _dot(jnp.concatenate(_split2(stacked), axis=1), expand2_ref[...])
    dt_w = wide[0:L]
    in_decay_w = wide[L:2 * L]
    out_decay_w = wide[2 * L:3 * L]

    xdt = xs * dt_w
    xst = xdt * out_decay_w
    causal = lax.broadcasted_iota(jnp.int32, (L, L), 0) >= lax.broadcasted_iota(jnp.int32, (L, L), 1)
    causal_bias = jnp.where(causal, 0.0, MASK_NEG)
    lane = lax.broadcasted_iota(jnp.int32, (L, LANES), 1)
    low_half = lane < SSM_HEAD_DIM
    chunk_decay = jnp.exp2(acs_t[:, L - 1:L])

    for g in range(SSM_GROUPS):
        gs = slice(g * GROUP_W, (g + 1) * GROUP_W)
        bg = bm[:, g * SSM_STATE:(g + 1) * SSM_STATE]
        cg = cm[:, g * SSM_STATE:(g + 1) * SSM_STATE]
        cb = _dot_nt(cg, bg)
        prev = state_scr[gs, :]
        y_off = _dot_nt(cg, prev.astype(BF16)) * in_decay_w[:, gs]
        pieces = []
        for hp in range(HEADS_PER_GROUP // 2):
            lhs = []
            for e in range(2):
                h = g * HEADS_PER_GROUP + 2 * hp + e
                decay = jnp.exp2(acs[:, h:h + 1] - acs_t[h:h + 1, :] + causal_bias)
                lhs.append((cb * decay).astype(BF16))
            slab = xdt[:, g * GROUP_W + hp * LANES:g * GROUP_W + (hp + 1) * LANES]
            rhs = jnp.concatenate([jnp.where(low_half, slab, 0.0), jnp.where(low_half, 0.0, slab)], axis=0)
            pieces.append(_dot(jnp.concatenate(lhs, axis=1), rhs.astype(BF16)))
        y_g = jnp.concatenate(pieces, axis=1) + y_off
        y_g = y_g + dfull_ref[:, gs] * xs[:, gs]
        y_g = y_g * _silu(z_ref[rows, gs].astype(F32))
        ms = jnp.mean(y_g * y_g, axis=-1, keepdims=True)
        o_ref[rows, gs] = (y_g * lax.rsqrt(ms + EPS) * nw_ref[:, gs]).astype(o_ref.dtype)
        new = _dot(xst[:, gs].T.astype(BF16), bg)
        cd = jnp.broadcast_to(chunk_decay[g * HEADS_PER_GROUP:(g + 1) * HEADS_PER_GROUP], (HEADS_PER_GROUP, LANES))
        cd = jnp.broadcast_to(cd[:, None, :], (HEADS_PER_GROUP, SSM_HEAD_DIM, LANES)).reshape(GROUP_W, LANES)
        state_scr[gs, :] = prev * cd + new


def _mamba(main, small, cwx, cwb, cwc, cbx, cbb, cbc, dtb, alog, dfull, nw, expand2, tril3, seq):
    L = MAMBA_ROWS
    bw = SSM_GROUPS * SSM_STATE
    full = lambda a: pl.BlockSpec(a.shape, lambda c: (0,) * a.ndim)
    consts = [cwx, cwb, cwc, cbx, cbb, cbc, dtb, alog, dfull, nw, expand2, tril3]
    return pl.pallas_call(
        _mamba_kernel,
        grid=(seq // L,),
        in_specs=[pl.BlockSpec((L, SSM_D_INNER), lambda c: (c, COL_Z // SSM_D_INNER)),
                  pl.BlockSpec((L, SSM_D_INNER), lambda c: (c, COL_XS // SSM_D_INNER)),
                  pl.BlockSpec((L, bw), lambda c: (c, COL_B // bw)),
                  pl.BlockSpec((L, bw), lambda c: (c, COL_C // bw)),
                  pl.BlockSpec((L, LANES), lambda c: (c, 0))] + [full(a) for a in consts],
        out_specs=pl.BlockSpec((L, SSM_D_INNER), lambda c: (c, 0)),
        out_shape=jax.ShapeDtypeStruct((seq, SSM_D_INNER), BF16),
        scratch_shapes=[pltpu.VMEM((SSM_D_INNER, SSM_STATE), F32),
                        pltpu.VMEM((HALO, SSM_D_INNER), F32),
                        pltpu.VMEM((HALO, bw), F32),
                        pltpu.VMEM((HALO, bw), F32)],
        compiler_params=_params(("arbitrary",)),
        name="mamba2_ssd",
    )(main, main, main, main, small, *consts)


def _merge_kernel(a_ref, b_ref, pa_ref, pb_ref, ga_ref, gb_ref, o_ref):
    ya = _dot(a_ref[...], pa_ref[...])
    yb = _dot(b_ref[...], pb_ref[...])
    o = jax.nn.sigmoid(ga_ref[...].astype(F32)) * ya + jax.nn.sigmoid(gb_ref[...].astype(F32)) * yb
    o_ref[...] = o.astype(o_ref.dtype)


def _merge(o_nsa, o_ssm, pa, pb, main, seq):
    tm = min(seq, 512)
    tn = 1024
    return pl.pallas_call(
        _merge_kernel,
        grid=(seq // tm, D_MODEL // tn),
        in_specs=[pl.BlockSpec((tm, NSA_HEADS * DH), lambda i, j: (i, 0)),
                  pl.BlockSpec((tm, SSM_D_INNER), lambda i, j: (i, 0)),
                  pl.BlockSpec((NSA_HEADS * DH, tn), lambda i, j: (0, j)),
                  pl.BlockSpec((SSM_D_INNER, tn), lambda i, j: (0, j)),
                  pl.BlockSpec((tm, tn), lambda i, j: (i, COL_GA // tn + j)),
                  pl.BlockSpec((tm, tn), lambda i, j: (i, COL_GB // tn + j))],
        out_specs=pl.BlockSpec((tm, tn), lambda i, j: (i, j)),
        out_shape=jax.ShapeDtypeStruct((seq, D_MODEL), BF16),
        compiler_params=_params(("arbitrary", "arbitrary")),
        name="gated_merge",
    )(o_nsa, o_ssm, pa, pb, main, main)


def _outproj_kernel(m_ref, w_ref, x_ref, nw_ref, h_ref, hn_ref):
    h = x_ref[...] + _dot(m_ref[...], w_ref[...])
    h_ref[...] = h
    ms = jnp.mean(h * h, axis=-1, keepdims=True)
    hn_ref[...] = (h * lax.rsqrt(ms + EPS) * nw_ref[...]).astype(hn_ref.dtype)


def _outproj(merged, w_out, x2, norm_w, seq):
    tm = min(seq, 512)
    row = pl.BlockSpec((tm, D_MODEL), lambda i: (i, 0))
    return pl.pallas_call(
        _outproj_kernel,
        grid=(seq // tm,),
        in_specs=[row, pl.BlockSpec((D_MODEL, D_MODEL), lambda i: (0, 0), pipeline_mode=pl.Buffered(1)), row,
                  pl.BlockSpec((1, D_MODEL), lambda i: (0, 0))],
        out_specs=[row, row],
        out_shape=[jax.ShapeDtypeStruct((seq, D_MODEL), F32), jax.ShapeDtypeStruct((seq, D_MODEL), BF16)],
        compiler_params=_params(("arbitrary",)),
        name="outproj_residual_norm",
    )(merged, w_out, x2, norm_w.reshape(1, D_MODEL))


def _mlp_kernel(hn_ref, wu_ref, wd_ref, h_ref, nw_ref, o_ref, acc_scr):
    f = pl.program_id(1)

    @pl.when(f == 0)
    def _():
        acc_scr[...] = jnp.zeros(acc_scr.shape, F32)

    up = jnp.maximum(_dot(hn_ref[...], wu_ref[...]), 0.0)
    acc_scr[...] += _dot((up * up).astype(BF16), wd_ref[...])

    @pl.when(f == pl.num_programs(1) - 1)
    def _():
        h = h_ref[...] + acc_scr[...]
        ms = jnp.mean(h * h, axis=-1, keepdims=True)
        o_ref[...] = h * lax.rsqrt(ms + EPS) * nw_ref[...]


def _mlp(hn, w_up, w_down, h1, norm_w, seq):
    tm = min(seq, 512)
    tf = 1024
    row = lambda i, f: (i, 0)
    return pl.pallas_call(
        _mlp_kernel,
        grid=(seq // tm, D_FF // tf),
        in_specs=[pl.BlockSpec((tm, D_MODEL), row),
                  pl.BlockSpec((D_MODEL, tf), lambda i, f: (0, f)),
                  pl.BlockSpec((tf, D_MODEL), lambda i, f: (f, 0)),
                  pl.BlockSpec((tm, D_MODEL), row),
                  pl.BlockSpec((1, D_MODEL), lambda i, f: (0, 0))],
        out_specs=pl.BlockSpec((tm, D_MODEL), row),
        out_shape=jax.ShapeDtypeStruct((seq, D_MODEL), F32),
        scratch_shapes=[pltpu.VMEM((tm, D_MODEL), F32)],
        compiler_params=_params(("arbitrary", "arbitrary")),
        name="mlp_final_norm",
    )(hn, w_up, w_down, h1, norm_w.reshape(1, D_MODEL))


def _cmp_to_sel(seq):
    n_cmp = (seq - CMP_BLOCK) // CMP_STRIDE + 1
    n_sel = seq // SEL_BLOCK
    c_start = np.arange(n_cmp) * CMP_STRIDE
    s_start = np.arange(n_sel) * SEL_BLOCK
    overlap = np.clip(np.minimum(c_start[:, None] + CMP_BLOCK, s_start[None, :] + SEL_BLOCK)
                      - np.maximum(c_start[:, None], s_start[None, :]), 0, None)
    m = np.zeros((seq // CMP_STRIDE, LANES), np.float32)
    m[:n_cmp, :n_sel] = overlap / CMP_STRIDE
    return jnp.asarray(m.T, BF16)


def kernel(x, positions, norm_mix_w, w_in, cmp_pos_k, cmp_pos_v, cmp_k_w1, cmp_k_w2, cmp_v_w1, cmp_v_w2, conv_w, conv_b, dt_bias, a_log, ssm_d, ssm_norm_w, w_proj_nsa, w_proj_ssm, w_out, norm_mlp_w, w_up, w_down, norm_final_w):
    bsz, seq, _ = x.shape
    assert bsz == 1 and seq % 1024 == 0 and SEL_TOPK <= seq // SEL_BLOCK <= LANES
    assert w_in.shape[0] == 1, "one layer"
    x2 = x.reshape(seq, D_MODEL)

    o_gate = NSA_HEADS * DH + 6 * NSA_GROUPS * DH
    o_z = o_gate + 3 * NSA_HEADS
    o_dt = o_z + 2 * SSM_D_INNER + 2 * SSM_GROUPS * SSM_STATE
    o_bg = o_dt + SSM_HEADS
    assert o_z == SRC_Z and o_bg == SRC_BG and o_gate == ATT_W
    w_t = jnp.swapaxes(w_in, 1, 2)[0]
    n_gate = 3 * NSA_HEADS
    w_small_t = jnp.concatenate([w_t[o_dt:o_dt + SSM_HEADS], w_t[o_gate:o_gate + n_gate],
                                 jnp.zeros((SMALL_W - SSM_HEADS - n_gate, D_MODEL), F32)], axis=0)

    pos = positions.reshape(seq)
    cosf, sinf = _rope_tables(pos)
    att, kc_chunks, vc_chunks, u = _inproj(x2, norm_mix_w[0], w_t, cosf, sinf)
    ssm = _proj_ssm(u, w_t)
    small = _proj_small(u, w_small_t)

    ncp = seq // CMP_STRIDE
    def pos2(p):
        p = p.reshape(2, CMP_STRIDE * DH)
        return jnp.concatenate([p, jnp.zeros((14, CMP_STRIDE * DH), F32)], axis=0).astype(BF16)
    pos_end = pos[CMP_BLOCK - 1::CMP_STRIDE]
    cos_c, sin_c = _rope_tables(jnp.concatenate([pos_end, jnp.zeros((ncp - pos_end.shape[0],), pos.dtype)]))
    kc, vc = _compress(kc_chunks, vc_chunks, pos2(cmp_pos_k[0]), pos2(cmp_pos_v[0]),
                       cmp_k_w1[0].astype(BF16), cmp_k_w2[0].astype(BF16),
                       cmp_v_w1[0].astype(BF16), cmp_v_w2[0].astype(BF16), cos_c, sin_c)

    o_cmp, sel_bias = _cmp_attn(att, kc, vc, _cmp_to_sel(seq), seq)
    o_nsa, pa, pb, wo, wu, wd = _nsa_attn(att, small, sel_bias, o_cmp,
                                          (w_proj_nsa, w_proj_ssm, w_out, w_up, w_down), seq)

    cw = conv_w[0]
    cb_ = conv_b[0].reshape(1, -1)
    lane_pad = lambda a: jnp.concatenate([a.reshape(1, -1), jnp.zeros((1, LANES - a.shape[-1]), F32)], axis=1)
    head_of = np.arange(SSM_D_INNER) // SSM_HEAD_DIM
    expand = np.arange(LANES)[:, None] == head_of[None, :]
    expand2 = jnp.asarray(np.concatenate([expand, expand], axis=0), BF16)
    tril = np.tril(np.ones((SSD_CHUNK, SSD_CHUNK), np.float32))
    tril3 = jnp.asarray(np.concatenate([tril, tril, tril], axis=1), BF16)
    o_ssm = _mamba(ssm, small, cw[:, :4096], cw[:, 4096:5120], cw[:, 5120:],
                   cb_[:, :4096], cb_[:, 4096:5120], cb_[:, 5120:],
                   lane_pad(dt_bias[0]), lane_pad(a_log[0]),
                   jnp.repeat(ssm_d[0], SSM_HEAD_DIM).reshape(1, -1), ssm_norm_w[0].reshape(1, -1),
                   expand2, tril3, seq)

    merged = _merge(o_nsa, o_ssm, pa, pb, ssm, seq)
    h1, hn = _outproj(merged, wo, x2, norm_mlp_w[0], seq)
    out = _mlp(hn, wu, wd, h1, norm_final_w, seq)
    return out.reshape(bsz, seq, D_MODEL)
```
